```python
import math
import jax, jax.numpy as jnp
from jax import lax
import numpy as np

D_MODEL = 1024
BATCH = 4
SEQ = 4096
DEPTH = 2

GRID_W = 64
CTX_LEN = 256
EPS = 1e-6
N_BRANCH = 4
BRANCH_W = D_MODEL // 2

SSD_HEADS = 8
SSD_HEAD_DIM = BRANCH_W // SSD_HEADS
SSD_INNER = SSD_HEADS * SSD_HEAD_DIM
SSD_GROUPS = 2
SSD_STATE = 128
SSD_CHUNK = 128
SSD_XBC = SSD_INNER + 2 * SSD_GROUPS * SSD_STATE
CONV_W = 4

GLA_HEADS = 4
GLA_DK = BRANCH_W // GLA_HEADS
GLA_DV = GLA_DK
GLA_RANK = 16
GLA_GATE_NORM = 16.0
GLA_CHUNK = 64

LRU_W = BRANCH_W
LRU_BLOCKS = 8
LRU_BLOCK = LRU_W // LRU_BLOCKS
LRU_C = 8.0

ATT_HEADS = 4
ATT_KV_HEADS = 2
ATT_HEAD_DIM = BRANCH_W // ATT_HEADS
ATT_BLOCK = 128
ROPE_THETA = 10000.0

MOE_GROUPS = 4
MOE_PER_GROUP = 4
MOE_EXPERTS = MOE_GROUPS * MOE_PER_GROUP
MOE_TOPK = 2
MOE_FF = D_MODEL // 2

IN_SIZES = (
    SSD_INNER,
    SSD_XBC,
    2 * SSD_HEADS,
    GLA_HEADS * GLA_DK,
    GLA_HEADS * GLA_DK,
    GLA_HEADS * GLA_DV,
    2 * GLA_RANK,
    GLA_HEADS * GLA_DV,
    LRU_W,
    LRU_W,
    ATT_HEADS * ATT_HEAD_DIM,
    ATT_KV_HEADS * ATT_HEAD_DIM,
    ATT_KV_HEADS * ATT_HEAD_DIM,
    N_BRANCH * D_MODEL,
)
IN_WIDTH = sum(IN_SIZES)

kernel_name = 'hybrid_ssd_gla_rglru_gqa_hmoe_diffusion_block'

F32 = jnp.float32


def rms_norm(x, g):
    xf = x.astype(F32)
    y = xf * lax.rsqrt(jnp.mean(xf * xf, axis=-1, keepdims=True) + EPS)
    return (y * g.astype(F32)).astype(x.dtype)


def split_in(p):
    out, o = [], 0
    for s in IN_SIZES:
        out.append(p[..., o:o + s])
        o += s
    return out


def dw_conv(x, w, b):
    k = w.shape[0]
    y = lax.conv_general_dilated(x, w[:, None, :].astype(x.dtype), window_strides=(1,),
                                 padding=[(k // 2, k - 1 - k // 2)],
                                 dimension_numbers=('NWC', 'WIO', 'NWC'),
                                 feature_group_count=x.shape[-1])
    return y + b.astype(x.dtype)


def ssd_scan(xh, dt, a, bm, cm, h0, with_y):
    bsz, L, H, P = xh.shape
    G, N = bm.shape[2], bm.shape[3]
    E = H // G
    Q = min(SSD_CHUNK, L)
    nc = L // Q
    dtc = dt.astype(F32).reshape(bsz, nc, Q, G, E)
    dtx = xh.astype(F32).reshape(bsz, nc, Q, G, E, P) * dtc[..., None]
    b = bm.astype(F32).reshape(bsz, nc, Q, G, N)
    acum = jnp.cumsum(dtc * a.astype(F32).reshape(G, E), axis=2)
    a_last = acum[:, :, -1]
    states = jnp.einsum('bclgn,bclge,bclgep->bcgepn', b, jnp.exp(a_last[:, :, None] - acum), dtx)

    def step(h, inp):
        s, al = inp
        return jnp.exp(al)[..., None, None] * h + s, h

    h_fin, h_starts = lax.scan(step, h0.astype(F32).reshape(bsz, G, E, P, N),
                               (jnp.moveaxis(states, 1, 0), jnp.moveaxis(a_last, 1, 0)))
    h_fin = h_fin.reshape(bsz, H, P, N)
    if not with_y:
        return None, h_fin
    h_starts = jnp.moveaxis(h_starts, 0, 1)
    c = cm.astype(F32).reshape(bsz, nc, Q, G, N)
    seg = acum[:, :, :, None] - acum[:, :, None]
    causal = jnp.tril(jnp.ones((Q, Q), bool))[:, :, None, None]
    lmat = jnp.exp(jnp.where(causal, seg, -jnp.inf))
    cb = jnp.einsum('bcign,bcjgn->bcijg', c, b)
    y_diag = jnp.einsum('bcijge,bcjgep->bcigep', cb[..., None] * lmat, dtx)
    y_off = jnp.einsum('bcign,bcige,bcgepn->bcigep', c, jnp.exp(acum), h_starts)
    return (y_diag + y_off).reshape(bsz, L, H, P), h_fin


def ssd_seq(z, xbc, dt_raw, lp, h0_f, h0_b, with_y):
    bsz, L, _ = xbc.shape
    xbc = jax.nn.silu(dw_conv(xbc, lp['ssd_conv_w'], lp['ssd_conv_b']))
    xs = xbc[..., :SSD_INNER]
    bm = xbc[..., SSD_INNER:SSD_INNER + SSD_GROUPS * SSD_STATE].reshape(bsz, L, SSD_GROUPS, SSD_STATE)
    cm = xbc[..., SSD_INNER + SSD_GROUPS * SSD_STATE:].reshape(bsz, L, SSD_GROUPS, SSD_STATE)
    xh = xs.reshape(bsz, L, SSD_HEADS, SSD_HEAD_DIM)
    dt = jax.nn.softplus(dt_raw.astype(F32).reshape(bsz, L, 2, SSD_HEADS) + lp['ssd_dt_bias'].astype(F32))
    a = -jnp.exp(lp['ssd_a_log'].astype(F32))
    yf, hf = ssd_scan(xh, dt[:, :, 0], a[0], bm, cm, h0_f, with_y)
    yb, hb = ssd_scan(xh[:, ::-1], dt[:, ::-1, 1], a[1], bm[:, ::-1], cm[:, ::-1], h0_b, with_y)
    if not with_y:
        return None, hf, hb
    y = yf + yb[:, ::-1] + lp['ssd_d'].astype(F32)[:, None] * xh.astype(F32)
    y = y.reshape(bsz, L, SSD_INNER) * jax.nn.silu(z.astype(F32))
    return rms_norm(y, lp['ssd_norm']).astype(z.dtype), hf, hb


def gla_scan(q, k, v, g, s0, with_y):
    bsz, L, H, K = k.shape
    V = v.shape[-1]
    Q = min(GLA_CHUNK, L)
    nc = L // Q
    k = k.reshape(bsz, nc, Q, H, K)
    v = v.reshape(bsz, nc, Q, H, V)
    gc = jnp.cumsum(g.reshape(bsz, nc, Q, H, K), axis=2)
    g_last = gc[:, :, -1]
    chunk_kv = jnp.einsum('bclhk,bclhv->bchkv', k * jnp.exp(g_last[:, :, None] - gc), v)

    def step(s, inp):
        kv, gl = inp
        return jnp.exp(gl)[..., None] * s + kv, s

    s_fin, s_starts = lax.scan(step, s0.astype(F32),
                               (jnp.moveaxis(chunk_kv, 1, 0), jnp.moveaxis(g_last, 1, 0)))
    if not with_y:
        return None, s_fin
    q = q.reshape(bsz, nc, Q, H, K)
    s_starts = jnp.moveaxis(s_starts, 0, 1)
    o_inter = jnp.einsum('bclhk,bchkv->bclhv', q * jnp.exp(gc), s_starts)
    g_ref = gc[:, :, Q // 2:Q // 2 + 1]
    att = jnp.einsum('bcihk,bcjhk->bchij', q * jnp.exp(gc - g_ref), k * jnp.exp(g_ref - gc))
    att = jnp.where(jnp.tril(jnp.ones((Q, Q), bool)), att, 0.0)
    o_intra = jnp.einsum('bchij,bcjhv->bcihv', att, v)
    return (o_inter + o_intra).reshape(bsz, L, H, V), s_fin


def gla_seq(q, k, v, g1, r, lp, s0_f, s0_b, with_y):
    bsz, L, _ = k.shape
    q = q.astype(F32).reshape(bsz, L, GLA_HEADS, GLA_DK) * (GLA_DK ** -0.5)
    k = k.astype(F32).reshape(bsz, L, GLA_HEADS, GLA_DK)
    v = v.astype(F32).reshape(bsz, L, GLA_HEADS, GLA_DV)
    logit = jnp.einsum('bldr,drk->bldk', g1.astype(F32).reshape(bsz, L, 2, GLA_RANK),
                       lp['gla_g2'].astype(F32)) + lp['gla_gb'].astype(F32)
    g = (jax.nn.log_sigmoid(logit) / GLA_GATE_NORM).reshape(bsz, L, 2, GLA_HEADS, GLA_DK)
    of, sf = gla_scan(q, k, v, g[:, :, 0], s0_f, with_y)
    ob, sb = gla_scan(q[:, ::-1], k[:, ::-1], v[:, ::-1], g[:, ::-1, 1], s0_b, with_y)
    if not with_y:
        return None, sf, sb
    o = rms_norm(of + ob[:, ::-1], lp['gla_norm'].reshape(GLA_HEADS, GLA_DV))
    y = o.reshape(bsz, L, GLA_HEADS * GLA_DV) * jax.nn.silu(r.astype(F32))
    return y.astype(r.dtype), sf, sb


def lru_gates(u, lp, d):
    bsz, L, W = u.shape
    ub = u.reshape(bsz, L, LRU_BLOCKS, LRU_BLOCK)
    r = jax.nn.sigmoid(jnp.einsum('blnc,ncd->blnd', ub, lp['lru_wa'][d].astype(F32)).reshape(bsz, L, W)
                       + lp['lru_ba'][d].astype(F32))
    i = jax.nn.sigmoid(jnp.einsum('blnc,ncd->blnd', ub, lp['lru_wx'][d].astype(F32)).reshape(bsz, L, W)
                       + lp['lru_bx'][d].astype(F32))
    log_a = -LRU_C * jax.nn.softplus(-lp['lru_lambda'][d].astype(F32)) * r
    return log_a, u * i * jnp.sqrt(-jnp.expm1(2.0 * log_a))


def lru_scan(v, log_a, h0):
    def comb(left, right):
        al, bl = left
        ar, br = right
        return al * ar, ar * bl + br
    a_cum, h = lax.associative_scan(comb, (jnp.exp(log_a), v), axis=1)
    return h + a_cum * h0[:, None]


def lru_seq(xb, gb, lp, h0_f, h0_b, with_y):
    u = dw_conv(xb, lp['lru_conv_w'], lp['lru_conv_b']).astype(F32)
    la_f, v_f = lru_gates(u, lp, 0)
    la_b, v_b = lru_gates(u[:, ::-1], lp, 1)
    hf = lru_scan(v_f, la_f, h0_f)
    hb = lru_scan(v_b, la_b, h0_b)
    if not with_y:
        return None, hf[:, -1], hb[:, -1]
    y = (hf + hb[:, ::-1]) * jax.nn.gelu(gb.astype(F32))
    return y.astype(xb.dtype), hf[:, -1], hb[:, -1]


def rope_1d(x, pos):
    f = x.shape[-1] // 2
    inv = ROPE_THETA ** (-jnp.arange(f, dtype=F32) / f)
    ang = pos.astype(F32)[:, None] * inv
    cos, sin = jnp.cos(ang)[None, :, None], jnp.sin(ang)[None, :, None]
    x1, x2 = x[..., :f].astype(F32), x[..., f:].astype(F32)
    return jnp.concatenate([x1 * cos - x2 * sin, x1 * sin + x2 * cos], axis=-1)


def rope_2d(x, row, col):
    h = x.shape[-1] // 2
    return jnp.concatenate([rope_1d(x[..., :h], row), rope_1d(x[..., h:], col)], axis=-1)


def attn_softmax(q, k, v):
    s = jnp.einsum('bqgrd,bkgd->bgrqk', q, k) * (ATT_HEAD_DIM ** -0.5)
    p = jax.nn.softmax(s, axis=-1)
    return jnp.einsum('bgrqk,bkgd->bqgrd', p, v)


def attn_mixer(qc, kc, vc, ql, kl, vl, lp, row, col, with_ctx):
    bsz, L, _ = ql.shape
    C = kc.shape[1]
    rep = ATT_HEADS // ATT_KV_HEADS
    heads = lambda t, n: t.reshape(t.shape[0], t.shape[1], n, ATT_HEAD_DIM)
    kc_h = rms_norm(heads(kc, ATT_KV_HEADS), lp['att_knorm']).astype(F32)
    vc_h = heads(vc, ATT_KV_HEADS).astype(F32)
    ql_h = rope_2d(rms_norm(heads(ql, ATT_HEADS), lp['att_qnorm']), row, col)
    kl_h = rope_2d(rms_norm(heads(kl, ATT_KV_HEADS), lp['att_knorm']), row, col)
    k_all = jnp.concatenate([kc_h, kl_h], axis=1)
    v_all = jnp.concatenate([vc_h, heads(vl, ATT_KV_HEADS).astype(F32)], axis=1)
    nb = L // ATT_BLOCK
    q_blocks = jnp.moveaxis(ql_h.reshape(bsz, nb, ATT_BLOCK, ATT_KV_HEADS, rep, ATT_HEAD_DIM), 1, 0)
    o = lax.map(lambda qb: attn_softmax(qb, k_all, v_all), q_blocks)
    y_lat = jnp.moveaxis(o, 0, 1).reshape(bsz, L, ATT_HEADS * ATT_HEAD_DIM).astype(ql.dtype)
    if not with_ctx:
        return None, y_lat
    qc_h = rms_norm(heads(qc, ATT_HEADS), lp['att_qnorm']).astype(F32)
    oc = attn_softmax(qc_h.reshape(bsz, C, ATT_KV_HEADS, rep, ATT_HEAD_DIM), kc_h, vc_h)
    return oc.reshape(bsz, C, ATT_HEADS * ATT_HEAD_DIM).astype(qc.dtype), y_lat


def merge_branches(ys, gate_raw, lp):
    gates = jax.nn.sigmoid(gate_raw)
    acc = sum(gates[..., n * D_MODEL:(n + 1) * D_MODEL] * (y @ lp['w_branch'][n]) for n, y in enumerate(ys))
    return acc @ lp['w_out']


def hier_moe(h, lp):
    T = h.shape[0]
    hf = h.astype(F32)
    lg = hf @ lp['router_wg'].astype(F32) + lp['router_bg'].astype(F32)
    pg = jax.nn.softmax(lg, axis=-1)
    _, gi = lax.top_k(lg, 1)
    gsel = jax.nn.one_hot(gi[:, 0], MOE_GROUPS, dtype=F32)
    le = (hf @ lp['router_we'].astype(F32) + lp['router_be'].astype(F32)).reshape(T, MOE_GROUPS, MOE_PER_GROUP)
    pe = jax.nn.softmax(jnp.einsum('tg,tge->te', gsel, le), axis=-1)
    tv, ti = lax.top_k(pe, MOE_TOPK)
    tv = tv / jnp.sum(tv, axis=-1, keepdims=True)
    w_in_grp = jnp.einsum('tk,tke->te', tv, jax.nn.one_hot(ti, MOE_PER_GROUP, dtype=F32))
    w = ((gsel * pg)[:, :, None] * w_in_grp[:, None, :]).reshape(T, MOE_EXPERTS).astype(h.dtype)
    out = jnp.zeros_like(h)
    for e in range(MOE_EXPERTS):
        a = jax.nn.silu(h @ lp['exp_w1'][e]) * (h @ lp['exp_w3'][e])
        out = out + w[:, e:e + 1] * (a @ lp['exp_w2'][e])
    return out


def hybrid_layer(xl, xc, mod_l, mod_c, lp, row, col, with_ctx):
    bsz, _, D = xl.shape
    C = xc.shape[1]
    sh1, sc1, g1, sh2, sc2, g2 = jnp.split(mod_l[:, None, :], 6, axis=-1)
    csh1, csc1, cg1, csh2, csc2, cg2 = jnp.split(mod_c, 6, axis=-1)
    hl = rms_norm(xl, lp['norm1']) * (1 + sc1) + sh1
    hc = rms_norm(xc, lp['norm1']) * (1 + csc1) + csh1
    pl = split_in(hl @ lp['w_in'])
    pc = split_in(hc @ lp['w_in'])
    z_ssd = jnp.zeros((bsz, SSD_HEADS, SSD_HEAD_DIM, SSD_STATE), F32)
    z_gla = jnp.zeros((bsz, GLA_HEADS, GLA_DK, GLA_DV), F32)
    z_lru = jnp.zeros((bsz, LRU_W), F32)
    yc_a, sf, sb = ssd_seq(pc[0], pc[1], pc[2], lp, z_ssd, z_ssd, with_ctx)
    yl_a, _, _ = ssd_seq(pl[0], pl[1], pl[2], lp, sf, sb, True)
    yc_b, sf, sb = gla_seq(pc[3], pc[4], pc[5], pc[6], pc[7], lp, z_gla, z_gla, with_ctx)
    yl_b, _, _ = gla_seq(pl[3], pl[4], pl[5], pl[6], pl[7], lp, sf, sb, True)
    yc_c, sf, sb = lru_seq(pc[8], pc[9], lp, z_lru, z_lru, with_ctx)
    yl_c, _, _ = lru_seq(pl[8], pl[9], lp, sf, sb, True)
    yc_d, yl_d = attn_mixer(pc[10], pc[11], pc[12], pl[10], pl[11], pl[12], lp, row, col, with_ctx)
    xl = xl + g1 * merge_branches((yl_a, yl_b, yl_c, yl_d), pl[13], lp)
    hl2 = None
    if with_ctx:
        xc = xc + cg1 * merge_branches((yc_a, yc_b, yc_c, yc_d), pc[13], lp)
        hl2 = rms_norm(xl, lp['norm2']) * (1 + sc2) + sh2
        hc2 = rms_norm(xc, lp['norm2']) * (1 + csc2) + csh2
        f = hier_moe(jnp.concatenate([hc2.reshape(-1, D), hl2.reshape(-1, D)], axis=0), lp)
        xc = xc + cg2 * f[:bsz * C].reshape(xc.shape)
        xl = xl + g2 * f[bsz * C:].reshape(xl.shape)
        return xl, xc
    hl2 = rms_norm(xl, lp['norm2']) * (1 + sc2) + sh2
    xl = xl + g2 * hier_moe(hl2.reshape(-1, D), lp).reshape(xl.shape)
    return xl, None


def setup_inputs(seed: int = 0) -> dict:
    key = jax.random.key(seed)
    ks = jax.random.split(key, 40)
    D, Ld = D_MODEL, DEPTH
    nrm = lambda i, shape, s: jax.random.normal(ks[i], shape, jnp.float32) * s
    uni = lambda i, shape, lo, hi: jax.random.uniform(ks[i], shape, jnp.float32, lo, hi)
    dt0 = jnp.exp(uni(11, (Ld, 2, SSD_HEADS), math.log(1e-3), math.log(1e-1)))
    a_c = uni(24, (Ld, 2, LRU_W), 0.9, 0.999) ** (1.0 / LRU_C)
    return {
        'x': nrm(0, (BATCH, SEQ, D), 1.0),
        'c': nrm(1, (BATCH, D), 1.0),
        'ctx': nrm(2, (BATCH, CTX_LEN, D), 1.0),
        'c_ctx': nrm(3, (D,), 1.0),
        'ada_w': nrm(4, (Ld, D, 6 * D), 0.5 * D ** -0.5),
        'ada_b': nrm(5, (Ld, 6 * D), 0.02),
        'norm1': 1.0 + nrm(6, (Ld, D), 0.02),
        'norm2': 1.0 + nrm(7, (Ld, D), 0.02),
        'w_in': nrm(8, (Ld, D, IN_WIDTH), D ** -0.5),
        'ssd_conv_w': nrm(9, (Ld, CONV_W, SSD_XBC), CONV_W ** -0.5),
        'ssd_conv_b': nrm(10, (Ld, SSD_XBC), 0.02),
        'ssd_dt_bias': dt0 + jnp.log(-jnp.expm1(-dt0)),
        'ssd_a_log': jnp.log(uni(12, (Ld, 2, SSD_HEADS), 1.0, 16.0)),
        'ssd_d': 1.0 + nrm(13, (Ld, SSD_HEADS), 0.1),
        'ssd_norm': 1.0 + nrm(14, (Ld, SSD_INNER), 0.02),
        'gla_g2': nrm(15, (Ld, 2, GLA_RANK, GLA_HEADS * GLA_DK), GLA_RANK ** -0.5),
        'gla_gb': nrm(16, (Ld, 2, GLA_HEADS * GLA_DK), 0.1),
        'gla_norm': 1.0 + nrm(17, (Ld, GLA_HEADS * GLA_DV), 0.02),
        'lru_conv_w': nrm(18, (Ld, CONV_W, LRU_W), CONV_W ** -0.5),
        'lru_conv_b': nrm(19, (Ld, LRU_W), 0.02),
        'lru_wa': nrm(20, (Ld, 2, LRU_BLOCKS, LRU_BLOCK, LRU_BLOCK), LRU_BLOCK ** -0.5),
        'lru_ba': nrm(21, (Ld, 2, LRU_W), 0.02),
        'lru_wx': nrm(22, (Ld, 2, LRU_BLOCKS, LRU_BLOCK, LRU_BLOCK), LRU_BLOCK ** -0.5),
        'lru_bx': nrm(23, (Ld, 2, LRU_W), 0.02),
        'lru_lambda': jnp.log(a_c) - jnp.log1p(-a_c),
        'att_qnorm': 1.0 + nrm(25, (Ld, ATT_HEAD_DIM), 0.02),
        'att_knorm': 1.0 + nrm(26, (Ld, ATT_HEAD_DIM), 0.02),
        'w_branch': nrm(27, (Ld, N_BRANCH, BRANCH_W, D), BRANCH_W ** -0.5),
        'w_out': nrm(28, (Ld, D, D), D ** -0.5),
        'router_wg': nrm(29, (Ld, D, MOE_GROUPS), D ** -0.5),
        'router_bg': nrm(30, (Ld, MOE_GROUPS), 0.01),
        'router_we': nrm(31, (Ld, D, MOE_EXPERTS), D ** -0.5),
        'router_be': nrm(32, (Ld, MOE_EXPERTS), 0.01),
        'exp_w1': nrm(33, (Ld, MOE_EXPERTS, D, MOE_FF), D ** -0.5),
        'exp_w3': nrm(34, (Ld, MOE_EXPERTS, D, MOE_FF), D ** -0.5),
        'exp_w2': nrm(35, (Ld, MOE_EXPERTS, MOE_FF, D), MOE_FF ** -0.5),
    }


def reference(x, c, ctx, c_ctx, ada_w, ada_b, norm1, norm2, w_in, ssd_conv_w, ssd_conv_b, ssd_dt_bias,
              ssd_a_log, ssd_d, ssd_norm, gla_g2, gla_gb, gla_norm, lru_conv_w, lru_conv_b, lru_wa, lru_ba,
              lru_wx, lru_bx, lru_lambda, att_qnorm, att_knorm, w_branch, w_out, router_wg, router_bg,
              router_we, router_be, exp_w1, exp_w3, exp_w2):
    L = x.shape[1]
    rows = L // GRID_W
    row = jnp.repeat(jnp.arange(rows, dtype=jnp.int32), GRID_W)
    col = jnp.tile(jnp.arange(GRID_W, dtype=jnp.int32), rows)
    xl, xc = x, ctx
    for l in range(DEPTH):
        lp = dict(norm1=norm1[l], norm2=norm2[l], w_in=w_in[l],
                  ssd_conv_w=ssd_conv_w[l], ssd_conv_b=ssd_conv_b[l], ssd_dt_bias=ssd_dt_bias[l],
                  ssd_a_log=ssd_a_log[l], ssd_d=ssd_d[l], ssd_norm=ssd_norm[l],
                  gla_g2=gla_g2[l], gla_gb=gla_gb[l], gla_norm=gla_norm[l],
                  lru_conv_w=lru_conv_w[l], lru_conv_b=lru_conv_b[l], lru_wa=lru_wa[l], lru_ba=lru_ba[l],
                  lru_wx=lru_wx[l], lru_bx=lru_bx[l], lru_lambda=lru_lambda[l],
                  att_qnorm=att_qnorm[l], att_knorm=att_knorm[l], w_branch=w_branch[l], w_out=w_out[l],
                  router_wg=router_wg[l], router_bg=router_bg[l], router_we=router_we[l],
                  router_be=router_be[l], exp_w1=exp_w1[l], exp_w3=exp_w3[l], exp_w2=exp_w2[l])
        mod_l = jax.nn.silu(c) @ ada_w[l] + ada_b[l]
        mod_c = jax.nn.silu(c_ctx) @ ada_w[l] + ada_b[l]
        xl, xc = hybrid_layer(xl, xc, mod_l, mod_c, lp, row, col, l < DEPTH - 1)
    return xl
```

```python
import functools
import math

import jax
import jax.numpy as jnp
from jax import lax
from jax.experimental import pallas as pl
from jax.experimental.pallas import tpu as pltpu

F32 = jnp.float32
BF16 = jnp.bfloat16

EPS = 1e-6
GRID_W = 64
ROPE_THETA = 10000.0

SSD_HEADS = 8
SSD_HEAD_DIM = 64
SSD_INNER = 512
SSD_GROUPS = 2
SSD_STATE = 128
SSD_XBC = 1024
CONV_W = 4
GLA_HEADS = 4
GLA_DK = 128
GLA_RANK = 16
GLA_GATE_NORM = 16.0
GLA_CHUNK = 64
LRU_W = 512
LRU_BLOCKS = 8
LRU_BLOCK = 64
LRU_C = 8.0
ATT_HEADS = 4
ATT_KV_HEADS = 2
ATT_HEAD_DIM = 128
MOE_GROUPS = 4
MOE_PER_GROUP = 4
MOE_EXPERTS = 16
MOE_FF = 512
BRANCH_W = 512

C_MG, C_XBC, C_Z, C_GQ, C_GK, C_GV, C_GR = 0, 4096, 5120, 5632, 6144, 6656, 7168
C_LX, C_LG, C_AQ, C_AK, C_AV, N_MAIN = 7680, 8192, 8704, 9216, 9472, 9728
N_SMALL = 256
SM_DT, SM_G1 = 0, 8

HALO = 16
SSD_Q = 128
GLA_R = 128
LRU_R = 256
LRU_SEG = LRU_R // 8

VMEM_LIMIT = 56 * 1024 * 1024

_NT = (((1,), (1,)), ((), ()))
_TN = (((0,), (0,)), ((), ()))


def _dot(a, b):
    return jnp.dot(a, b, preferred_element_type=F32)


def _dot_nt(a, b):
    return lax.dot_general(a, b, _NT, preferred_element_type=F32)


def _dot_tn(a, b):
    return lax.dot_general(a, b, _TN, preferred_element_type=F32)


def _split3(x):
    h = x.astype(BF16)
    r = x - h.astype(F32)
    m = r.astype(BF16)
    l = (r - m.astype(F32)).astype(BF16)
    return h, m, l


def _dot_mask_lhs(mask_bf16, x):
    h, m, l = _split3(x)
    return _dot(mask_bf16, h) + _dot(mask_bf16, m) + _dot(mask_bf16, l)


def _dot_mask_rhs(x, mask_bf16):
    h, m, l = _split3(x)
    return _dot(h, mask_bf16) + _dot(m, mask_bf16) + _dot(l, mask_bf16)


def _sigmoid(x):
    return 1.0 / (1.0 + jnp.exp(-x))


def _silu(x):
    return x * _sigmoid(x)


def _softplus(x):
    return jnp.maximum(x, 0.0) + jnp.log1p(jnp.exp(-jnp.abs(x)))


def _cparams(sem):
    return pltpu.CompilerParams(dimension_semantics=sem, vmem_limit_bytes=VMEM_LIMIT)


def _mod_kernel(c_ref, w_ref, b_ref, o_ref):
    s = _silu(c_ref[...])
    sh, sm, sl = _split3(s)
    w = w_ref[0]
    wh = w.astype(BF16)
    wl = (w - wh.astype(F32)).astype(BF16)
    o_ref[0] = _dot(sh, wh) + _dot(sm, wh) + _dot(sh, wl) + b_ref[0]


def _modulation(cvec, ada_w, ada_b):
    ld, d, n6 = ada_w.shape
    tn = 1536
    return pl.pallas_call(
        _mod_kernel,
        grid=(ld, n6 // tn),
        in_specs=[
            pl.BlockSpec((8, d), lambda l, j: (0, 0)),
            pl.BlockSpec((1, d, tn), lambda l, j: (l, 0, j)),
            pl.BlockSpec((1, 1, tn), lambda l, j: (l, 0, j)),
        ],
        out_specs=pl.BlockSpec((1, 8, tn), lambda l, j: (l, 0, j)),
        out_shape=jax.ShapeDtypeStruct((ld, 8, n6), F32),
        compiler_params=_cparams(("arbitrary", "arbitrary")),
        name="modulation",
    )(cvec, ada_w, ada_b.reshape(ld, 1, n6))


def _inproj_kernel(x_ref, g_ref, sh_ref, sc_ref, w_ref, wsh_ref, wsl_ref, wsth_ref, wstl_ref,
                   p_ref, s_ref, st_ref, h_ref):
    j = pl.program_id(1)

    @pl.when(j == 0)
    def _():
        x = x_ref[...]
        h = x * lax.rsqrt(jnp.mean(x * x, axis=-1, keepdims=True) + EPS) * g_ref[...]
        h = h * (1.0 + sc_ref[0]) + sh_ref[0]
        hh = h.astype(BF16)
        hl = (h - hh.astype(F32)).astype(BF16)
        h_ref[...] = hh
        wsh = wsh_ref[...]
        s_ref[...] = _dot(hh, wsh) + _dot(hl, wsh) + _dot(hh, wsl_ref[...])
        wsth = wsth_ref[...]
        st_ref[...] = _dot_nt(wsth, hh) + _dot_nt(wsth, hl) + _dot_nt(wstl_ref[...], hh)

    p_ref[...] = _dot(h_ref[...], w_ref[...]).astype(BF16)


def _in_projection(x_all, norm_g, mod3, w_main, ws_hi, ws_lo, wst_hi, wst_lo, dims):
    t, d = x_all.shape
    tm, tn = dims["tm_proj"], N_MAIN // 4
    modrow = dims["modrow"]
    return pl.pallas_call(
        _inproj_kernel,
        grid=(t // tm, N_MAIN // tn),
        in_specs=[
            pl.BlockSpec((tm, d), lambda i, j: (i, 0)),
            pl.BlockSpec((1, d), lambda i, j: (0, 0)),
            pl.BlockSpec((1, 1, d), lambda i, j: (modrow(i, tm) * 6 + 0, 0, 0)),
            pl.BlockSpec((1, 1, d), lambda i, j: (modrow(i, tm) * 6 + 1, 0, 0)),
            pl.BlockSpec((d, tn), lambda i, j: (0, j)),
            pl.BlockSpec((d, N_SMALL), lambda i, j: (0, 0)),
            pl.BlockSpec((d, N_SMALL), lambda i, j: (0, 0)),
            pl.BlockSpec((N_SMALL, d), lambda i, j: (0, 0)),
            pl.BlockSpec((N_SMALL, d), lambda i, j: (0, 0)),
        ],
        out_specs=[
            pl.BlockSpec((tm, tn), lambda i, j: (i, j)),
            pl.BlockSpec((tm, N_SMALL), lambda i, j: (i, 0)),
            pl.BlockSpec((N_SMALL, tm), lambda i, j: (0, i)),
        ],
        out_shape=[
            jax.ShapeDtypeStruct((t, N_MAIN), BF16),
            jax.ShapeDtypeStruct((t, N_SMALL), F32),
            jax.ShapeDtypeStruct((N_SMALL, t), F32),
        ],
        scratch_shapes=[pltpu.VMEM((tm, d), BF16)],
        compiler_params=_cparams(("arbitrary", "arbitrary")),
        name="in_projection",
    )(x_all, norm_g, mod3, mod3, w_main, ws_hi, ws_lo, wst_hi, wst_lo)


def _seq_pos(p, i, nctx, nlat):
    is_ctx = i < nctx
    k = jnp.where(is_ctx, jnp.where(p == 0, i, nctx - 1 - i),
                  jnp.where(p == 0, i - nctx, nlat - 1 - (i - nctx)))
    return k, jnp.where(is_ctx, nctx, nlat)


def _seq_chunk(b, p, i, nctx, nlat, nb):
    k, _ = _seq_pos(p, i, nctx, nlat)
    return jnp.where(i < nctx, nb * nlat + b * nctx + k, b * nlat + k)


def _seq_out_chunk(b, p, i, nctx, nlat, nb):
    return _seq_chunk(b, 1, jnp.where(p == 0, 0, i), nctx, nlat, nb)


def _fwd_slot(i, nctx, nlat):
    return jnp.where(i < nctx, nctx - 1 - i, nctx + nlat - 1 - (i - nctx))


def _conv4(prev_ref, cur_ref, next_ref, w, b, pv, nv, rows):
    xe = jnp.concatenate([prev_ref[...].astype(F32) * pv, cur_ref[...].astype(F32),
                          next_ref[...].astype(F32) * nv], axis=0)
    n = rows + 2 * HALO
    y = (pltpu.roll(xe, 2, 0) * w[0:1] + pltpu.roll(xe, 1, 0) * w[1:2] + xe * w[2:3]
         + pltpu.roll(xe, n - 1, 0) * w[3:4])
    return y[HALO:HALO + rows] + b


def _dir_mask(p, n):
    r = lax.broadcasted_iota(jnp.int32, (n, n), 0)
    c = lax.broadcasted_iota(jnp.int32, (n, n), 1)
    sgn = 1 - 2 * p
    return (r - c) * sgn >= 0


def _ssd_kernel(xc_ref, xp_ref, xn_ref, z_ref, sm_ref, smt_ref, cw_ref, cb_ref, brow_ref, arow_ref,
                bcol_ref, acol_ref, e_ref, dexp_ref, nrm_ref, o_ref, yf_ref, ht_ref, *, nctx, nlat):
    q = SSD_Q
    p = pl.program_id(1)
    i = pl.program_id(2)
    k, n = _seq_pos(p, i, nctx, nlat)
    pv = (k > 0).astype(F32)
    nv = (k < n - 1).astype(F32)

    @pl.when(i == 0)
    def _():
        ht_ref[...] = jnp.zeros_like(ht_ref)

    xbc = _silu(_conv4(xp_ref, xc_ref, xn_ref, cw_ref[...], cb_ref[...], pv, nv, q))
    xs = xbc[:, :SSD_INNER]
    bmat = xbc[:, SSD_INNER:SSD_INNER + 256].astype(BF16)
    cmat = xbc[:, SSD_INNER + 256:].astype(BF16)

    mask = _dir_mask(p, q)
    mask_t = _dir_mask(1 - p, q)
    tri = jnp.where(mask, 1.0, 0.0).astype(BF16)
    tri_t = jnp.where(mask_t, 1.0, 0.0).astype(BF16)

    dt_c = _softplus(sm_ref[...] + brow_ref[0])
    dta_c = dt_c * arow_ref[0]
    acum_c = _dot_mask_lhs(tri, dta_c)
    alast_c = jnp.sum(dta_c, axis=0, keepdims=True)
    dt_r = _softplus(smt_ref[...] + bcol_ref[0])
    acum_r = _dot_mask_rhs(dt_r * acol_ref[0], tri_t)

    e = e_ref[...]
    stack = jnp.concatenate([dt_c, jnp.exp(acum_c), dt_c * jnp.exp(alast_c - acum_c)], axis=0)
    ex = _dot(stack.astype(BF16), e)
    dt_e, ea_e, ds_e = ex[:q], ex[q:2 * q], ex[2 * q:]
    dec_e = _dot_mask_rhs(jnp.broadcast_to(jnp.exp(alast_c), (8, 128)), e)[0:1]

    dtx = (xs * dt_e).astype(BF16)
    dsx = (xs * ds_e).astype(BF16)
    lane = lax.broadcasted_iota(jnp.int32, (q, 128), 1)
    ydiag, yoff = [], []
    for g in range(SSD_GROUPS):
        bg = bmat[:, g * 128:(g + 1) * 128]
        cg = cmat[:, g * 128:(g + 1) * 128]
        cb = _dot_nt(cg, bg)
        htg = ht_ref[g]
        yoff.append(_dot(cg, htg.astype(BF16)))
        ht_ref[g] = htg * dec_e[:, g * 256:(g + 1) * 256] + _dot_tn(bg, dsx[:, g * 256:(g + 1) * 256])
        for pr in range(2):
            hd = g * 4 + pr * 2
            pair = dtx[:, hd * 64:hd * 64 + 128]
            outs = []
            for hh in (hd, hd + 1):
                seg = acum_c[:, hh:hh + 1] - acum_r[hh:hh + 1, :]
                lm = jnp.exp(jnp.where(mask, seg, -1e30))
                outs.append(_dot((cb * lm).astype(BF16), pair))
            ydiag.append(jnp.where(lane < 64, outs[0], outs[1]))
    y = jnp.concatenate(ydiag, axis=1) + jnp.concatenate(yoff, axis=1) * ea_e

    @pl.when(p == 0)
    def _():
        yf_ref[i] = y

    @pl.when(p == 1)
    def _():
        yt = (y + yf_ref[_fwd_slot(i, nctx, nlat)] + dexp_ref[...] * xs) * _silu(z_ref[...].astype(F32))
        yn = yt * lax.rsqrt(jnp.mean(yt * yt, axis=-1, keepdims=True) + EPS) * nrm_ref[...]
        o_ref[...] = yn.astype(BF16)


def _ssd_mixer(pm, sm, smt, prm, dims):
    t = pm.shape[0]
    nb, nctx, nlat = dims["B"], dims["C"] // SSD_Q, dims["L"] // SSD_Q
    nhalo = t // HALO
    per = SSD_Q // HALO
    ck = functools.partial(_seq_chunk, nctx=nctx, nlat=nlat, nb=nb)
    oc = functools.partial(_seq_out_chunk, nctx=nctx, nlat=nlat, nb=nb)
    full = lambda shape: pl.BlockSpec(shape, lambda b, p, i: (0,) * len(shape))
    bydir = lambda shape: pl.BlockSpec((1,) + shape, lambda b, p, i: (p,) + (0,) * len(shape))
    return pl.pallas_call(
        functools.partial(_ssd_kernel, nctx=nctx, nlat=nlat),
        grid=(nb, 2, nctx + nlat),
        in_specs=[
            pl.BlockSpec((SSD_Q, SSD_XBC), lambda b, p, i: (ck(b, p, i), C_XBC // SSD_XBC)),
            pl.BlockSpec((HALO, SSD_XBC),
                         lambda b, p, i: (jnp.maximum(ck(b, p, i) * per - 1, 0), C_XBC // SSD_XBC)),
            pl.BlockSpec((HALO, SSD_XBC),
                         lambda b, p, i: (jnp.minimum(ck(b, p, i) * per + per, nhalo - 1), C_XBC // SSD_XBC)),
            pl.BlockSpec((SSD_Q, SSD_INNER), lambda b, p, i: (ck(b, p, i), C_Z // SSD_INNER)),
            pl.BlockSpec((SSD_Q, 128), lambda b, p, i: (ck(b, p, i), p)),
            pl.BlockSpec((128, SSD_Q), lambda b, p, i: (p, ck(b, p, i))),
            full((CONV_W, SSD_XBC)), full((1, SSD_XBC)),
            bydir((1, 128)), bydir((1, 128)), bydir((128, 1)), bydir((128, 1)),
            full((128, SSD_INNER)), full((1, SSD_INNER)), full((1, SSD_INNER)),
        ],
        out_specs=pl.BlockSpec((SSD_Q, SSD_INNER), lambda b, p, i: (oc(b, p, i), 0)),
        out_shape=jax.ShapeDtypeStruct((t, SSD_INNER), BF16),
        scratch_shapes=[pltpu.VMEM((nctx + nlat, SSD_Q, SSD_INNER), F32),
                        pltpu.VMEM((SSD_GROUPS, SSD_STATE, 256), F32)],
        compiler_params=_cparams(("arbitrary", "arbitrary", "arbitrary")),
        name="ssd_mixer",
    )(pm, pm, pm, pm, sm, smt, prm["ssd_cw"], prm["ssd_cb"], prm["ssd_brow"], prm["ssd_arow"],
      prm["ssd_bcol"], prm["ssd_acol"], prm["ssd_e"], prm["ssd_dexp"], prm["ssd_norm"])


def _gla_kernel(q_ref, k_ref, v_ref, r_ref, sm_ref, g2h_ref, g2l_ref, gb_ref, nrm_ref,
                o_ref, yf_ref, st_ref, *, nctx, nlat):
    qc = GLA_CHUNK
    nsub = GLA_R // qc
    p = pl.program_id(1)
    i = pl.program_id(2)

    @pl.when(i == 0)
    def _():
        st_ref[...] = jnp.zeros_like(st_ref)

    mask = _dir_mask(p, qc)
    tri = jnp.where(mask, 1.0, 0.0).astype(BF16)
    g2h, g2l, gb = g2h_ref[0], g2l_ref[0], gb_ref[0]
    scale = GLA_DK ** -0.5
    slot = _fwd_slot(i, nctx, nlat)

    for s in range(nsub):
        r0 = pl.multiple_of(jnp.where(p == 0, s, nsub - 1 - s) * qc, qc)
        rows = pl.ds(r0, qc)
        sm = sm_ref[rows, :]
        smh = sm.astype(BF16)
        sml = (sm - smh.astype(F32)).astype(BF16)
        logit = _dot(smh, g2h) + _dot(sml, g2h) + _dot(smh, g2l) + gb
        g = -_softplus(-logit) * (1.0 / GLA_GATE_NORM)
        gc = _dot_mask_lhs(tri, g)
        glast = jnp.sum(g, axis=0, keepdims=True)
        qf = q_ref[rows, :].astype(F32) * scale
        kf = k_ref[rows, :].astype(F32)
        vb = v_ref[rows, :]
        outs = []
        for h in range(GLA_HEADS):
            hs = slice(h * GLA_DK, (h + 1) * GLA_DK)
            gch, qh, kh, vh = gc[:, hs], qf[:, hs], kf[:, hs], vb[:, hs]
            gref = gch[qc // 2:qc // 2 + 1]
            att = _dot_nt((qh * jnp.exp(gch - gref)).astype(BF16), (kh * jnp.exp(gref - gch)).astype(BF16))
            att = jnp.where(mask, att, 0.0).astype(BF16)
            sth = st_ref[h]
            o = _dot(att, vh) + _dot_nt((qh * jnp.exp(gch)).astype(BF16), sth.astype(BF16))
            st_ref[h] = sth * jnp.exp(glast[:, hs]) + _dot_tn(vh, (kh * jnp.exp(glast[:, hs] - gch)).astype(BF16))
            outs.append(o)
        y = jnp.concatenate(outs, axis=1)

        @pl.when(p == 0)
        def _():
            yf_ref[i, rows, :] = y

        @pl.when(p == 1)
        def _():
            yt = y + yf_ref[slot, rows, :]
            nrm = nrm_ref[...]
            parts = []
            for h in range(GLA_HEADS):
                hs = slice(h * GLA_DK, (h + 1) * GLA_DK)
                yh = yt[:, hs]
                parts.append(yh * lax.rsqrt(jnp.mean(yh * yh, axis=-1, keepdims=True) + EPS) * nrm[:, hs])
            o_ref[rows, :] = (jnp.concatenate(parts, axis=1) * _silu(r_ref[rows, :].astype(F32))).astype(BF16)


def _gla_mixer(pm, sm, prm, dims):
    t = pm.shape[0]
    nb, nctx, nlat = dims["B"], dims["C"] // GLA_R, dims["L"] // GLA_R
    ck = functools.partial(_seq_chunk, nctx=nctx, nlat=nlat, nb=nb)
    oc = functools.partial(_seq_out_chunk, nctx=nctx, nlat=nlat, nb=nb)
    w = GLA_HEADS * GLA_DK
    col = lambda c: pl.BlockSpec((GLA_R, w), lambda b, p, i: (ck(b, p, i), c // w))
    bydir = lambda shape: pl.BlockSpec((1,) + shape, lambda b, p, i: (p,) + (0,) * len(shape))
    return pl.pallas_call(
        functools.partial(_gla_kernel, nctx=nctx, nlat=nlat),
        grid=(nb, 2, nctx + nlat),
        in_specs=[
            col(C_GQ), col(C_GK), col(C_GV), col(C_GR),
            pl.BlockSpec((GLA_R, 128), lambda b, p, i: (ck(b, p, i), p)),
            bydir((128, w)), bydir((128, w)), bydir((1, w)),
            pl.BlockSpec((1, w), lambda b, p, i: (0, 0)),
        ],
        out_specs=pl.BlockSpec((GLA_R, w), lambda b, p, i: (oc(b, p, i), 0)),
        out_shape=jax.ShapeDtypeStruct((t, w), BF16),
        scratch_shapes=[pltpu.VMEM((nctx + nlat, GLA_R, w), F32),
                        pltpu.VMEM((GLA_HEADS, GLA_DK, GLA_DK), F32)],
        compiler_params=_cparams(("arbitrary", "arbitrary", "arbitrary")),
        name="gla_mixer",
    )(pm, pm, pm, pm, sm, prm["gla_g2h"], prm["gla_g2l"], prm["gla_gb"], prm["gla_norm"])


def _lru_kernel(xc_ref, xp_ref, xn_ref, gl_ref, cw_ref, cb_ref, wg_ref, ba_ref, bx_ref, lam_ref,
                o_ref, yf_ref, h_ref, a_scr, v_scr, as_scr, hs_scr, *, nctx, nlat):
    r, seg = LRU_R, LRU_SEG
    p = pl.program_id(1)
    i = pl.program_id(2)
    k, n = _seq_pos(p, i, nctx, nlat)
    pv = (k > 0).astype(F32)
    nv = (k < n - 1).astype(F32)

    @pl.when(i == 0)
    def _():
        h_ref[...] = jnp.zeros_like(h_ref)

    u = _conv4(xp_ref, xc_ref, xn_ref, cw_ref[...], cb_ref[...], pv, nv, r)
    ub = u.astype(BF16)
    ra, ix = [], []
    for j in range(LRU_W // 128):
        gj = _dot(ub[:, j * 128:(j + 1) * 128], wg_ref[0, j])
        ra.append(gj[:, :128])
        ix.append(gj[:, 128:])
    rg = _sigmoid(jnp.concatenate(ra, axis=1) + ba_ref[0])
    ig = _sigmoid(jnp.concatenate(ix, axis=1) + bx_ref[0])
    log_a = (-LRU_C * _softplus(-lam_ref[0])) * rg
    a_all = jnp.exp(log_a)
    th = jnp.tanh(log_a)
    v_all = u * ig * jnp.sqrt(-2.0 * th / (1.0 - th))
    nslab = LRU_W // 128
    for j in range(nslab):
        a_scr[j] = a_all[:, j * 128:(j + 1) * 128]
        v_scr[j] = v_all[:, j * 128:(j + 1) * 128]

    def scan(order, seg_order):
        for j in range(nslab):
            ls = slice(j * 128, (j + 1) * 128)
            acc_a = jnp.ones((8, 128), F32)
            acc_h = jnp.zeros((8, 128), F32)
            for kk in order:
                ak = a_scr[j, pl.ds(kk, 8, stride=seg), :]
                vk = v_scr[j, pl.ds(kk, 8, stride=seg), :]
                acc_h = ak * acc_h + vk
                acc_a = ak * acc_a
                as_scr[j, kk * 8:(kk + 1) * 8, :] = acc_a
                hs_scr[j, kk * 8:(kk + 1) * 8, :] = acc_h
            carry = h_ref[0:1, ls]
            rows = [None] * 8
            for s in seg_order:
                rows[s] = carry
                carry = acc_a[s:s + 1, :] * carry + acc_h[s:s + 1, :]
            h_ref[0:1, ls] = carry
            cin = jnp.concatenate(rows, axis=0)
            for kk in order:
                ks = slice(kk * 8, (kk + 1) * 8)
                hs_scr[j, ks, :] = hs_scr[j, ks, :] + as_scr[j, ks, :] * cin

    @pl.when(p == 0)
    def _():
        scan(range(seg), range(8))

    @pl.when(p == 1)
    def _():
        scan(range(seg - 1, -1, -1), range(7, -1, -1))

    def natural(s):
        return jnp.concatenate([hs_scr[j, pl.ds(s, seg, stride=8), :] for j in range(nslab)], axis=1)

    @pl.when(p == 0)
    def _():
        for s in range(8):
            yf_ref[i, s * seg:(s + 1) * seg, :] = natural(s)

    @pl.when(p == 1)
    def _():
        slot = _fwd_slot(i, nctx, nlat)
        c0 = math.sqrt(2.0 / math.pi)
        for s in range(8):
            rs = slice(s * seg, (s + 1) * seg)
            gt = gl_ref[rs, :].astype(F32)
            gelu = 0.5 * gt * (1.0 + jnp.tanh(c0 * (gt + 0.044715 * (gt * gt * gt))))
            o_ref[rs, :] = ((natural(s) + yf_ref[slot, rs, :]) * gelu).astype(BF16)


def _lru_mixer(pm, prm, dims):
    t = pm.shape[0]
    nb, nctx, nlat = dims["B"], dims["C"] // LRU_R, dims["L"] // LRU_R
    nhalo = t // HALO
    per = LRU_R // HALO
    ck = functools.partial(_seq_chunk, nctx=nctx, nlat=nlat, nb=nb)
    oc = functools.partial(_seq_out_chunk, nctx=nctx, nlat=nlat, nb=nb)
    w = LRU_W
    full = lambda shape: pl.BlockSpec(shape, lambda b, p, i: (0,) * len(shape))
    bydir = lambda shape: pl.BlockSpec((1,) + shape, lambda b, p, i: (p,) + (0,) * len(shape))
    return pl.pallas_call(
        functools.partial(_lru_kernel, nctx=nctx, nlat=nlat),
        grid=(nb, 2, nctx + nlat),
        in_specs=[
            pl.BlockSpec((LRU_R, w), lambda b, p, i: (ck(b, p, i), C_LX // w)),
            pl.BlockSpec((HALO, w), lambda b, p, i: (jnp.maximum(ck(b, p, i) * per - 1, 0), C_LX // w)),
            pl.BlockSpec((HALO, w), lambda b, p, i: (jnp.minimum(ck(b, p, i) * per + per, nhalo - 1), C_LX // w)),
            pl.BlockSpec((LRU_R, w), lambda b, p, i: (ck(b, p, i), C_LG // w)),
            full((CONV_W, w)), full((1, w)),
            bydir((w // 128, 128, 256)), bydir((1, w)), bydir((1, w)), bydir((1, w)),
        ],
        out_specs=pl.BlockSpec((LRU_R, w), lambda b, p, i: (oc(b, p, i), 0)),
        out_shape=jax.ShapeDtypeStruct((t, w), BF16),
        scratch_shapes=[pltpu.VMEM((nctx + nlat, LRU_R, w), F32),
                        pltpu.VMEM((8, w), F32),
                        pltpu.VMEM((w // 128, LRU_R, 128), F32), pltpu.VMEM((w // 128, LRU_R, 128), F32),
                        pltpu.VMEM((w // 128, LRU_R, 128), F32), pltpu.VMEM((w // 128, LRU_R, 128), F32)],
        compiler_params=_cparams(("arbitrary", "arbitrary", "arbitrary")),
        name="lru_mixer",
    )(pm, pm, pm, pm, prm["lru_cw"], prm["lru_cb"], prm["lru_wg"], prm["lru_ba"], prm["lru_bx"],
      prm["lru_lam"])


def _qkprep_kernel(q_ref, k_ref, cos_ref, sin_ref, qn_ref, kn_ref, qo_ref, ko_ref, *, nlat_tiles):
    i = pl.program_id(0)
    is_ctx = i >= nlat_tiles
    cos = jnp.where(is_ctx, 1.0, cos_ref[...])
    sin = jnp.where(is_ctx, 0.0, sin_ref[...])
    lane = lax.broadcasted_iota(jnp.int32, cos.shape, 1)
    first = (lane % 64) < 32

    def prep(x, g, scale):
        xn = x * lax.rsqrt(jnp.mean(x * x, axis=-1, keepdims=True) + EPS) * g
        sw = jnp.where(first, pltpu.roll(xn, 96, 1), pltpu.roll(xn, 32, 1))
        return ((xn * cos + sw * sin) * scale).astype(BF16)

    qn, kn = qn_ref[...], kn_ref[...]
    for h in range(ATT_HEADS):
        hs = slice(h * ATT_HEAD_DIM, (h + 1) * ATT_HEAD_DIM)
        qo_ref[:, hs] = prep(q_ref[:, hs].astype(F32), qn, ATT_HEAD_DIM ** -0.5)
    for h in range(ATT_KV_HEADS):
        hs = slice(h * ATT_HEAD_DIM, (h + 1) * ATT_HEAD_DIM)
        ko_ref[:, hs] = prep(k_ref[:, hs].astype(F32), kn, 1.0)


def _qk_prep(pm, cos, sin, prm, dims):
    t = pm.shape[0]
    tm = dims["tm_qk"]
    nlat_tiles = dims["B"] * dims["L"] // tm
    per_seq = dims["L"] // tm
    return pl.pallas_call(
        functools.partial(_qkprep_kernel, nlat_tiles=nlat_tiles),
        grid=(t // tm,),
        in_specs=[
            pl.BlockSpec((tm, 512), lambda i: (i, C_AQ // 512)),
            pl.BlockSpec((tm, 256), lambda i: (i, C_AK // 256)),
            pl.BlockSpec((tm, 128), lambda i: (i % per_seq, 0)),
            pl.BlockSpec((tm, 128), lambda i: (i % per_seq, 0)),
            pl.BlockSpec((1, 128), lambda i: (0, 0)),
            pl.BlockSpec((1, 128), lambda i: (0, 0)),
        ],
        out_specs=[pl.BlockSpec((tm, 512), lambda i: (i, 0)), pl.BlockSpec((tm, 256), lambda i: (i, 0))],
        out_shape=[jax.ShapeDtypeStruct((t, 512), BF16), jax.ShapeDtypeStruct((t, 256), BF16)],
        compiler_params=_cparams(("arbitrary",)),
        name="qk_prep",
    )(pm, pm, cos, sin, prm["att_qnorm"], prm["att_knorm"])


def _attn_kernel(q_ref, kc_ref, vc_ref, *rest, tk, nlat_chunks):
    if nlat_chunks:
        kl_ref, vl_ref, o_ref, m_scr, l_scr, acc_scr = rest
    else:
        _, o_ref, m_scr, l_scr, acc_scr = rest
    tq = q_ref.shape[0]
    q = jnp.concatenate([q_ref[:, :ATT_HEAD_DIM], q_ref[:, ATT_HEAD_DIM:]], axis=0)
    m_scr[...] = jnp.full_like(m_scr, -1e30)
    l_scr[...] = jnp.zeros_like(l_scr)
    acc_scr[...] = jnp.zeros_like(acc_scr)

    def step(kb, vb):
        s = _dot_nt(q, kb)
        m_prev = m_scr[...]
        m_new = jnp.maximum(m_prev, jnp.max(s, axis=-1, keepdims=True))
        alpha = jnp.exp(m_prev - m_new)
        pmat = jnp.exp(s - m_new)
        l_scr[...] = alpha * l_scr[...] + jnp.sum(pmat, axis=-1, keepdims=True)
        acc_scr[...] = alpha * acc_scr[...] + _dot(pmat.astype(BF16), vb)
        m_scr[...] = m_new

    step(kc_ref[...], vc_ref[...])
    if nlat_chunks:
        def body(c, carry):
            rows = pl.ds(pl.multiple_of(c * tk, tk), tk)
            step(kl_ref[rows, :], vl_ref[rows, :])
            return carry
        lax.fori_loop(0, nlat_chunks, body, 0)
    o = acc_scr[...] / l_scr[...]
    o_ref[...] = jnp.concatenate([o[:tq], o[tq:]], axis=1).astype(BF16)


def _attention(qn, kn, pm, dims, latent, prev=None):
    nb, l, c = dims["B"], dims["L"], dims["C"]
    dh = ATT_HEAD_DIM
    tq = dims["tq"] if latent else c
    nq = (l if latent else c) // tq
    qrow = (lambda b, i: b * nq + i) if latent else (lambda b, i: nb * l // c + b)
    ctx_blk = nb * l // c
    av = C_AV // dh
    in_specs = [
        pl.BlockSpec((tq, 2 * dh), lambda b, g, i: (qrow(b, i), g)),
        pl.BlockSpec((c, dh), lambda b, g, i: (ctx_blk + b, g)),
        pl.BlockSpec((c, dh), lambda b, g, i: (ctx_blk + b, av + g)),
    ]
    args = [qn, kn, pm]
    if latent:
        in_specs += [pl.BlockSpec((l, dh), lambda b, g, i: (b, g)),
                     pl.BlockSpec((l, dh), lambda b, g, i: (b, av + g))]
        args += [kn, pm]
    else:
        in_specs.append(pl.BlockSpec(memory_space=pl.ANY))
        args.append(prev)
    tk = c
    return pl.pallas_call(
        functools.partial(_attn_kernel, tk=tk, nlat_chunks=(l // tk if latent else 0)),
        grid=(nb, ATT_KV_HEADS, nq),
        in_specs=in_specs,
        out_specs=pl.BlockSpec((tq, 2 * dh), lambda b, g, i: (qrow(b, i), g)),
        out_shape=jax.ShapeDtypeStruct((qn.shape[0], ATT_HEADS * dh), BF16),
        scratch_shapes=[pltpu.VMEM((2 * tq, 1), F32), pltpu.VMEM((2 * tq, 1), F32),
                        pltpu.VMEM((2 * tq, dh), F32)],
        input_output_aliases={} if latent else {3: 0},
        compiler_params=_cparams(("arbitrary", "arbitrary", "arbitrary")),
        name="attention_latent" if latent else "attention_context",
    )(*args)


def _merge_kernel(x_ref, ya_ref, yb_ref, yc_ref, yd_ref, mg_ref, g1_ref, sh_ref, sc_ref, n2_ref,
                  wb_ref, wo_ref, rwh_ref, rwl_ref, rb_ref, xo_ref, h_ref, lg_ref):
    d = x_ref.shape[1]
    acc = None
    for nbr, y_ref in enumerate((ya_ref, yb_ref, yc_ref, yd_ref)):
        gate = _sigmoid(mg_ref[:, nbr * d:(nbr + 1) * d].astype(F32))
        term = gate * _dot(y_ref[...], wb_ref[nbr])
        acc = term if acc is None else acc + term
    xn = x_ref[...] + g1_ref[0] * _dot(acc.astype(BF16), wo_ref[...])
    xo_ref[...] = xn
    h = xn * lax.rsqrt(jnp.mean(xn * xn, axis=-1, keepdims=True) + EPS) * n2_ref[...]
    h = h * (1.0 + sc_ref[0]) + sh_ref[0]
    hh = h.astype(BF16)
    hl = (h - hh.astype(F32)).astype(BF16)
    h_ref[...] = hh
    rwh = rwh_ref[...]
    lg_ref[...] = _dot(hh, rwh) + _dot(hl, rwh) + _dot(hh, rwl_ref[...]) + rb_ref[...]


def _merge(x_all, ya, yb, yc, yd, pm, mod3, prm, dims, n_tiles):
    t, d = x_all.shape
    tm = dims["tm_merge"]
    modrow = dims["modrow"]
    row = lambda shape: pl.BlockSpec(shape, lambda i: (i, 0))
    full = lambda shape: pl.BlockSpec(shape, lambda i: (0,) * len(shape))
    modspec = lambda comp: pl.BlockSpec((1, 1, d), lambda i: (modrow(i, tm) * 6 + comp, 0, 0))
    return pl.pallas_call(
        _merge_kernel,
        grid=(n_tiles,),
        in_specs=[
            row((tm, d)), row((tm, BRANCH_W)), row((tm, BRANCH_W)), row((tm, BRANCH_W)), row((tm, BRANCH_W)),
            pl.BlockSpec((tm, 4 * d), lambda i: (i, C_MG // (4 * d))),
            modspec(2), modspec(3), modspec(4), full((1, d)),
            full((4, BRANCH_W, d)), full((d, d)), full((d, 128)), full((d, 128)), full((1, 128)),
        ],
        out_specs=[row((tm, d)), row((tm, d)), row((tm, 128))],
        out_shape=[jax.ShapeDtypeStruct((t, d), F32), jax.ShapeDtypeStruct((t, d), BF16),
                   jax.ShapeDtypeStruct((t, 128), F32)],
        compiler_params=_cparams(("arbitrary",)),
        name="merge",
    )(x_all, ya, yb, yc, yd, pm, mod3, mod3, mod3, prm["norm2"], prm["w_branch"], prm["w_out"],
      prm["rw_hi"], prm["rw_lo"], prm["rb"])


def _route_weights(lg):
    lane = lax.broadcasted_iota(jnp.int32, lg.shape, 1)
    neg = -1e30
    is_g = lane < MOE_GROUPS
    gl = jnp.where(is_g, lg, neg)
    gmax = jnp.max(gl, axis=-1, keepdims=True)
    gsum = jnp.sum(jnp.where(is_g, jnp.exp(gl - gmax), 0.0), axis=-1, keepdims=True)
    gi = jnp.min(jnp.where(is_g & (gl == gmax), lane, 1 << 20), axis=-1, keepdims=True)
    pg_sel = 1.0 / gsum
    lo = MOE_GROUPS + gi * MOE_PER_GROUP
    in_grp = (lane >= lo) & (lane < lo + MOE_PER_GROUP)
    el = jnp.where(in_grp, lg, neg)
    emax = jnp.max(el, axis=-1, keepdims=True)
    ex = jnp.where(in_grp, jnp.exp(el - emax), 0.0)
    pe = ex / jnp.sum(ex, axis=-1, keepdims=True)
    pe_m = jnp.where(in_grp, pe, -1.0)
    v1 = jnp.max(pe_m, axis=-1, keepdims=True)
    i1 = jnp.min(jnp.where(pe_m == v1, lane, 1 << 20), axis=-1, keepdims=True)
    pe_m2 = jnp.where(lane == i1, -1.0, pe_m)
    v2 = jnp.max(pe_m2, axis=-1, keepdims=True)
    i2 = jnp.min(jnp.where(pe_m2 == v2, lane, 1 << 20), axis=-1, keepdims=True)
    tot = v1 + v2
    w = jnp.where(lane == i1, v1 / tot, 0.0) + jnp.where(lane == i2, v2 / tot, 0.0)
    return w * pg_sel


def _moe_kernel(h_ref, lg_ref, x_ref, g2_ref, w1_ref, w3_ref, w2_ref, o_ref, acc_ref, rw_ref):
    e = pl.program_id(1)

    @pl.when(e == 0)
    def _():
        acc_ref[...] = jnp.zeros_like(acc_ref)
        rw_ref[...] = _route_weights(lg_ref[...])

    h = h_ref[...]
    a = _silu(_dot(h, w1_ref[0])) * _dot(h, w3_ref[0])
    y = _dot(a.astype(BF16), w2_ref[0])
    rw = rw_ref[...]
    lane = lax.broadcasted_iota(jnp.int32, rw.shape, 1)
    we = jnp.sum(jnp.where(lane == e + MOE_GROUPS, rw, 0.0), axis=-1, keepdims=True)
    acc_ref[...] += we * y

    @pl.when(e == MOE_EXPERTS - 1)
    def _():
        o_ref[...] = x_ref[...] + g2_ref[0] * acc_ref[...]


def _moe(h2, lg, x_mid, mod3, prm, dims, n_tiles):
    t, d = x_mid.shape
    tm = dims["tm_moe"]
    modrow = dims["modrow"]
    return pl.pallas_call(
        _moe_kernel,
        grid=(n_tiles, MOE_EXPERTS),
        in_specs=[
            pl.BlockSpec((tm, d), lambda i, e: (i, 0)),
            pl.BlockSpec((tm, 128), lambda i, e: (i, 0)),
            pl.BlockSpec((tm, d), lambda i, e: (i, 0)),
            pl.BlockSpec((1, 1, d), lambda i, e: (modrow(i, tm) * 6 + 5, 0, 0)),
            pl.BlockSpec((1, d, MOE_FF), lambda i, e: (e, 0, 0)),
            pl.BlockSpec((1, d, MOE_FF), lambda i, e: (e, 0, 0)),
            pl.BlockSpec((1, MOE_FF, d), lambda i, e: (e, 0, 0)),
        ],
        out_specs=pl.BlockSpec((tm, d), lambda i, e: (i, 0)),
        out_shape=jax.ShapeDtypeStruct((n_tiles * tm, d), F32),
        scratch_shapes=[pltpu.VMEM((tm, d), F32), pltpu.VMEM((tm, 128), F32)],
        compiler_params=_cparams(("arbitrary", "arbitrary")),
        name="moe",
    )(h2, lg, x_mid, mod3, prm["exp_w1"], prm["exp_w3"], prm["exp_w2"])


def _hi_lo(w):
    hi = w.astype(BF16)
    return hi, (w - hi.astype(F32)).astype(BF16)


def _layer_params(l, p):
    d = p["w_in"].shape[1]
    w_in = p["w_in"][l]
    o = 0
    cols = {}
    for name, size in (("z", 512), ("xbc", 1024), ("dt", 16), ("gq", 512), ("gk", 512), ("gv", 512),
                       ("g1", 32), ("gr", 512), ("lx", 512), ("lg", 512), ("aq", 512), ("ak", 256),
                       ("av", 256), ("mg", 4 * d)):
        cols[name] = w_in[:, o:o + size]
        o += size
    out = {}
    out["w_main"] = jnp.concatenate([cols[n] for n in ("mg", "xbc", "z", "gq", "gk", "gv", "gr", "lx", "lg",
                                                        "aq", "ak", "av")], axis=1).astype(BF16)
    zpad = jnp.zeros((d, 128 - 8 - GLA_RANK), F32)
    w_small = jnp.concatenate([cols["dt"][:, :8], cols["g1"][:, :GLA_RANK], zpad,
                               cols["dt"][:, 8:], cols["g1"][:, GLA_RANK:], zpad], axis=1)
    out["ws_hi"], out["ws_lo"] = _hi_lo(w_small)
    out["wst_hi"], out["wst_lo"] = _hi_lo(w_small.T)
    out["norm1"] = p["norm1"][l][None]
    out["norm2"] = p["norm2"][l][None]

    out["ssd_cw"] = p["ssd_conv_w"][l]
    out["ssd_cb"] = p["ssd_conv_b"][l][None]
    pad8 = lambda v: jnp.pad(v, ((0, 0), (0, 128 - SSD_HEADS)))
    brow = pad8(p["ssd_dt_bias"][l])
    arow = pad8(-jnp.exp(p["ssd_a_log"][l]))
    out["ssd_brow"], out["ssd_arow"] = brow[:, None, :], arow[:, None, :]
    out["ssd_bcol"], out["ssd_acol"] = brow[:, :, None], arow[:, :, None]
    head_of_lane = jnp.arange(SSD_INNER) // SSD_HEAD_DIM
    out["ssd_e"] = (jnp.arange(128)[:, None] == head_of_lane[None, :]).astype(BF16)
    out["ssd_dexp"] = jnp.repeat(p["ssd_d"][l], SSD_HEAD_DIM)[None]
    out["ssd_norm"] = p["ssd_norm"][l][None]

    g2 = jnp.zeros((2, 128, GLA_HEADS * GLA_DK), F32).at[:, SM_G1:SM_G1 + GLA_RANK].set(p["gla_g2"][l])
    out["gla_g2h"], out["gla_g2l"] = _hi_lo(g2)
    out["gla_gb"] = p["gla_gb"][l][:, None, :]
    out["gla_norm"] = p["gla_norm"][l][None]

    def pairs(w):
        w = w.reshape(2, LRU_BLOCKS // 2, 2, LRU_BLOCK, LRU_BLOCK)
        z = jnp.zeros_like(w[:, :, 0])
        top = jnp.concatenate([w[:, :, 0], z], axis=-1)
        bot = jnp.concatenate([z, w[:, :, 1]], axis=-1)
        return jnp.concatenate([top, bot], axis=-2)
    out["lru_wg"] = jnp.concatenate([pairs(p["lru_wa"][l]), pairs(p["lru_wx"][l])], axis=-1).astype(BF16)
    out["lru_cw"] = p["lru_conv_w"][l]
    out["lru_cb"] = p["lru_conv_b"][l][None]
    out["lru_ba"] = p["lru_ba"][l][:, None, :]
    out["lru_bx"] = p["lru_bx"][l][:, None, :]
    out["lru_lam"] = p["lru_lambda"][l][:, None, :]

    out["att_qnorm"] = p["att_qnorm"][l][None]
    out["att_knorm"] = p["att_knorm"][l][None]
    out["w_branch"] = p["w_branch"][l].astype(BF16)
    out["w_out"] = p["w_out"][l].astype(BF16)
    rw = jnp.concatenate([p["router_wg"][l], p["router_we"][l],
                          jnp.zeros((d, 128 - MOE_GROUPS - MOE_EXPERTS), F32)], axis=1)
    out["rw_hi"], out["rw_lo"] = _hi_lo(rw)
    out["rb"] = jnp.concatenate([p["router_bg"][l], p["router_be"][l],
                                 jnp.zeros((128 - MOE_GROUPS - MOE_EXPERTS,), F32)])[None]
    out["exp_w1"] = p["exp_w1"][l].astype(BF16)
    out["exp_w3"] = p["exp_w3"][l].astype(BF16)
    out["exp_w2"] = p["exp_w2"][l].astype(BF16)
    return out


def _rope_tables(l):
    f = ATT_HEAD_DIM // 4
    inv = ROPE_THETA ** (-jnp.arange(f, dtype=F32) / f)
    tpos = jnp.arange(l, dtype=jnp.int32)
    row = (tpos // GRID_W).astype(F32)[:, None] * inv
    col = (tpos % GRID_W).astype(F32)[:, None] * inv
    cos = jnp.concatenate([jnp.cos(row), jnp.cos(row), jnp.cos(col), jnp.cos(col)], axis=1)
    sin = jnp.concatenate([-jnp.sin(row), jnp.sin(row), -jnp.sin(col), jnp.sin(col)], axis=1)
    return cos, sin


def _pick(n, cands):
    for c in cands:
        if n % c == 0:
            return c
    raise ValueError(f"no tile size for {n}")


def kernel(x, c, ctx, c_ctx, ada_w, ada_b, norm1, norm2, w_in, ssd_conv_w, ssd_conv_b, ssd_dt_bias, ssd_a_log, ssd_d, ssd_norm, gla_g2, gla_gb, gla_norm, lru_conv_w, lru_conv_b, lru_wa, lru_ba, lru_wx, lru_bx, lru_lambda, att_qnorm, att_knorm, w_branch, w_out, router_wg, router_bg, router_we, router_be, exp_w1, exp_w3, exp_w2):
    nb, l, d = x.shape
    c_len = ctx.shape[1]
    depth = ada_w.shape[0]
    tl, tc = nb * l, nb * c_len
    assert l % LRU_R == 0 and c_len % LRU_R == 0 and nb + 1 <= 8 and l % GRID_W == 0
    params = dict(norm1=norm1, norm2=norm2, w_in=w_in, ssd_conv_w=ssd_conv_w, ssd_conv_b=ssd_conv_b,
                  ssd_dt_bias=ssd_dt_bias, ssd_a_log=ssd_a_log, ssd_d=ssd_d, ssd_norm=ssd_norm,
                  gla_g2=gla_g2, gla_gb=gla_gb, gla_norm=gla_norm, lru_conv_w=lru_conv_w,
                  lru_conv_b=lru_conv_b, lru_wa=lru_wa, lru_ba=lru_ba, lru_wx=lru_wx, lru_bx=lru_bx,
                  lru_lambda=lru_lambda, att_qnorm=att_qnorm, att_knorm=att_knorm, w_branch=w_branch,
                  w_out=w_out, router_wg=router_wg, router_bg=router_bg, router_we=router_we,
                  router_be=router_be, exp_w1=exp_w1, exp_w3=exp_w3, exp_w2=exp_w2)

    tile = _pick(math.gcd(l, tc), (1024, 512, 256))

    def modrow(i, tm):
        return jnp.where(i < tl // tm, i // (l // tm), nb)

    dims = dict(B=nb, L=l, C=c_len, modrow=modrow, tm_proj=tile, tm_merge=min(tile, 512), tm_moe=tile,
                tm_qk=min(tile, 512), tq=_pick(l, (512, 256)))

    cvec = jnp.zeros((8, d), F32).at[:nb].set(c).at[nb].set(c_ctx)
    mod_all = _modulation(cvec, ada_w, ada_b)
    cos, sin = _rope_tables(l)
    x_all = jnp.concatenate([x.reshape(tl, d), ctx.reshape(tc, d)], axis=0)

    for layer in range(depth):
        last = layer == depth - 1
        prm = _layer_params(layer, params)
        mod3 = mod_all[layer].reshape(8 * 6, 1, d)
        pm, sm, smt = _in_projection(x_all, prm["norm1"], mod3, prm["w_main"], prm["ws_hi"], prm["ws_lo"],
                                     prm["wst_hi"], prm["wst_lo"], dims)
        ya = _ssd_mixer(pm, sm, smt, prm, dims)
        yb = _gla_mixer(pm, sm, prm, dims)
        yc = _lru_mixer(pm, prm, dims)
        qn, kn = _qk_prep(pm, cos, sin, prm, dims)
        yd = _attention(qn, kn, pm, dims, latent=True)
        if not last:
            yd = _attention(qn, kn, pm, dims, latent=False, prev=yd)
        n_rows = tl if last else tl + tc
        x_mid, h2, lg = _merge(x_all, ya, yb, yc, yd, pm, mod3, prm, dims, n_rows // dims["tm_merge"])
        x_all = _moe(h2, lg, x_mid, mod3, prm, dims, n_rows // dims["tm_moe"])
    return x_all.reshape(nb, l, d)
```

```python
import functools
import math

import jax
import jax.numpy as jnp
from jax import lax
from jax.experimental import pallas as pl
from jax.experimental.pallas import tpu as pltpu

F32 = jnp.float32
BF16 = jnp.bfloat16

EPS = 1e-6
GRID_W = 64
ROPE_THETA = 10000.0

SSD_HEADS = 8
SSD_HEAD_DIM = 64
SSD_INNER = 512
SSD_GROUPS = 2
SSD_STATE = 128
SSD_XBC = 1024
CONV_W = 4
GLA_HEADS = 4
GLA_DK = 128
GLA_RANK = 16
GLA_GATE_NORM = 16.0
GLA_CHUNK = 64
LRU_W = 512
LRU_BLOCKS = 8
LRU_BLOCK = 64
LRU_C = 8.0
ATT_HEADS = 4
ATT_KV_HEADS = 2
ATT_HEAD_DIM = 128
MOE_GROUPS = 4
MOE_PER_GROUP = 4
MOE_EXPERTS = 16
MOE_FF = 512
BRANCH_W = 512

C_MG, C_XBC, C_Z, C_GQ, C_GK, C_GV, C_GR = 0, 4096, 5120, 5632, 6144, 6656, 7168
C_LX, C_LG, C_AQ, C_AK, C_AV, N_MAIN = 7680, 8192, 8704, 9216, 9472, 9728
N_SMALL = 256
SM_DT, SM_G1 = 0, 8

HALO = 16
SSD_Q = 128
GLA_R = 256
LRU_R = 256
LRU_SEG = LRU_R // 8

VMEM_LIMIT = 56 * 1024 * 1024

_NT = (((1,), (1,)), ((), ()))
_TN = (((0,), (0,)), ((), ()))


def _dot(a, b):
    return jnp.dot(a, b, preferred_element_type=F32)


def _dot_nt(a, b):
    return lax.dot_general(a, b, _NT, preferred_element_type=F32)


def _dot_tn(a, b):
    return lax.dot_general(a, b, _TN, preferred_element_type=F32)


def _split3(x):
    h = x.astype(BF16)
    r = x - h.astype(F32)
    m = r.astype(BF16)
    l = (r - m.astype(F32)).astype(BF16)
    return h, m, l


def _dot_mask_lhs(mask_bf16, x):
    h, m, l = _split3(x)
    return _dot(mask_bf16, h) + _dot(mask_bf16, m) + _dot(mask_bf16, l)


def _dot_mask_rhs(x, mask_bf16):
    h, m, l = _split3(x)
    return _dot(h, mask_bf16) + _dot(m, mask_bf16) + _dot(l, mask_bf16)


def _sigmoid(x):
    return 1.0 / (1.0 + jnp.exp(-x))


def _silu(x):
    return x * _sigmoid(x)


def _softplus(x):
    return jnp.maximum(x, 0.0) + jnp.log1p(jnp.exp(-jnp.abs(x)))


def _cparams(sem):
    return pltpu.CompilerParams(dimension_semantics=sem, vmem_limit_bytes=VMEM_LIMIT)


def _mod_kernel(c_ref, w_ref, b_ref, o_ref):
    s = _silu(c_ref[...])
    sh, sm, sl = _split3(s)
    w = w_ref[0]
    wh = w.astype(BF16)
    wl = (w - wh.astype(F32)).astype(BF16)
    o_ref[0] = _dot(sh, wh) + _dot(sm, wh) + _dot(sh, wl) + b_ref[0]


def _modulation(cvec, ada_w, ada_b):
    ld, d, n6 = ada_w.shape
    tn = 1536
    return pl.pallas_call(
        _mod_kernel,
        grid=(ld, n6 // tn),
        in_specs=[
            pl.BlockSpec((8, d), lambda l, j: (0, 0)),
            pl.BlockSpec((1, d, tn), lambda l, j: (l, 0, j)),
            pl.BlockSpec((1, 1, tn), lambda l, j: (l, 0, j)),
        ],
        out_specs=pl.BlockSpec((1, 8, tn), lambda l, j: (l, 0, j)),
        out_shape=jax.ShapeDtypeStruct((ld, 8, n6), F32),
        compiler_params=_cparams(("arbitrary", "arbitrary")),
        name="modulation",
    )(cvec, ada_w, ada_b.reshape(ld, 1, n6))


def _inproj_kernel(x_ref, g_ref, sh_ref, sc_ref, w_ref, wsh_ref, wsl_ref, wsth_ref, wstl_ref,
                   p_ref, s_ref, st_ref, h_ref):
    j = pl.program_id(1)

    @pl.when(j == 0)
    def _():
        x = x_ref[...]
        h = x * lax.rsqrt(jnp.mean(x * x, axis=-1, keepdims=True) + EPS) * g_ref[...]
        h = h * (1.0 + sc_ref[0]) + sh_ref[0]
        hh = h.astype(BF16)
        hl = (h - hh.astype(F32)).astype(BF16)
        h_ref[...] = hh
        wsh = wsh_ref[...]
        s_ref[...] = _dot(hh, wsh) + _dot(hl, wsh) + _dot(hh, wsl_ref[...])
        wsth = wsth_ref[...]
        st_ref[...] = _dot_nt(wsth, hh) + _dot_nt(wsth, hl) + _dot_nt(wstl_ref[...], hh)

    p_ref[...] = _dot(h_ref[...], w_ref[...]).astype(BF16)


def _in_projection(x_all, norm_g, mod3, w_main, ws_hi, ws_lo, wst_hi, wst_lo, dims):
    t, d = x_all.shape
    tm, tn = dims["tm_proj"], N_MAIN // 4
    modrow = dims["modrow"]
    return pl.pallas_call(
        _inproj_kernel,
        grid=(t // tm, N_MAIN // tn),
        in_specs=[
            pl.BlockSpec((tm, d), lambda i, j: (i, 0)),
            pl.BlockSpec((1, d), lambda i, j: (0, 0)),
            pl.BlockSpec((1, 1, d), lambda i, j: (modrow(i, tm) * 6 + 0, 0, 0)),
            pl.BlockSpec((1, 1, d), lambda i, j: (modrow(i, tm) * 6 + 1, 0, 0)),
            pl.BlockSpec((d, tn), lambda i, j: (0, j)),
            pl.BlockSpec((d, N_SMALL), lambda i, j: (0, 0)),
            pl.BlockSpec((d, N_SMALL), lambda i, j: (0, 0)),
            pl.BlockSpec((N_SMALL, d), lambda i, j: (0, 0)),
            pl.BlockSpec((N_SMALL, d), lambda i, j: (0, 0)),
        ],
        out_specs=[
            pl.BlockSpec((tm, tn), lambda i, j: (i, j)),
            pl.BlockSpec((tm, N_SMALL), lambda i, j: (i, 0)),
            pl.BlockSpec((N_SMALL, tm), lambda i, j: (0, i)),
        ],
        out_shape=[
            jax.ShapeDtypeStruct((t, N_MAIN), BF16),
            jax.ShapeDtypeStruct((t, N_SMALL), F32),
            jax.ShapeDtypeStruct((N_SMALL, t), F32),
        ],
        scratch_shapes=[pltpu.VMEM((tm, d), BF16)],
        compiler_params=_cparams(("arbitrary", "arbitrary")),
        name="in_projection",
    )(x_all, norm_g, mod3, mod3, w_main, ws_hi, ws_lo, wst_hi, wst_lo)


def _seq_pos(p, i, nctx, nlat):
    is_ctx = i < nctx
    k = jnp.where(is_ctx, jnp.where(p == 0, i, nctx - 1 - i),
                  jnp.where(p == 0, i - nctx, nlat - 1 - (i - nctx)))
    return k, jnp.where(is_ctx, nctx, nlat)


def _seq_chunk(b, p, i, nctx, nlat, nb):
    k, _ = _seq_pos(p, i, nctx, nlat)
    return jnp.where(i < nctx, nb * nlat + b * nctx + k, b * nlat + k)


def _seq_out_chunk(b, p, i, nctx, nlat, nb):
    return _seq_chunk(b, 1, jnp.where(p == 0, 0, i), nctx, nlat, nb)


def _fwd_slot(i, nctx, nlat):
    return jnp.where(i < nctx, nctx - 1 - i, nctx + nlat - 1 - (i - nctx))


def _conv4(prev_ref, cur_ref, next_ref, w, b, pv, nv, rows):
    xe = jnp.concatenate([prev_ref[...].astype(F32) * pv, cur_ref[...].astype(F32),
                          next_ref[...].astype(F32) * nv], axis=0)
    n = rows + 2 * HALO
    y = (pltpu.roll(xe, 2, 0) * w[0:1] + pltpu.roll(xe, 1, 0) * w[1:2] + xe * w[2:3]
         + pltpu.roll(xe, n - 1, 0) * w[3:4])
    return y[HALO:HALO + rows] + b


def _dir_mask(p, n):
    r = lax.broadcasted_iota(jnp.int32, (n, n), 0)
    c = lax.broadcasted_iota(jnp.int32, (n, n), 1)
    sgn = 1 - 2 * p
    return (r - c) * sgn >= 0


def _ssd_kernel(xc_ref, xp_ref, xn_ref, z_ref, sm_ref, smt_ref, cw_ref, cb_ref, brow_ref, arow_ref,
                bcol_ref, acol_ref, e_ref, dexp_ref, nrm_ref, o_ref, yf_ref, ht_ref, *, nctx, nlat):
    q = SSD_Q
    p = pl.program_id(1)
    i = pl.program_id(2)
    k, n = _seq_pos(p, i, nctx, nlat)
    pv = (k > 0).astype(F32)
    nv = (k < n - 1).astype(F32)

    @pl.when(i == 0)
    def _():
        ht_ref[...] = jnp.zeros_like(ht_ref)

    xbc = _silu(_conv4(xp_ref, xc_ref, xn_ref, cw_ref[...], cb_ref[...], pv, nv, q))
    xs = xbc[:, :SSD_INNER]
    bmat = xbc[:, SSD_INNER:SSD_INNER + 256].astype(BF16)
    cmat = xbc[:, SSD_INNER + 256:].astype(BF16)

    mask = _dir_mask(p, q)
    mask_t = _dir_mask(1 - p, q)
    tri = jnp.where(mask, 1.0, 0.0).astype(BF16)
    tri_t = jnp.where(mask_t, 1.0, 0.0).astype(BF16)

    dt_c = _softplus(sm_ref[...] + brow_ref[0])
    dta_c = dt_c * arow_ref[0]
    acum_c = _dot_mask_lhs(tri, dta_c)
    alast_c = jnp.sum(dta_c, axis=0, keepdims=True)
    dt_r = _softplus(smt_ref[...] + bcol_ref[0])
    acum_r = _dot_mask_rhs(dt_r * acol_ref[0], tri_t)

    e = e_ref[...]
    stack = jnp.concatenate([dt_c, jnp.exp(acum_c), dt_c * jnp.exp(alast_c - acum_c)], axis=0)
    ex = _dot(stack.astype(BF16), e)
    dt_e, ea_e, ds_e = ex[:q], ex[q:2 * q], ex[2 * q:]
    dec_e = _dot_mask_rhs(jnp.broadcast_to(jnp.exp(alast_c), (8, 128)), e)[0:1]

    dtx = (xs * dt_e).astype(BF16)
    dsx = (xs * ds_e).astype(BF16)
    lane = lax.broadcasted_iota(jnp.int32, (q, 128), 1)
    ydiag, yoff = [], []
    for g in range(SSD_GROUPS):
        bg = bmat[:, g * 128:(g + 1) * 128]
        cg = cmat[:, g * 128:(g + 1) * 128]
        cb = _dot_nt(cg, bg)
        htg = ht_ref[g]
        yoff.append(_dot(cg, htg.astype(BF16)))
        ht_ref[g] = htg * dec_e[:, g * 256:(g + 1) * 256] + _dot_tn(bg, dsx[:, g * 256:(g + 1) * 256])
        for pr in range(2):
            hd = g * 4 + pr * 2
            pair = dtx[:, hd * 64:hd * 64 + 128]
            outs = []
            for hh in (hd, hd + 1):
                seg = acum_c[:, hh:hh + 1] - acum_r[hh:hh + 1, :]
                lm = jnp.exp(jnp.where(mask, seg, -1e30))
                outs.append(_dot((cb * lm).astype(BF16), pair))
            ydiag.append(jnp.where(lane < 64, outs[0], outs[1]))
    y = jnp.concatenate(ydiag, axis=1) + jnp.concatenate(yoff, axis=1) * ea_e

    @pl.when(p == 0)
    def _():
        yf_ref[i] = y

    @pl.when(p == 1)
    def _():
        yt = (y + yf_ref[_fwd_slot(i, nctx, nlat)] + dexp_ref[...] * xs) * _silu(z_ref[...].astype(F32))
        yn = yt * lax.rsqrt(jnp.mean(yt * yt, axis=-1, keepdims=True) + EPS) * nrm_ref[...]
        o_ref[...] = yn.astype(BF16)


def _ssd_mixer(pm, sm, smt, prm, dims):
    t = pm.shape[0]
    nb, nctx, nlat = dims["B"], dims["C"] // SSD_Q, dims["L"] // SSD_Q
    nhalo = t // HALO
    per = SSD_Q // HALO
    ck = functools.partial(_seq_chunk, nctx=nctx, nlat=nlat, nb=nb)
    oc = functools.partial(_seq_out_chunk, nctx=nctx, nlat=nlat, nb=nb)
    full = lambda shape: pl.BlockSpec(shape, lambda b, p, i: (0,) * len(shape))
    bydir = lambda shape: pl.BlockSpec((1,) + shape, lambda b, p, i: (p,) + (0,) * len(shape))
    return pl.pallas_call(
        functools.partial(_ssd_kernel, nctx=nctx, nlat=nlat),
        grid=(nb, 2, nctx + nlat),
        in_specs=[
            pl.BlockSpec((SSD_Q, SSD_XBC), lambda b, p, i: (ck(b, p, i), C_XBC // SSD_XBC)),
            pl.BlockSpec((HALO, SSD_XBC),
                         lambda b, p, i: (jnp.maximum(ck(b, p, i) * per - 1, 0), C_XBC // SSD_XBC)),
            pl.BlockSpec((HALO, SSD_XBC),
                         lambda b, p, i: (jnp.minimum(ck(b, p, i) * per + per, nhalo - 1), C_XBC // SSD_XBC)),
            pl.BlockSpec((SSD_Q, SSD_INNER), lambda b, p, i: (ck(b, p, i), C_Z // SSD_INNER)),
            pl.BlockSpec((SSD_Q, 128), lambda b, p, i: (ck(b, p, i), p)),
            pl.BlockSpec((128, SSD_Q), lambda b, p, i: (p, ck(b, p, i))),
            full((CONV_W, SSD_XBC)), full((1, SSD_XBC)),
            bydir((1, 128)), bydir((1, 128)), bydir((128, 1)), bydir((128, 1)),
            full((128, SSD_INNER)), full((1, SSD_INNER)), full((1, SSD_INNER)),
        ],
        out_specs=pl.BlockSpec((SSD_Q, SSD_INNER), lambda b, p, i: (oc(b, p, i), 0)),
        out_shape=jax.ShapeDtypeStruct((t, SSD_INNER), BF16),
        scratch_shapes=[pltpu.VMEM((nctx + nlat, SSD_Q, SSD_INNER), F32),
                        pltpu.VMEM((SSD_GROUPS, SSD_STATE, 256), F32)],
        compiler_params=_cparams(("arbitrary", "arbitrary", "arbitrary")),
        name="ssd_mixer",
    )(pm, pm, pm, pm, sm, smt, prm["ssd_cw"], prm["ssd_cb"], prm["ssd_brow"], prm["ssd_arow"],
      prm["ssd_bcol"], prm["ssd_acol"], prm["ssd_e"], prm["ssd_dexp"], prm["ssd_norm"])


def _gla_kernel(q_ref, k_ref, v_ref, r_ref, sm_ref, g2h_ref, g2l_ref, gb_ref, nrm_ref,
                o_ref, yf_ref, st_ref, *, nctx, nlat):
    qc = GLA_CHUNK
    nsub = GLA_R // qc
    p = pl.program_id(1)
    i = pl.program_id(2)

    @pl.when(i == 0)
    def _():
        st_ref[...] = jnp.zeros_like(st_ref)

    r = GLA_R
    rr = lax.broadcasted_iota(jnp.int32, (r, r), 0)
    cc = lax.broadcasted_iota(jnp.int32, (r, r), 1)
    shift = qc.bit_length() - 1
    mask = (lax.shift_right_logical(rr, shift) == lax.shift_right_logical(cc, shift)) & ((rr - cc) * (1 - 2 * p) >= 0)
    tri = jnp.where(mask, 1.0, 0.0).astype(BF16)
    scale = GLA_DK ** -0.5

    sm = sm_ref[...]
    smh = sm.astype(BF16)
    sml = (sm - smh.astype(F32)).astype(BF16)
    g2h = g2h_ref[0]
    logit = _dot(smh, g2h) + _dot(sml, g2h) + _dot(smh, g2l_ref[0]) + gb_ref[0]
    g = -_softplus(-logit) * (1.0 / GLA_GATE_NORM)
    gc = _dot_mask_lhs(tri, g)
    glast = [jnp.sum(g[s * qc:(s + 1) * qc], axis=0, keepdims=True) for s in range(nsub)]
    bcast = lambda rows_: jnp.concatenate([jnp.broadcast_to(x, (qc, x.shape[1])) for x in rows_], axis=0)
    gref = bcast([gc[s * qc + qc // 2:s * qc + qc // 2 + 1] for s in range(nsub)])
    glast_f = bcast(glast)
    qf = q_ref[...].astype(F32) * scale
    kf = k_ref[...].astype(F32)
    vb = v_ref[...]
    qe = (qf * jnp.exp(gc - gref)).astype(BF16)
    ke = (kf * jnp.exp(gref - gc)).astype(BF16)
    qg = (qf * jnp.exp(gc)).astype(BF16)
    k2 = (kf * jnp.exp(glast_f - gc)).astype(BF16)
    o_intra = []
    for h in range(GLA_HEADS):
        hs = slice(h * GLA_DK, (h + 1) * GLA_DK)
        att = jnp.where(mask, _dot_nt(qe[:, hs], ke[:, hs]), 0.0).astype(BF16)
        o_intra.append(_dot(att, vb[:, hs]))

    def sweep(order, emit):
        st = [st_ref[h] for h in range(GLA_HEADS)]
        for s in order:
            rs = slice(s * qc, (s + 1) * qc)
            dec = jnp.exp(glast[s])
            outs = []
            for h in range(GLA_HEADS):
                hs = slice(h * GLA_DK, (h + 1) * GLA_DK)
                outs.append(o_intra[h][rs] + _dot_nt(qg[rs, hs], st[h].astype(BF16)))
                st[h] = st[h] * dec[:, hs] + _dot_tn(vb[rs, hs], k2[rs, hs])
            emit(rs, jnp.concatenate(outs, axis=1))
        for h in range(GLA_HEADS):
            st_ref[h] = st[h]

    @pl.when(p == 0)
    def _():
        def emit(rs, y):
            yf_ref[i, rs, :] = y
        sweep(range(nsub), emit)

    @pl.when(p == 1)
    def _():
        slot = _fwd_slot(i, nctx, nlat)
        nrm = nrm_ref[...]

        def emit(rs, y):
            yt = y + yf_ref[slot, rs, :]
            parts = []
            for h in range(GLA_HEADS):
                hs = slice(h * GLA_DK, (h + 1) * GLA_DK)
                yh = yt[:, hs]
                parts.append(yh * lax.rsqrt(jnp.mean(yh * yh, axis=-1, keepdims=True) + EPS) * nrm[:, hs])
            o_ref[rs, :] = (jnp.concatenate(parts, axis=1) * _silu(r_ref[rs, :].astype(F32))).astype(BF16)
        sweep(range(nsub - 1, -1, -1), emit)


def _gla_mixer(pm, sm, prm, dims):
    t = pm.shape[0]
    nb, nctx, nlat = dims["B"], dims["C"] // GLA_R, dims["L"] // GLA_R
    ck = functools.partial(_seq_chunk, nctx=nctx, nlat=nlat, nb=nb)
    oc = functools.partial(_seq_out_chunk, nctx=nctx, nlat=nlat, nb=nb)
    w = GLA_HEADS * GLA_DK
    col = lambda c: pl.BlockSpec((GLA_R, w), lambda b, p, i: (ck(b, p, i), c // w))
    bydir = lambda shape: pl.BlockSpec((1,) + shape, lambda b, p, i: (p,) + (0,) * len(shape))
    return pl.pallas_call(
        functools.partial(_gla_kernel, nctx=nctx, nlat=nlat),
        grid=(nb, 2, nctx + nlat),
        in_specs=[
            col(C_GQ), col(C_GK), col(C_GV), col(C_GR),
            pl.BlockSpec((GLA_R, 128), lambda b, p, i: (ck(b, p, i), p)),
            bydir((128, w)), bydir((128, w)), bydir((1, w)),
            pl.BlockSpec((1, w), lambda b, p, i: (0, 0)),
        ],
        out_specs=pl.BlockSpec((GLA_R, w), lambda b, p, i: (oc(b, p, i), 0)),
        out_shape=jax.ShapeDtypeStruct((t, w), BF16),
        scratch_shapes=[pltpu.VMEM((nctx + nlat, GLA_R, w), F32),
                        pltpu.VMEM((GLA_HEADS, GLA_DK, GLA_DK), F32)],
        compiler_params=_cparams(("arbitrary", "arbitrary", "arbitrary")),
        name="gla_mixer",
    )(pm, pm, pm, pm, sm, prm["gla_g2h"], prm["gla_g2l"], prm["gla_gb"], prm["gla_norm"])


def _lru_kernel(xc_ref, xp_ref, xn_ref, gl_ref, cw_ref, cb_ref, wg_ref, ba_ref, bx_ref, lam_ref,
                o_ref, yf_ref, h_ref, a_scr, v_scr, as_scr, hs_scr, *, nctx, nlat):
    r, seg = LRU_R, LRU_SEG
    p = pl.program_id(1)
    i = pl.program_id(2)
    k, n = _seq_pos(p, i, nctx, nlat)
    pv = (k > 0).astype(F32)
    nv = (k < n - 1).astype(F32)

    @pl.when(i == 0)
    def _():
        h_ref[...] = jnp.zeros_like(h_ref)

    u = _conv4(xp_ref, xc_ref, xn_ref, cw_ref[...], cb_ref[...], pv, nv, r)
    ub = u.astype(BF16)
    ra, ix = [], []
    for j in range(LRU_W // 128):
        gj = _dot(ub[:, j * 128:(j + 1) * 128], wg_ref[0, j])
        ra.append(gj[:, :128])
        ix.append(gj[:, 128:])
    rg = _sigmoid(jnp.concatenate(ra, axis=1) + ba_ref[0])
    ig = _sigmoid(jnp.concatenate(ix, axis=1) + bx_ref[0])
    log_a = (-LRU_C * _softplus(-lam_ref[0])) * rg
    a_all = jnp.exp(log_a)
    th = jnp.tanh(log_a)
    v_all = u * ig * jnp.sqrt(-2.0 * th / (1.0 - th))
    nslab = LRU_W // 128
    for j in range(nslab):
        a_scr[j] = a_all[:, j * 128:(j + 1) * 128]
        v_scr[j] = v_all[:, j * 128:(j + 1) * 128]

    def scan(order, seg_order):
        for j in range(nslab):
            ls = slice(j * 128, (j + 1) * 128)
            acc_a = jnp.ones((8, 128), F32)
            acc_h = jnp.zeros((8, 128), F32)
            for kk in order:
                ak = a_scr[j, pl.ds(kk, 8, stride=seg), :]
                vk = v_scr[j, pl.ds(kk, 8, stride=seg), :]
                acc_h = ak * acc_h + vk
                acc_a = ak * acc_a
                as_scr[j, kk * 8:(kk + 1) * 8, :] = acc_a
                hs_scr[j, kk * 8:(kk + 1) * 8, :] = acc_h
            carry = h_ref[0:1, ls]
            rows = [None] * 8
            for s in seg_order:
                rows[s] = carry
                carry = acc_a[s:s + 1, :] * carry + acc_h[s:s + 1, :]
            h_ref[0:1, ls] = carry
            cin = jnp.concatenate(rows, axis=0)
            for kk in order:
                ks = slice(kk * 8, (kk + 1) * 8)
                hs_scr[j, ks, :] = hs_scr[j, ks, :] + as_scr[j, ks, :] * cin

    @pl.when(p == 0)
    def _():
        scan(range(seg), range(8))

    @pl.when(p == 1)
    def _():
        scan(range(seg - 1, -1, -1), range(7, -1, -1))

    def natural(s):
        return jnp.concatenate([hs_scr[j, pl.ds(s, seg, stride=8), :] for j in range(nslab)], axis=1)

    @pl.when(p == 0)
    def _():
        for s in range(8):
            yf_ref[i, s * seg:(s + 1) * seg, :] = natural(s)

    @pl.when(p == 1)
    def _():
        slot = _fwd_slot(i, nctx, nlat)
        c0 = math.sqrt(2.0 / math.pi)
        for s in range(8):
            rs = slice(s * seg, (s + 1) * seg)
            gt = gl_ref[rs, :].astype(F32)
            gelu = 0.5 * gt * (1.0 + jnp.tanh(c0 * (gt + 0.044715 * (gt * gt * gt))))
            o_ref[rs, :] = ((natural(s) + yf_ref[slot, rs, :]) * gelu).astype(BF16)


def _lru_mixer(pm, prm, dims):
    t = pm.shape[0]
    nb, nctx, nlat = dims["B"], dims["C"] // LRU_R, dims["L"] // LRU_R
    nhalo = t // HALO
    per = LRU_R // HALO
    ck = functools.partial(_seq_chunk, nctx=nctx, nlat=nlat, nb=nb)
    oc = functools.partial(_seq_out_chunk, nctx=nctx, nlat=nlat, nb=nb)
    w = LRU_W
    full = lambda shape: pl.BlockSpec(shape, lambda b, p, i: (0,) * len(shape))
    bydir = lambda shape: pl.BlockSpec((1,) + shape, lambda b, p, i: (p,) + (0,) * len(shape))
    return pl.pallas_call(
        functools.partial(_lru_kernel, nctx=nctx, nlat=nlat),
        grid=(nb, 2, nctx + nlat),
        in_specs=[
            pl.BlockSpec((LRU_R, w), lambda b, p, i: (ck(b, p, i), C_LX // w)),
            pl.BlockSpec((HALO, w), lambda b, p, i: (jnp.maximum(ck(b, p, i) * per - 1, 0), C_LX // w)),
            pl.BlockSpec((HALO, w), lambda b, p, i: (jnp.minimum(ck(b, p, i) * per + per, nhalo - 1), C_LX // w)),
            pl.BlockSpec((LRU_R, w), lambda b, p, i: (ck(b, p, i), C_LG // w)),
            full((CONV_W, w)), full((1, w)),
            bydir((w // 128, 128, 256)), bydir((1, w)), bydir((1, w)), bydir((1, w)),
        ],
        out_specs=pl.BlockSpec((LRU_R, w), lambda b, p, i: (oc(b, p, i), 0)),
        out_shape=jax.ShapeDtypeStruct((t, w), BF16),
        scratch_shapes=[pltpu.VMEM((nctx + nlat, LRU_R, w), F32),
                        pltpu.VMEM((8, w), F32),
                        pltpu.VMEM((w // 128, LRU_R, 128), F32), pltpu.VMEM((w // 128, LRU_R, 128), F32),
                        pltpu.VMEM((w // 128, LRU_R, 128), F32), pltpu.VMEM((w // 128, LRU_R, 128), F32)],
        compiler_params=_cparams(("arbitrary", "arbitrary", "arbitrary")),
        name="lru_mixer",
    )(pm, pm, pm, pm, prm["lru_cw"], prm["lru_cb"], prm["lru_wg"], prm["lru_ba"], prm["lru_bx"],
      prm["lru_lam"])


def _qkprep_kernel(q_ref, k_ref, cos_ref, sin_ref, qn_ref, kn_ref, qo_ref, ko_ref, *, nlat_tiles):
    i = pl.program_id(0)
    is_ctx = i >= nlat_tiles
    cos = jnp.where(is_ctx, 1.0, cos_ref[...])
    sin = jnp.where(is_ctx, 0.0, sin_ref[...])
    lane = lax.broadcasted_iota(jnp.int32, cos.shape, 1)
    first = (lane % 64) < 32

    def prep(x, g, scale):
        xn = x * lax.rsqrt(jnp.mean(x * x, axis=-1, keepdims=True) + EPS) * g
        sw = jnp.where(first, pltpu.roll(xn, 96, 1), pltpu.roll(xn, 32, 1))
        return ((xn * cos + sw * sin) * scale).astype(BF16)

    qn, kn = qn_ref[...], kn_ref[...]
    for h in range(ATT_HEADS):
        hs = slice(h * ATT_HEAD_DIM, (h + 1) * ATT_HEAD_DIM)
        qo_ref[:, hs] = prep(q_ref[:, hs].astype(F32), qn, ATT_HEAD_DIM ** -0.5)
    for h in range(ATT_KV_HEADS):
        hs = slice(h * ATT_HEAD_DIM, (h + 1) * ATT_HEAD_DIM)
        ko_ref[:, hs] = prep(k_ref[:, hs].astype(F32), kn, 1.0)


def _qk_prep(pm, cos, sin, prm, dims):
    t = pm.shape[0]
    tm = dims["tm_qk"]
    nlat_tiles = dims["B"] * dims["L"] // tm
    per_seq = dims["L"] // tm
    return pl.pallas_call(
        functools.partial(_qkprep_kernel, nlat_tiles=nlat_tiles),
        grid=(t // tm,),
        in_specs=[
            pl.BlockSpec((tm, 512), lambda i: (i, C_AQ // 512)),
            pl.BlockSpec((tm, 256), lambda i: (i, C_AK // 256)),
            pl.BlockSpec((tm, 128), lambda i: (i % per_seq, 0)),
            pl.BlockSpec((tm, 128), lambda i: (i % per_seq, 0)),
            pl.BlockSpec((1, 128), lambda i: (0, 0)),
            pl.BlockSpec((1, 128), lambda i: (0, 0)),
        ],
        out_specs=[pl.BlockSpec((tm, 512), lambda i: (i, 0)), pl.BlockSpec((tm, 256), lambda i: (i, 0))],
        out_shape=[jax.ShapeDtypeStruct((t, 512), BF16), jax.ShapeDtypeStruct((t, 256), BF16)],
        compiler_params=_cparams(("arbitrary",)),
        name="qk_prep",
    )(pm, pm, cos, sin, prm["att_qnorm"], prm["att_knorm"])


def _attn_kernel(q_ref, kc_ref, vc_ref, *rest, tk, nlat_chunks):
    if nlat_chunks:
        kl_ref, vl_ref, o_ref = rest
    else:
        _, o_ref = rest
    tq = q_ref.shape[0]
    dh = ATT_HEAD_DIM
    q = jnp.concatenate([q_ref[:, :dh], q_ref[:, dh:]], axis=0)

    chunks = [(kc_ref, vc_ref, slice(None))]
    chunks += [(kl_ref, vl_ref, slice(c * tk, (c + 1) * tk)) for c in range(nlat_chunks)]
    m = l = acc = None
    s_next = _dot_nt(q, kc_ref[...])
    for c, (_, v_ref, rows) in enumerate(chunks):
        s = s_next
        if c + 1 < len(chunks):
            k_ref, _, nrows = chunks[c + 1]
            s_next = _dot_nt(q, k_ref[nrows, :])
        mx = jnp.max(s, axis=-1, keepdims=True)
        if m is None:
            m = mx
            pmat = jnp.exp(s - m)
            l = jnp.sum(pmat, axis=-1, keepdims=True)
            acc = _dot(pmat.astype(BF16), v_ref[rows, :])
        else:
            m_new = jnp.maximum(m, mx)
            alpha = jnp.exp(m - m_new)
            pmat = jnp.exp(s - m_new)
            l = alpha * l + jnp.sum(pmat, axis=-1, keepdims=True)
            acc = alpha * acc + _dot(pmat.astype(BF16), v_ref[rows, :])
            m = m_new
    o = acc * (1.0 / l)
    o_ref[...] = jnp.concatenate([o[:tq], o[tq:]], axis=1).astype(BF16)


def _attention(qn, kn, pm, dims, latent, prev=None):
    nb, l, c = dims["B"], dims["L"], dims["C"]
    dh = ATT_HEAD_DIM
    tq = dims["tq"] if latent else c
    nq = (l if latent else c) // tq
    qrow = (lambda b, i: b * nq + i) if latent else (lambda b, i: nb * l // c + b)
    ctx_blk = nb * l // c
    av = C_AV // dh
    in_specs = [
        pl.BlockSpec((tq, 2 * dh), lambda b, g, i: (qrow(b, i), g)),
        pl.BlockSpec((c, dh), lambda b, g, i: (ctx_blk + b, g)),
        pl.BlockSpec((c, dh), lambda b, g, i: (ctx_blk + b, av + g)),
    ]
    args = [qn, kn, pm]
    if latent:
        in_specs += [pl.BlockSpec((l, dh), lambda b, g, i: (b, g)),
                     pl.BlockSpec((l, dh), lambda b, g, i: (b, av + g))]
        args += [kn, pm]
    else:
        in_specs.append(pl.BlockSpec(memory_space=pl.ANY))
        args.append(prev)
    tk = _pick(l, (1024, 512, 256))
    return pl.pallas_call(
        functools.partial(_attn_kernel, tk=tk, nlat_chunks=(l // tk if latent else 0)),
        grid=(nb, ATT_KV_HEADS, nq),
        in_specs=in_specs,
        out_specs=pl.BlockSpec((tq, 2 * dh), lambda b, g, i: (qrow(b, i), g)),
        out_shape=jax.ShapeDtypeStruct((qn.shape[0], ATT_HEADS * dh), BF16),
        input_output_aliases={} if latent else {3: 0},
        compiler_params=_cparams(("arbitrary", "arbitrary", "arbitrary")),
        name="attention_latent" if latent else "attention_context",
    )(*args)


def _merge_kernel(x_ref, ya_ref, yb_ref, yc_ref, yd_ref, mg_ref, g1_ref, sh_ref, sc_ref, n2_ref,
                  wb_ref, wo_ref, rwh_ref, rwl_ref, rb_ref, xo_ref, h_ref, lg_ref):
    d = x_ref.shape[1]
    acc = None
    for nbr, y_ref in enumerate((ya_ref, yb_ref, yc_ref, yd_ref)):
        gate = _sigmoid(mg_ref[:, nbr * d:(nbr + 1) * d].astype(F32))
        term = gate * _dot(y_ref[...], wb_ref[nbr])
        acc = term if acc is None else acc + term
    xn = x_ref[...] + g1_ref[0] * _dot(acc.astype(BF16), wo_ref[...])
    xo_ref[...] = xn
    h = xn * lax.rsqrt(jnp.mean(xn * xn, axis=-1, keepdims=True) + EPS) * n2_ref[...]
    h = h * (1.0 + sc_ref[0]) + sh_ref[0]
    hh = h.astype(BF16)
    hl = (h - hh.astype(F32)).astype(BF16)
    h_ref[...] = hh
    rwh = rwh_ref[...]
    lg_ref[...] = _dot(hh, rwh) + _dot(hl, rwh) + _dot(hh, rwl_ref[...]) + rb_ref[...]


def _merge(x_all, ya, yb, yc, yd, pm, mod3, prm, dims, n_tiles):
    t, d = x_all.shape
    tm = dims["tm_merge"]
    modrow = dims["modrow"]
    row = lambda shape: pl.BlockSpec(shape, lambda i: (i, 0))
    full = lambda shape: pl.BlockSpec(shape, lambda i: (0,) * len(shape))
    modspec = lambda comp: pl.BlockSpec((1, 1, d), lambda i: (modrow(i, tm) * 6 + comp, 0, 0))
    return pl.pallas_call(
        _merge_kernel,
        grid=(n_tiles,),
        in_specs=[
            row((tm, d)), row((tm, BRANCH_W)), row((tm, BRANCH_W)), row((tm, BRANCH_W)), row((tm, BRANCH_W)),
            pl.BlockSpec((tm, 4 * d), lambda i: (i, C_MG // (4 * d))),
            modspec(2), modspec(3), modspec(4), full((1, d)),
            full((4, BRANCH_W, d)), full((d, d)), full((d, 128)), full((d, 128)), full((1, 128)),
        ],
        out_specs=[row((tm, d)), row((tm, d)), row((tm, 128))],
        out_shape=[jax.ShapeDtypeStruct((t, d), F32), jax.ShapeDtypeStruct((t, d), BF16),
                   jax.ShapeDtypeStruct((t, 128), F32)],
        compiler_params=_cparams(("arbitrary",)),
        name="merge",
    )(x_all, ya, yb, yc, yd, pm, mod3, mod3, mod3, prm["norm2"], prm["w_branch"], prm["w_out"],
      prm["rw_hi"], prm["rw_lo"], prm["rb"])


def _route_weights(lg):
    lane = lax.broadcasted_iota(jnp.int32, lg.shape, 1)
    neg = -1e30
    is_g = lane < MOE_GROUPS
    gl = jnp.where(is_g, lg, neg)
    gmax = jnp.max(gl, axis=-1, keepdims=True)
    gsum = jnp.sum(jnp.where(is_g, jnp.exp(gl - gmax), 0.0), axis=-1, keepdims=True)
    gi = jnp.min(jnp.where(is_g & (gl == gmax), lane, 1 << 20), axis=-1, keepdims=True)
    pg_sel = 1.0 / gsum
    lo = MOE_GROUPS + gi * MOE_PER_GROUP
    in_grp = (lane >= lo) & (lane < lo + MOE_PER_GROUP)
    el = jnp.where(in_grp, lg, neg)
    emax = jnp.max(el, axis=-1, keepdims=True)
    ex = jnp.where(in_grp, jnp.exp(el - emax), 0.0)
    pe = ex / jnp.sum(ex, axis=-1, keepdims=True)
    pe_m = jnp.where(in_grp, pe, -1.0)
    v1 = jnp.max(pe_m, axis=-1, keepdims=True)
    i1 = jnp.min(jnp.where(pe_m == v1, lane, 1 << 20), axis=-1, keepdims=True)
    pe_m2 = jnp.where(lane == i1, -1.0, pe_m)
    v2 = jnp.max(pe_m2, axis=-1, keepdims=True)
    i2 = jnp.min(jnp.where(pe_m2 == v2, lane, 1 << 20), axis=-1, keepdims=True)
    tot = v1 + v2
    w = jnp.where(lane == i1, v1 / tot, 0.0) + jnp.where(lane == i2, v2 / tot, 0.0)
    return w * pg_sel


def _moe_kernel(h_ref, lg_ref, x_ref, g2_ref, w1_ref, w3_ref, w2_ref, o_ref, acc_ref, rw_ref):
    e = pl.program_id(1)

    @pl.when(e == 0)
    def _():
        acc_ref[...] = jnp.zeros_like(acc_ref)
        rw_ref[...] = _route_weights(lg_ref[...])

    h = h_ref[...]
    a = _silu(_dot(h, w1_ref[0])) * _dot(h, w3_ref[0])
    y = _dot(a.astype(BF16), w2_ref[0])
    rw = rw_ref[...]
    lane = lax.broadcasted_iota(jnp.int32, rw.shape, 1)
    we = jnp.sum(jnp.where(lane == e + MOE_GROUPS, rw, 0.0), axis=-1, keepdims=True)
    acc_ref[...] += we * y

    @pl.when(e == MOE_EXPERTS - 1)
    def _():
        o_ref[...] = x_ref[...] + g2_ref[0] * acc_ref[...]


def _moe(h2, lg, x_mid, mod3, prm, dims, n_tiles):
    t, d = x_mid.shape
    tm = dims["tm_moe"]
    modrow = dims["modrow"]
    return pl.pallas_call(
        _moe_kernel,
        grid=(n_tiles, MOE_EXPERTS),
        in_specs=[
            pl.BlockSpec((tm, d), lambda i, e: (i, 0)),
            pl.BlockSpec((tm, 128), lambda i, e: (i, 0)),
            pl.BlockSpec((tm, d), lambda i, e: (i, 0)),
            pl.BlockSpec((1, 1, d), lambda i, e: (modrow(i, tm) * 6 + 5, 0, 0)),
            pl.BlockSpec((1, d, MOE_FF), lambda i, e: (e, 0, 0)),
            pl.BlockSpec((1, d, MOE_FF), lambda i, e: (e, 0, 0)),
            pl.BlockSpec((1, MOE_FF, d), lambda i, e: (e, 0, 0)),
        ],
        out_specs=pl.BlockSpec((tm, d), lambda i, e: (i, 0)),
        out_shape=jax.ShapeDtypeStruct((n_tiles * tm, d), F32),
        scratch_shapes=[pltpu.VMEM((tm, d), F32), pltpu.VMEM((tm, 128), F32)],
        compiler_params=_cparams(("arbitrary", "arbitrary")),
        name="moe",
    )(h2, lg, x_mid, mod3, prm["exp_w1"], prm["exp_w3"], prm["exp_w2"])


def _hi_lo(w):
    hi = w.astype(BF16)
    return hi, (w - hi.astype(F32)).astype(BF16)


def _layer_params(l, p):
    d = p["w_in"].shape[1]
    w_in = p["w_in"][l]
    o = 0
    cols = {}
    for name, size in (("z", 512), ("xbc", 1024), ("dt", 16), ("gq", 512), ("gk", 512), ("gv", 512),
                       ("g1", 32), ("gr", 512), ("lx", 512), ("lg", 512), ("aq", 512), ("ak", 256),
                       ("av", 256), ("mg", 4 * d)):
        cols[name] = w_in[:, o:o + size]
        o += size
    out = {}
    out["w_main"] = jnp.concatenate([cols[n] for n in ("mg", "xbc", "z", "gq", "gk", "gv", "gr", "lx", "lg",
                                                        "aq", "ak", "av")], axis=1).astype(BF16)
    zpad = jnp.zeros((d, 128 - 8 - GLA_RANK), F32)
    w_small = jnp.concatenate([cols["dt"][:, :8], cols["g1"][:, :GLA_RANK], zpad,
                               cols["dt"][:, 8:], cols["g1"][:, GLA_RANK:], zpad], axis=1)
    out["ws_hi"], out["ws_lo"] = _hi_lo(w_small)
    out["wst_hi"], out["wst_lo"] = _hi_lo(w_small.T)
    out["norm1"] = p["norm1"][l][None]
    out["norm2"] = p["norm2"][l][None]

    out["ssd_cw"] = p["ssd_conv_w"][l]
    out["ssd_cb"] = p["ssd_conv_b"][l][None]
    pad8 = lambda v: jnp.pad(v, ((0, 0), (0, 128 - SSD_HEADS)))
    brow = pad8(p["ssd_dt_bias"][l])
    arow = pad8(-jnp.exp(p["ssd_a_log"][l]))
    out["ssd_brow"], out["ssd_arow"] = brow[:, None, :], arow[:, None, :]
    out["ssd_bcol"], out["ssd_acol"] = brow[:, :, None], arow[:, :, None]
    head_of_lane = jnp.arange(SSD_INNER) // SSD_HEAD_DIM
    out["ssd_e"] = (jnp.arange(128)[:, None] == head_of_lane[None, :]).astype(BF16)
    out["ssd_dexp"] = jnp.repeat(p["ssd_d"][l], SSD_HEAD_DIM)[None]
    out["ssd_norm"] = p["ssd_norm"][l][None]

    g2 = jnp.zeros((2, 128, GLA_HEADS * GLA_DK), F32).at[:, SM_G1:SM_G1 + GLA_RANK].set(p["gla_g2"][l])
    out["gla_g2h"], out["gla_g2l"] = _hi_lo(g2)
    out["gla_gb"] = p["gla_gb"][l][:, None, :]
    out["gla_norm"] = p["gla_norm"][l][None]

    def pairs(w):
        w = w.reshape(2, LRU_BLOCKS // 2, 2, LRU_BLOCK, LRU_BLOCK)
        z = jnp.zeros_like(w[:, :, 0])
        top = jnp.concatenate([w[:, :, 0], z], axis=-1)
        bot = jnp.concatenate([z, w[:, :, 1]], axis=-1)
        return jnp.concatenate([top, bot], axis=-2)
    out["lru_wg"] = jnp.concatenate([pairs(p["lru_wa"][l]), pairs(p["lru_wx"][l])], axis=-1).astype(BF16)
    out["lru_cw"] = p["lru_conv_w"][l]
    out["lru_cb"] = p["lru_conv_b"][l][None]
    out["lru_ba"] = p["lru_ba"][l][:, None, :]
    out["lru_bx"] = p["lru_bx"][l][:, None, :]
    out["lru_lam"] = p["lru_lambda"][l][:, None, :]

    out["att_qnorm"] = p["att_qnorm"][l][None]
    out["att_knorm"] = p["att_knorm"][l][None]
    out["w_branch"] = p["w_branch"][l].astype(BF16)
    out["w_out"] = p["w_out"][l].astype(BF16)
    rw = jnp.concatenate([p["router_wg"][l], p["router_we"][l],
                          jnp.zeros((d, 128 - MOE_GROUPS - MOE_EXPERTS), F32)], axis=1)
    out["rw_hi"], out["rw_lo"] = _hi_lo(rw)
    out["rb"] = jnp.concatenate([p["router_bg"][l], p["router_be"][l],
                                 jnp.zeros((128 - MOE_GROUPS - MOE_EXPERTS,), F32)])[None]
    out["exp_w1"] = p["exp_w1"][l].astype(BF16)
    out["exp_w3"] = p["exp_w3"][l].astype(BF16)
    out["exp_w2"] = p["exp_w2"][l].astype(BF16)
    return out


def _rope_tables(l):
    f = ATT_HEAD_DIM // 4
    inv = ROPE_THETA ** (-jnp.arange(f, dtype=F32) / f)
    tpos = jnp.arange(l, dtype=jnp.int32)
    row = (tpos // GRID_W).astype(F32)[:, None] * inv
    col = (tpos % GRID_W).astype(F32)[:, None] * inv
    cos = jnp.concatenate([jnp.cos(row), jnp.cos(row), jnp.cos(col), jnp.cos(col)], axis=1)
    sin = jnp.concatenate([-jnp.sin(row), jnp.sin(row), -jnp.sin(col), jnp.sin(col)], axis=1)
    return cos, sin


def _pick(n, cands):
    for c in cands:
        if n % c == 0:
            return c
    raise ValueError(f"no tile size for {n}")


def kernel(x, c, ctx, c_ctx, ada_w, ada_b, norm1, norm2, w_in, ssd_conv_w, ssd_conv_b, ssd_dt_bias, ssd_a_log, ssd_d, ssd_norm, gla_g2, gla_gb, gla_norm, lru_conv_w, lru_conv_b, lru_wa, lru_ba, lru_wx, lru_bx, lru_lambda, att_qnorm, att_knorm, w_branch, w_out, router_wg, router_bg, router_we, router_be, exp_w1, exp_w3, exp_w2):
    nb, l, d = x.shape
    c_len = ctx.shape[1]
    depth = ada_w.shape[0]
    tl, tc = nb * l, nb * c_len
    assert l % LRU_R == 0 and c_len % LRU_R == 0 and nb + 1 <= 8 and l % GRID_W == 0
    params = dict(norm1=norm1, norm2=norm2, w_in=w_in, ssd_conv_w=ssd_conv_w, ssd_conv_b=ssd_conv_b,
                  ssd_dt_bias=ssd_dt_bias, ssd_a_log=ssd_a_log, ssd_d=ssd_d, ssd_norm=ssd_norm,
                  gla_g2=gla_g2, gla_gb=gla_gb, gla_norm=gla_norm, lru_conv_w=lru_conv_w,
                  lru_conv_b=lru_conv_b, lru_wa=lru_wa, lru_ba=lru_ba, lru_wx=lru_wx, lru_bx=lru_bx,
                  lru_lambda=lru_lambda, att_qnorm=att_qnorm, att_knorm=att_knorm, w_branch=w_branch,
                  w_out=w_out, router_wg=router_wg, router_bg=router_bg, router_we=router_we,
                  router_be=router_be, exp_w1=exp_w1, exp_w3=exp_w3, exp_w2=exp_w2)

    tile = _pick(math.gcd(l, tc), (1024, 512, 256))

    def modrow(i, tm):
        return jnp.where(i < tl // tm, i // (l // tm), nb)

    dims = dict(B=nb, L=l, C=c_len, modrow=modrow, tm_proj=tile, tm_merge=min(tile, 512), tm_moe=tile,
                tm_qk=min(tile, 512), tq=_pick(l, (512, 256)))

    cvec = jnp.zeros((8, d), F32).at[:nb].set(c).at[nb].set(c_ctx)
    mod_all = _modulation(cvec, ada_w, ada_b)
    cos, sin = _rope_tables(l)
    x_all = jnp.concatenate([x.reshape(tl, d), ctx.reshape(tc, d)], axis=0)

    for layer in range(depth):
        last = layer == depth - 1
        prm = _layer_params(layer, params)
        mod3 = mod_all[layer].reshape(8 * 6, 1, d)
        pm, sm, smt = _in_projection(x_all, prm["norm1"], mod3, prm["w_main"], prm["ws_hi"], prm["ws_lo"],
                                     prm["wst_hi"], prm["wst_lo"], dims)
        ya = _ssd_mixer(pm, sm, smt, prm, dims)
        yb = _gla_mixer(pm, sm, prm, dims)
        yc = _lru_mixer(pm, prm, dims)
        qn, kn = _qk_prep(pm, cos, sin, prm, dims)
        yd = _attention(qn, kn, pm, dims, latent=True)
        if not last:
            yd = _attention(qn, kn, pm, dims, latent=False, prev=yd)
        n_rows = tl if last else tl + tc
        x_mid, h2, lg = _merge(x_all, ya, yb, yc, yd, pm, mod3, prm, dims, n_rows // dims["tm_merge"])
        x_all = _moe(h2, lg, x_mid, mod3, prm, dims, n_rows // dims["tm_moe"])
    return x_all.reshape(nb, l, d)
```

```python
import functools
import math

import jax
import jax.numpy as jnp
from jax import lax
from jax.experimental import pallas as pl
from jax.experimental.pallas import tpu as pltpu

F32 = jnp.float32
BF16 = jnp.bfloat16

EPS = 1e-6
GRID_W = 64
ROPE_THETA = 10000.0

SSD_HEADS = 8
SSD_HEAD_DIM = 64
SSD_INNER = 512
SSD_GROUPS = 2
SSD_STATE = 128
SSD_XBC = 1024
CONV_W = 4
GLA_HEADS = 4
GLA_DK = 128
GLA_RANK = 16
GLA_GATE_NORM = 16.0
GLA_CHUNK = 64
LRU_W = 512
LRU_BLOCKS = 8
LRU_BLOCK = 64
LRU_C = 8.0
ATT_HEADS = 4
ATT_KV_HEADS = 2
ATT_HEAD_DIM = 128
MOE_GROUPS = 4
MOE_PER_GROUP = 4
MOE_EXPERTS = 16
MOE_FF = 512
BRANCH_W = 512

C_MG, C_XBC, C_Z, C_GQ, C_GK, C_GV, C_GR = 0, 4096, 5120, 5632, 6144, 6656, 7168
C_LX, C_LG, C_AQ, C_AK, C_AV, N_MAIN = 7680, 8192, 8704, 9216, 9472, 9728
N_SMALL = 256
SM_DT, SM_G1 = 0, 8

HALO = 16
SSD_Q = 128
GLA_R = 256
LRU_R = 256
LRU_SEG = LRU_R // 8

VMEM_LIMIT = 56 * 1024 * 1024

_NT = (((1,), (1,)), ((), ()))
_TN = (((0,), (0,)), ((), ()))


def _dot(a, b):
    return jnp.dot(a, b, preferred_element_type=F32)


def _dot_nt(a, b):
    return lax.dot_general(a, b, _NT, preferred_element_type=F32)


def _dot_tn(a, b):
    return lax.dot_general(a, b, _TN, preferred_element_type=F32)


def _split3(x):
    h = x.astype(BF16)
    r = x - h.astype(F32)
    m = r.astype(BF16)
    l = (r - m.astype(F32)).astype(BF16)
    return h, m, l


def _dot_mask_lhs(mask_bf16, x):
    h, m, l = _split3(x)
    return _dot(mask_bf16, h) + _dot(mask_bf16, m) + _dot(mask_bf16, l)


def _dot_mask_rhs(x, mask_bf16):
    h, m, l = _split3(x)
    return _dot(h, mask_bf16) + _dot(m, mask_bf16) + _dot(l, mask_bf16)


def _sigmoid(x):
    return 0.5 * jnp.tanh(0.5 * x) + 0.5


def _silu(x):
    return x * _sigmoid(x)


def _softplus(x):
    return jnp.maximum(x, 0.0) + jnp.log1p(jnp.exp(-jnp.abs(x)))


def _cparams(sem):
    return pltpu.CompilerParams(dimension_semantics=sem, vmem_limit_bytes=VMEM_LIMIT)


def _mod_kernel(c_ref, w_ref, b_ref, o_ref):
    s = _silu(c_ref[...])
    sh, sm, sl = _split3(s)
    w = w_ref[0]
    wh = w.astype(BF16)
    wl = (w - wh.astype(F32)).astype(BF16)
    o_ref[0] = _dot(sh, wh) + _dot(sm, wh) + _dot(sh, wl) + b_ref[0]


def _modulation(cvec, ada_w, ada_b):
    ld, d, n6 = ada_w.shape
    tn = 1536
    return pl.pallas_call(
        _mod_kernel,
        grid=(ld, n6 // tn),
        in_specs=[
            pl.BlockSpec((8, d), lambda l, j: (0, 0)),
            pl.BlockSpec((1, d, tn), lambda l, j: (l, 0, j)),
            pl.BlockSpec((1, 1, tn), lambda l, j: (l, 0, j)),
        ],
        out_specs=pl.BlockSpec((1, 8, tn), lambda l, j: (l, 0, j)),
        out_shape=jax.ShapeDtypeStruct((ld, 8, n6), F32),
        compiler_params=_cparams(("arbitrary", "arbitrary")),
        name="modulation",
    )(cvec, ada_w, ada_b.reshape(ld, 1, n6))


def _inproj_kernel(x_ref, g_ref, sh_ref, sc_ref, w_ref, wsh_ref, wsl_ref, wsth_ref, wstl_ref,
                   p_ref, s_ref, st_ref, h_ref):
    j = pl.program_id(1)

    @pl.when(j == 0)
    def _():
        x = x_ref[...]
        h = x * lax.rsqrt(jnp.mean(x * x, axis=-1, keepdims=True) + EPS) * g_ref[...]
        h = h * (1.0 + sc_ref[0]) + sh_ref[0]
        hh = h.astype(BF16)
        hl = (h - hh.astype(F32)).astype(BF16)
        h_ref[...] = hh
        wsh = wsh_ref[...]
        s_ref[...] = _dot(hh, wsh) + _dot(hl, wsh) + _dot(hh, wsl_ref[...])
        wsth = wsth_ref[...]
        st_ref[...] = _dot_nt(wsth, hh) + _dot_nt(wsth, hl) + _dot_nt(wstl_ref[...], hh)

    p_ref[...] = _dot(h_ref[...], w_ref[...]).astype(BF16)


def _in_projection(x_all, norm_g, mod3, w_main, ws_hi, ws_lo, wst_hi, wst_lo, dims):
    t, d = x_all.shape
    tm, tn = dims["tm_proj"], N_MAIN // 4
    modrow = dims["modrow"]
    return pl.pallas_call(
        _inproj_kernel,
        grid=(t // tm, N_MAIN // tn),
        in_specs=[
            pl.BlockSpec((tm, d), lambda i, j: (i, 0)),
            pl.BlockSpec((1, d), lambda i, j: (0, 0)),
            pl.BlockSpec((1, 1, d), lambda i, j: (modrow(i, tm) * 6 + 0, 0, 0)),
            pl.BlockSpec((1, 1, d), lambda i, j: (modrow(i, tm) * 6 + 1, 0, 0)),
            pl.BlockSpec((d, tn), lambda i, j: (0, j)),
            pl.BlockSpec((d, N_SMALL), lambda i, j: (0, 0)),
            pl.BlockSpec((d, N_SMALL), lambda i, j: (0, 0)),
            pl.BlockSpec((N_SMALL, d), lambda i, j: (0, 0)),
            pl.BlockSpec((N_SMALL, d), lambda i, j: (0, 0)),
        ],
        out_specs=[
            pl.BlockSpec((tm, tn), lambda i, j: (i, j)),
            pl.BlockSpec((tm, N_SMALL), lambda i, j: (i, 0)),
            pl.BlockSpec((N_SMALL, tm), lambda i, j: (0, i)),
        ],
        out_shape=[
            jax.ShapeDtypeStruct((t, N_MAIN), BF16),
            jax.ShapeDtypeStruct((t, N_SMALL), F32),
            jax.ShapeDtypeStruct((N_SMALL, t), F32),
        ],
        scratch_shapes=[pltpu.VMEM((tm, d), BF16)],
        compiler_params=_cparams(("arbitrary", "arbitrary")),
        name="in_projection",
    )(x_all, norm_g, mod3, mod3, w_main, ws_hi, ws_lo, wst_hi, wst_lo)


def _seq_pos(p, i, nctx, nlat):
    is_ctx = i < nctx
    k = jnp.where(is_ctx, jnp.where(p == 0, i, nctx - 1 - i),
                  jnp.where(p == 0, i - nctx, nlat - 1 - (i - nctx)))
    return k, jnp.where(is_ctx, nctx, nlat)


def _seq_chunk(b, p, i, nctx, nlat, nb):
    k, _ = _seq_pos(p, i, nctx, nlat)
    return jnp.where(i < nctx, nb * nlat + b * nctx + k, b * nlat + k)


def _seq_out_chunk(b, p, i, nctx, nlat, nb):
    return _seq_chunk(b, 1, jnp.where(p == 0, 0, i), nctx, nlat, nb)


def _fwd_slot(i, nctx, nlat):
    return jnp.where(i < nctx, nctx - 1 - i, nctx + nlat - 1 - (i - nctx))


def _conv4(prev_ref, cur_ref, next_ref, w, b, pv, nv, rows):
    xe = jnp.concatenate([prev_ref[...].astype(F32) * pv, cur_ref[...].astype(F32),
                          next_ref[...].astype(F32) * nv], axis=0)
    n = rows + 2 * HALO
    y = (pltpu.roll(xe, 2, 0) * w[0:1] + pltpu.roll(xe, 1, 0) * w[1:2] + xe * w[2:3]
         + pltpu.roll(xe, n - 1, 0) * w[3:4])
    return y[HALO:HALO + rows] + b


def _dir_mask(p, n):
    r = lax.broadcasted_iota(jnp.int32, (n, n), 0)
    c = lax.broadcasted_iota(jnp.int32, (n, n), 1)
    sgn = 1 - 2 * p
    return (r - c) * sgn >= 0


def _ssd_kernel(xc_ref, xp_ref, xn_ref, z_ref, sm_ref, smt_ref, cw_ref, cb_ref, brow_ref, arow_ref,
                bcol_ref, acol_ref, e_ref, dexp_ref, nrm_ref, o_ref, yf_ref, ht_ref, *, nctx, nlat):
    q = SSD_Q
    p = pl.program_id(1)
    i = pl.program_id(2)
    k, n = _seq_pos(p, i, nctx, nlat)
    pv = (k > 0).astype(F32)
    nv = (k < n - 1).astype(F32)

    @pl.when(i == 0)
    def _():
        ht_ref[...] = jnp.zeros_like(ht_ref)

    xbc = _silu(_conv4(xp_ref, xc_ref, xn_ref, cw_ref[...], cb_ref[...], pv, nv, q))
    xs = xbc[:, :SSD_INNER]
    bmat = xbc[:, SSD_INNER:SSD_INNER + 256].astype(BF16)
    cmat = xbc[:, SSD_INNER + 256:].astype(BF16)

    mask = _dir_mask(p, q)
    mask_t = _dir_mask(1 - p, q)
    tri = jnp.where(mask, 1.0, 0.0).astype(BF16)
    tri_t = jnp.where(mask_t, 1.0, 0.0).astype(BF16)

    dt_c = _softplus(sm_ref[...] + brow_ref[0])
    dta_c = dt_c * arow_ref[0]
    acum_c = _dot_mask_lhs(tri, dta_c)
    alast_c = jnp.sum(dta_c, axis=0, keepdims=True)
    dt_r = _softplus(smt_ref[...] + bcol_ref[0])
    acum_r = _dot_mask_rhs(dt_r * acol_ref[0], tri_t)

    e = e_ref[...]
    stack = jnp.concatenate([dt_c, jnp.exp(acum_c), dt_c * jnp.exp(alast_c - acum_c)], axis=0)
    ex = _dot(stack.astype(BF16), e)
    dt_e, ea_e, ds_e = ex[:q], ex[q:2 * q], ex[2 * q:]
    dec_e = _dot_mask_rhs(jnp.broadcast_to(jnp.exp(alast_c), (8, 128)), e)[0:1]

    dtx = (xs * dt_e).astype(BF16)
    dsx = (xs * ds_e).astype(BF16)
    lane = lax.broadcasted_iota(jnp.int32, (q, 128), 1)
    ydiag, yoff = [], []
    for g in range(SSD_GROUPS):
        bg = bmat[:, g * 128:(g + 1) * 128]
        cg = cmat[:, g * 128:(g + 1) * 128]
        cb = _dot_nt(cg, bg)
        htg = ht_ref[g]
        yoff.append(_dot(cg, htg.astype(BF16)))
        ht_ref[g] = htg * dec_e[:, g * 256:(g + 1) * 256] + _dot_tn(bg, dsx[:, g * 256:(g + 1) * 256])
        for pr in range(2):
            hd = g * 4 + pr * 2
            pair = dtx[:, hd * 64:hd * 64 + 128]
            outs = []
            for hh in (hd, hd + 1):
                seg = acum_c[:, hh:hh + 1] - acum_r[hh:hh + 1, :]
                lm = jnp.exp(jnp.where(mask, seg, -1e30))
                outs.append(_dot((cb * lm).astype(BF16), pair))
            ydiag.append(jnp.where(lane < 64, outs[0], outs[1]))
    y = jnp.concatenate(ydiag, axis=1) + jnp.concatenate(yoff, axis=1) * ea_e

    @pl.when(p == 0)
    def _():
        yf_ref[i] = y

    @pl.when(p == 1)
    def _():
        yt = (y + yf_ref[_fwd_slot(i, nctx, nlat)] + dexp_ref[...] * xs) * _silu(z_ref[...].astype(F32))
        yn = yt * lax.rsqrt(jnp.mean(yt * yt, axis=-1, keepdims=True) + EPS) * nrm_ref[...]
        o_ref[...] = yn.astype(BF16)


def _ssd_mixer(pm, sm, smt, prm, dims):
    t = pm.shape[0]
    nb, nctx, nlat = dims["B"], dims["C"] // SSD_Q, dims["L"] // SSD_Q
    nhalo = t // HALO
    per = SSD_Q // HALO
    ck = functools.partial(_seq_chunk, nctx=nctx, nlat=nlat, nb=nb)
    oc = functools.partial(_seq_out_chunk, nctx=nctx, nlat=nlat, nb=nb)
    full = lambda shape: pl.BlockSpec(shape, lambda b, p, i: (0,) * len(shape))
    bydir = lambda shape: pl.BlockSpec((1,) + shape, lambda b, p, i: (p,) + (0,) * len(shape))
    return pl.pallas_call(
        functools.partial(_ssd_kernel, nctx=nctx, nlat=nlat),
        grid=(nb, 2, nctx + nlat),
        in_specs=[
            pl.BlockSpec((SSD_Q, SSD_XBC), lambda b, p, i: (ck(b, p, i), C_XBC // SSD_XBC)),
            pl.BlockSpec((HALO, SSD_XBC),
                         lambda b, p, i: (jnp.maximum(ck(b, p, i) * per - 1, 0), C_XBC // SSD_XBC)),
            pl.BlockSpec((HALO, SSD_XBC),
                         lambda b, p, i: (jnp.minimum(ck(b, p, i) * per + per, nhalo - 1), C_XBC // SSD_XBC)),
            pl.BlockSpec((SSD_Q, SSD_INNER), lambda b, p, i: (ck(b, p, i), C_Z // SSD_INNER)),
            pl.BlockSpec((SSD_Q, 128), lambda b, p, i: (ck(b, p, i), p)),
            pl.BlockSpec((128, SSD_Q), lambda b, p, i: (p, ck(b, p, i))),
            full((CONV_W, SSD_XBC)), full((1, SSD_XBC)),
            bydir((1, 128)), bydir((1, 128)), bydir((128, 1)), bydir((128, 1)),
            full((128, SSD_INNER)), full((1, SSD_INNER)), full((1, SSD_INNER)),
        ],
        out_specs=pl.BlockSpec((SSD_Q, SSD_INNER), lambda b, p, i: (oc(b, p, i), 0)),
        out_shape=jax.ShapeDtypeStruct((t, SSD_INNER), BF16),
        scratch_shapes=[pltpu.VMEM((nctx + nlat, SSD_Q, SSD_INNER), F32),
                        pltpu.VMEM((SSD_GROUPS, SSD_STATE, 256), F32)],
        compiler_params=_cparams(("arbitrary", "arbitrary", "arbitrary")),
        name="ssd_mixer",
    )(pm, pm, pm, pm, sm, smt, prm["ssd_cw"], prm["ssd_cb"], prm["ssd_brow"], prm["ssd_arow"],
      prm["ssd_bcol"], prm["ssd_acol"], prm["ssd_e"], prm["ssd_dexp"], prm["ssd_norm"])


def _gla_kernel(q_ref, k_ref, v_ref, r_ref, sm_ref, g2h_ref, g2l_ref, gb_ref, nrm_ref,
                o_ref, yf_ref, st_ref, *, nctx, nlat):
    qc = GLA_CHUNK
    nsub = GLA_R // qc
    p = pl.program_id(1)
    i = pl.program_id(2)

    @pl.when(i == 0)
    def _():
        st_ref[...] = jnp.zeros_like(st_ref)

    r = GLA_R
    rr = lax.broadcasted_iota(jnp.int32, (r, r), 0)
    cc = lax.broadcasted_iota(jnp.int32, (r, r), 1)
    shift = qc.bit_length() - 1
    mask = (lax.shift_right_logical(rr, shift) == lax.shift_right_logical(cc, shift)) & ((rr - cc) * (1 - 2 * p) >= 0)
    tri = jnp.where(mask, 1.0, 0.0).astype(BF16)
    scale = GLA_DK ** -0.5

    sm = sm_ref[...]
    smh = sm.astype(BF16)
    sml = (sm - smh.astype(F32)).astype(BF16)
    g2h = g2h_ref[0]
    logit = _dot(smh, g2h) + _dot(sml, g2h) + _dot(smh, g2l_ref[0]) + gb_ref[0]
    g = -_softplus(-logit) * (1.0 / GLA_GATE_NORM)
    gc = _dot_mask_lhs(tri, g)
    glast = [jnp.sum(g[s * qc:(s + 1) * qc], axis=0, keepdims=True) for s in range(nsub)]
    bcast = lambda rows_: jnp.concatenate([jnp.broadcast_to(x, (qc, x.shape[1])) for x in rows_], axis=0)
    gref = bcast([gc[s * qc + qc // 2:s * qc + qc // 2 + 1] for s in range(nsub)])
    glast_f = bcast(glast)
    qf = q_ref[...].astype(F32) * scale
    kf = k_ref[...].astype(F32)
    vb = v_ref[...]
    qe = (qf * jnp.exp(gc - gref)).astype(BF16)
    ke = (kf * jnp.exp(gref - gc)).astype(BF16)
    qg = (qf * jnp.exp(gc)).astype(BF16)
    k2 = (kf * jnp.exp(glast_f - gc)).astype(BF16)
    o_intra = []
    for h in range(GLA_HEADS):
        hs = slice(h * GLA_DK, (h + 1) * GLA_DK)
        att = jnp.where(mask, _dot_nt(qe[:, hs], ke[:, hs]), 0.0).astype(BF16)
        o_intra.append(_dot(att, vb[:, hs]))

    def sweep(order, emit):
        st = [st_ref[h] for h in range(GLA_HEADS)]
        for s in order:
            rs = slice(s * qc, (s + 1) * qc)
            dec = jnp.exp(glast[s])
            outs = []
            for h in range(GLA_HEADS):
                hs = slice(h * GLA_DK, (h + 1) * GLA_DK)
                outs.append(o_intra[h][rs] + _dot_nt(qg[rs, hs], st[h].astype(BF16)))
                st[h] = st[h] * dec[:, hs] + _dot_tn(vb[rs, hs], k2[rs, hs])
            emit(rs, jnp.concatenate(outs, axis=1))
        for h in range(GLA_HEADS):
            st_ref[h] = st[h]

    @pl.when(p == 0)
    def _():
        def emit(rs, y):
            yf_ref[i, rs, :] = y
        sweep(range(nsub), emit)

    @pl.when(p == 1)
    def _():
        slot = _fwd_slot(i, nctx, nlat)
        nrm = nrm_ref[...]

        def emit(rs, y):
            yt = y + yf_ref[slot, rs, :]
            parts = []
            for h in range(GLA_HEADS):
                hs = slice(h * GLA_DK, (h + 1) * GLA_DK)
                yh = yt[:, hs]
                parts.append(yh * lax.rsqrt(jnp.mean(yh * yh, axis=-1, keepdims=True) + EPS) * nrm[:, hs])
            o_ref[rs, :] = (jnp.concatenate(parts, axis=1) * _silu(r_ref[rs, :].astype(F32))).astype(BF16)
        sweep(range(nsub - 1, -1, -1), emit)


def _gla_mixer(pm, sm, prm, dims):
    t = pm.shape[0]
    nb, nctx, nlat = dims["B"], dims["C"] // GLA_R, dims["L"] // GLA_R
    ck = functools.partial(_seq_chunk, nctx=nctx, nlat=nlat, nb=nb)
    oc = functools.partial(_seq_out_chunk, nctx=nctx, nlat=nlat, nb=nb)
    w = GLA_HEADS * GLA_DK
    col = lambda c: pl.BlockSpec((GLA_R, w), lambda b, p, i: (ck(b, p, i), c // w))
    bydir = lambda shape: pl.BlockSpec((1,) + shape, lambda b, p, i: (p,) + (0,) * len(shape))
    return pl.pallas_call(
        functools.partial(_gla_kernel, nctx=nctx, nlat=nlat),
        grid=(nb, 2, nctx + nlat),
        in_specs=[
            col(C_GQ), col(C_GK), col(C_GV), col(C_GR),
            pl.BlockSpec((GLA_R, 128), lambda b, p, i: (ck(b, p, i), p)),
            bydir((128, w)), bydir((128, w)), bydir((1, w)),
            pl.BlockSpec((1, w), lambda b, p, i: (0, 0)),
        ],
        out_specs=pl.BlockSpec((GLA_R, w), lambda b, p, i: (oc(b, p, i), 0)),
        out_shape=jax.ShapeDtypeStruct((t, w), BF16),
        scratch_shapes=[pltpu.VMEM((nctx + nlat, GLA_R, w), F32),
                        pltpu.VMEM((GLA_HEADS, GLA_DK, GLA_DK), F32)],
        compiler_params=_cparams(("arbitrary", "arbitrary", "arbitrary")),
        name="gla_mixer",
    )(pm, pm, pm, pm, sm, prm["gla_g2h"], prm["gla_g2l"], prm["gla_gb"], prm["gla_norm"])


def _lru_kernel(xc_ref, xp_ref, xn_ref, gl_ref, cw_ref, cb_ref, wg_ref, ba_ref, bx_ref, lam_ref,
                o_ref, yf_ref, h_ref, a_scr, v_scr, as_scr, hs_scr, *, nctx, nlat):
    r, seg = LRU_R, LRU_SEG
    p = pl.program_id(1)
    i = pl.program_id(2)
    k, n = _seq_pos(p, i, nctx, nlat)
    pv = (k > 0).astype(F32)
    nv = (k < n - 1).astype(F32)

    @pl.when(i == 0)
    def _():
        h_ref[...] = jnp.zeros_like(h_ref)

    u = _conv4(xp_ref, xc_ref, xn_ref, cw_ref[...], cb_ref[...], pv, nv, r)
    ub = u.astype(BF16)
    ra, ix = [], []
    for j in range(LRU_W // 128):
        gj = _dot(ub[:, j * 128:(j + 1) * 128], wg_ref[0, j])
        ra.append(gj[:, :128])
        ix.append(gj[:, 128:])
    rg = _sigmoid(jnp.concatenate(ra, axis=1) + ba_ref[0])
    ig = _sigmoid(jnp.concatenate(ix, axis=1) + bx_ref[0])
    log_a = (-LRU_C * _softplus(-lam_ref[0])) * rg
    a_all = jnp.exp(log_a)
    th = jnp.tanh(log_a)
    v_all = u * ig * jnp.sqrt(-2.0 * th / (1.0 - th))
    nslab = LRU_W // 128
    for j in range(nslab):
        a_scr[j] = a_all[:, j * 128:(j + 1) * 128]
        v_scr[j] = v_all[:, j * 128:(j + 1) * 128]

    def scan(order, seg_order):
        for j in range(nslab):
            ls = slice(j * 128, (j + 1) * 128)
            acc_a = jnp.ones((8, 128), F32)
            acc_h = jnp.zeros((8, 128), F32)
            for kk in order:
                ak = a_scr[j, pl.ds(kk, 8, stride=seg), :]
                vk = v_scr[j, pl.ds(kk, 8, stride=seg), :]
                acc_h = ak * acc_h + vk
                acc_a = ak * acc_a
                as_scr[j, kk * 8:(kk + 1) * 8, :] = acc_a
                hs_scr[j, kk * 8:(kk + 1) * 8, :] = acc_h
            carry = h_ref[0:1, ls]
            rows = [None] * 8
            for s in seg_order:
                rows[s] = carry
                carry = acc_a[s:s + 1, :] * carry + acc_h[s:s + 1, :]
            h_ref[0:1, ls] = carry
            cin = jnp.concatenate(rows, axis=0)
            for kk in order:
                ks = slice(kk * 8, (kk + 1) * 8)
                hs_scr[j, ks, :] = hs_scr[j, ks, :] + as_scr[j, ks, :] * cin

    @pl.when(p == 0)
    def _():
        scan(range(seg), range(8))

    @pl.when(p == 1)
    def _():
        scan(range(seg - 1, -1, -1), range(7, -1, -1))

    def natural(s):
        return jnp.concatenate([hs_scr[j, pl.ds(s, seg, stride=8), :] for j in range(nslab)], axis=1)

    @pl.when(p == 0)
    def _():
        for s in range(8):
            yf_ref[i, s * seg:(s + 1) * seg, :] = natural(s)

    @pl.when(p == 1)
    def _():
        slot = _fwd_slot(i, nctx, nlat)
        c0 = math.sqrt(2.0 / math.pi)
        for s in range(8):
            rs = slice(s * seg, (s + 1) * seg)
            gt = gl_ref[rs, :].astype(F32)
            gelu = 0.5 * gt * (1.0 + jnp.tanh(c0 * (gt + 0.044715 * (gt * gt * gt))))
            o_ref[rs, :] = ((natural(s) + yf_ref[slot, rs, :]) * gelu).astype(BF16)


def _lru_mixer(pm, prm, dims):
    t = pm.shape[0]
    nb, nctx, nlat = dims["B"], dims["C"] // LRU_R, dims["L"] // LRU_R
    nhalo = t // HALO
    per = LRU_R // HALO
    ck = functools.partial(_seq_chunk, nctx=nctx, nlat=nlat, nb=nb)
    oc = functools.partial(_seq_out_chunk, nctx=nctx, nlat=nlat, nb=nb)
    w = LRU_W
    full = lambda shape: pl.BlockSpec(shape, lambda b, p, i: (0,) * len(shape))
    bydir = lambda shape: pl.BlockSpec((1,) + shape, lambda b, p, i: (p,) + (0,) * len(shape))
    return pl.pallas_call(
        functools.partial(_lru_kernel, nctx=nctx, nlat=nlat),
        grid=(nb, 2, nctx + nlat),
        in_specs=[
            pl.BlockSpec((LRU_R, w), lambda b, p, i: (ck(b, p, i), C_LX // w)),
            pl.BlockSpec((HALO, w), lambda b, p, i: (jnp.maximum(ck(b, p, i) * per - 1, 0), C_LX // w)),
            pl.BlockSpec((HALO, w), lambda b, p, i: (jnp.minimum(ck(b, p, i) * per + per, nhalo - 1), C_LX // w)),
            pl.BlockSpec((LRU_R, w), lambda b, p, i: (ck(b, p, i), C_LG // w)),
            full((CONV_W, w)), full((1, w)),
            bydir((w // 128, 128, 256)), bydir((1, w)), bydir((1, w)), bydir((1, w)),
        ],
        out_specs=pl.BlockSpec((LRU_R, w), lambda b, p, i: (oc(b, p, i), 0)),
        out_shape=jax.ShapeDtypeStruct((t, w), BF16),
        scratch_shapes=[pltpu.VMEM((nctx + nlat, LRU_R, w), F32),
                        pltpu.VMEM((8, w), F32),
                        pltpu.VMEM((w // 128, LRU_R, 128), F32), pltpu.VMEM((w // 128, LRU_R, 128), F32),
                        pltpu.VMEM((w // 128, LRU_R, 128), F32), pltpu.VMEM((w // 128, LRU_R, 128), F32)],
        compiler_params=_cparams(("arbitrary", "arbitrary", "arbitrary")),
        name="lru_mixer",
    )(pm, pm, pm, pm, prm["lru_cw"], prm["lru_cb"], prm["lru_wg"], prm["lru_ba"], prm["lru_bx"],
      prm["lru_lam"])


def _qkprep_kernel(q_ref, k_ref, cos_ref, sin_ref, qn_ref, kn_ref, qo_ref, ko_ref, *, nlat_tiles):
    i = pl.program_id(0)
    is_ctx = i >= nlat_tiles
    cos = jnp.where(is_ctx, 1.0, cos_ref[...])
    sin = jnp.where(is_ctx, 0.0, sin_ref[...])
    lane = lax.broadcasted_iota(jnp.int32, cos.shape, 1)
    first = (lane % 64) < 32

    def prep(x, g, scale):
        xn = x * lax.rsqrt(jnp.mean(x * x, axis=-1, keepdims=True) + EPS) * g
        sw = jnp.where(first, pltpu.roll(xn, 96, 1), pltpu.roll(xn, 32, 1))
        return ((xn * cos + sw * sin) * scale).astype(BF16)

    qn, kn = qn_ref[...], kn_ref[...]
    for h in range(ATT_HEADS):
        hs = slice(h * ATT_HEAD_DIM, (h + 1) * ATT_HEAD_DIM)
        qo_ref[:, hs] = prep(q_ref[:, hs].astype(F32), qn, ATT_HEAD_DIM ** -0.5)
    for h in range(ATT_KV_HEADS):
        hs = slice(h * ATT_HEAD_DIM, (h + 1) * ATT_HEAD_DIM)
        ko_ref[:, hs] = prep(k_ref[:, hs].astype(F32), kn, 1.0)


def _qk_prep(pm, cos, sin, prm, dims):
    t = pm.shape[0]
    tm = dims["tm_qk"]
    nlat_tiles = dims["B"] * dims["L"] // tm
    per_seq = dims["L"] // tm
    return pl.pallas_call(
        functools.partial(_qkprep_kernel, nlat_tiles=nlat_tiles),
        grid=(t // tm,),
        in_specs=[
            pl.BlockSpec((tm, 512), lambda i: (i, C_AQ // 512)),
            pl.BlockSpec((tm, 256), lambda i: (i, C_AK // 256)),
            pl.BlockSpec((tm, 128), lambda i: (i % per_seq, 0)),
            pl.BlockSpec((tm, 128), lambda i: (i % per_seq, 0)),
            pl.BlockSpec((1, 128), lambda i: (0, 0)),
            pl.BlockSpec((1, 128), lambda i: (0, 0)),
        ],
        out_specs=[pl.BlockSpec((tm, 512), lambda i: (i, 0)), pl.BlockSpec((tm, 256), lambda i: (i, 0))],
        out_shape=[jax.ShapeDtypeStruct((t, 512), BF16), jax.ShapeDtypeStruct((t, 256), BF16)],
        compiler_params=_cparams(("arbitrary",)),
        name="qk_prep",
    )(pm, pm, cos, sin, prm["att_qnorm"], prm["att_knorm"])


def _attn_kernel(q_ref, kc_ref, vc_ref, *rest, tk, nlat_chunks):
    if nlat_chunks:
        kl_ref, vl_ref, o_ref = rest
    else:
        (o_ref,) = rest
    tq = q_ref.shape[0]
    dh = ATT_HEAD_DIM
    q = jnp.concatenate([q_ref[:, :dh], q_ref[:, dh:]], axis=0)

    chunks = [(kc_ref, vc_ref, slice(None))]
    chunks += [(kl_ref, vl_ref, slice(c * tk, (c + 1) * tk)) for c in range(nlat_chunks)]
    m = l = acc = None
    s_next = _dot_nt(q, kc_ref[...])
    for c, (_, v_ref, rows) in enumerate(chunks):
        s = s_next
        if c + 1 < len(chunks):
            k_ref, _, nrows = chunks[c + 1]
            s_next = _dot_nt(q, k_ref[nrows, :])
        mx = jnp.max(s, axis=-1, keepdims=True)
        if m is None:
            m = mx
            pmat = jnp.exp(s - m)
            l = jnp.sum(pmat, axis=-1, keepdims=True)
            acc = _dot(pmat.astype(BF16), v_ref[rows, :])
        else:
            m_new = jnp.maximum(m, mx)
            alpha = jnp.exp(m - m_new)
            pmat = jnp.exp(s - m_new)
            l = alpha * l + jnp.sum(pmat, axis=-1, keepdims=True)
            acc = alpha * acc + _dot(pmat.astype(BF16), v_ref[rows, :])
            m = m_new
    o = acc * (1.0 / l)
    o_ref[...] = jnp.concatenate([o[:tq], o[tq:]], axis=1).astype(BF16)


def _attention(qn, kn, pm, dims, latent):
    nb, l, c = dims["B"], dims["L"], dims["C"]
    dh = ATT_HEAD_DIM
    tq = dims["tq"] if latent else c
    nq = (l if latent else c) // tq
    ctx_blk = nb * l // c
    qrow = (lambda b, i: b * nq + i) if latent else (lambda b, i: ctx_blk + b)
    av = C_AV // dh
    in_specs = [
        pl.BlockSpec((tq, 2 * dh), lambda b, g, i: (qrow(b, i), g)),
        pl.BlockSpec((c, dh), lambda b, g, i: (ctx_blk + b, g)),
        pl.BlockSpec((c, dh), lambda b, g, i: (ctx_blk + b, av + g)),
    ]
    args = [qn, kn, pm]
    if latent:
        in_specs += [pl.BlockSpec((l, dh), lambda b, g, i: (b, g)),
                     pl.BlockSpec((l, dh), lambda b, g, i: (b, av + g))]
        args += [kn, pm]
    tk = _pick(l, (1024, 512, 256))
    return pl.pallas_call(
        functools.partial(_attn_kernel, tk=tk, nlat_chunks=(l // tk if latent else 0)),
        grid=(nb, ATT_KV_HEADS, nq),
        in_specs=in_specs,
        out_specs=pl.BlockSpec((tq, 2 * dh), lambda b, g, i: (b * nq + i, g)),
        out_shape=jax.ShapeDtypeStruct((nb * (l if latent else c), ATT_HEADS * dh), BF16),
        compiler_params=_cparams(("arbitrary", "arbitrary", "arbitrary")),
        name="attention_latent" if latent else "attention_context",
    )(*args)


def _route_weights(lg):
    lane = lax.broadcasted_iota(jnp.int32, lg.shape, 1)
    neg = -1e30
    is_g = lane < MOE_GROUPS
    gl = jnp.where(is_g, lg, neg)
    gmax = jnp.max(gl, axis=-1, keepdims=True)
    gsum = jnp.sum(jnp.where(is_g, jnp.exp(gl - gmax), 0.0), axis=-1, keepdims=True)
    gi = jnp.min(jnp.where(is_g & (gl == gmax), lane, 1 << 20), axis=-1, keepdims=True)
    pg_sel = 1.0 / gsum
    lo = MOE_GROUPS + gi * MOE_PER_GROUP
    in_grp = (lane >= lo) & (lane < lo + MOE_PER_GROUP)
    el = jnp.where(in_grp, lg, neg)
    emax = jnp.max(el, axis=-1, keepdims=True)
    ex = jnp.where(in_grp, jnp.exp(el - emax), 0.0)
    pe = ex / jnp.sum(ex, axis=-1, keepdims=True)
    pe_m = jnp.where(in_grp, pe, -1.0)
    v1 = jnp.max(pe_m, axis=-1, keepdims=True)
    i1 = jnp.min(jnp.where(pe_m == v1, lane, 1 << 20), axis=-1, keepdims=True)
    pe_m2 = jnp.where(lane == i1, -1.0, pe_m)
    v2 = jnp.max(pe_m2, axis=-1, keepdims=True)
    i2 = jnp.min(jnp.where(pe_m2 == v2, lane, 1 << 20), axis=-1, keepdims=True)
    tot = v1 + v2
    w = jnp.where(lane == i1, v1 / tot, 0.0) + jnp.where(lane == i2, v2 / tot, 0.0)
    return w * pg_sel, gi


def _merge_kernel(x_ref, ya_ref, yb_ref, yc_ref, ydl_ref, ydc_ref, mg_ref, g1_ref, sh_ref, sc_ref, n2_ref,
                  wb_ref, wo_ref, rwh_ref, rwl_ref, rb_ref, xo_ref, h_ref, plan_ref, cnt_ref, run_ref,
                  *, nlat_tiles):
    i = pl.program_id(0)
    tm, d = x_ref.shape

    @pl.when(i == 0)
    def _():
        run_ref[...] = jnp.zeros_like(run_ref)

    yd = jnp.where(i < nlat_tiles, ydl_ref[...], ydc_ref[...])
    acc = None
    for nbr, y in enumerate((ya_ref[...], yb_ref[...], yc_ref[...], yd)):
        gate = _sigmoid(mg_ref[:, nbr * d:(nbr + 1) * d].astype(F32))
        term = gate * _dot(y, wb_ref[nbr])
        acc = term if acc is None else acc + term
    xn = x_ref[...] + g1_ref[0] * _dot(acc.astype(BF16), wo_ref[...])
    xo_ref[...] = xn
    h = xn * lax.rsqrt(jnp.mean(xn * xn, axis=-1, keepdims=True) + EPS) * n2_ref[...]
    h = h * (1.0 + sc_ref[0]) + sh_ref[0]
    hh = h.astype(BF16)
    hl = (h - hh.astype(F32)).astype(BF16)
    rwh = rwh_ref[...]
    lg = _dot(hh, rwh) + _dot(hl, rwh) + _dot(hh, rwl_ref[...]) + rb_ref[...]
    rw, gi = _route_weights(lg)
    h_ref[:, :d] = h
    h_ref[:, d:] = rw

    lane = lax.broadcasted_iota(jnp.int32, (tm, 128), 1)
    onehot = jnp.where(lane == gi, 1.0, 0.0)
    r = lax.broadcasted_iota(jnp.int32, (tm, tm), 0)
    c = lax.broadcasted_iota(jnp.int32, (tm, tm), 1)
    before = jnp.where(c < r, 1.0, 0.0).astype(BF16)
    run = run_ref[0:1, :]
    rank = jnp.sum(onehot * (_dot(before, onehot.astype(BF16)) + run), axis=-1, keepdims=True)
    plan_ref[...] = jnp.where(lane == 0, rank, jnp.where(lane == 1, gi.astype(F32), 0.0))
    run = run + jnp.sum(onehot, axis=0, keepdims=True)
    run_ref[0:1, :] = run
    cnt_ref[...] = jnp.broadcast_to(run, cnt_ref.shape)


def _merge(x_all, ya, yb, yc, yd_lat, yd_ctx, pm, mod3, prm, dims, n_tiles):
    d = x_all.shape[1]
    tm = dims["tm_merge"]
    rows = n_tiles * tm
    nlat_tiles = dims["B"] * dims["L"] // tm
    modrow = dims["modrow"]
    row = lambda shape: pl.BlockSpec(shape, lambda i: (i, 0))
    full = lambda shape: pl.BlockSpec(shape, lambda i: (0,) * len(shape))
    modspec = lambda comp: pl.BlockSpec((1, 1, d), lambda i: (modrow(i, tm) * 6 + comp, 0, 0))
    return pl.pallas_call(
        functools.partial(_merge_kernel, nlat_tiles=nlat_tiles),
        grid=(n_tiles,),
        in_specs=[
            row((tm, d)), row((tm, BRANCH_W)), row((tm, BRANCH_W)), row((tm, BRANCH_W)),
            pl.BlockSpec((tm, BRANCH_W), lambda i: (jnp.minimum(i, nlat_tiles - 1), 0)),
            pl.BlockSpec((tm, BRANCH_W), lambda i: (jnp.maximum(i - nlat_tiles, 0), 0)),
            pl.BlockSpec((tm, 4 * d), lambda i: (i, C_MG // (4 * d))),
            modspec(2), modspec(3), modspec(4), full((1, d)),
            full((4, BRANCH_W, d)), full((d, d)), full((d, 128)), full((d, 128)), full((1, 128)),
        ],
        out_specs=[row((tm, d)), row((tm, d + 128)), row((tm, 128)), full((8, 128))],
        out_shape=[jax.ShapeDtypeStruct((rows, d), F32), jax.ShapeDtypeStruct((rows, d + 128), F32),
                   jax.ShapeDtypeStruct((rows, 128), F32), jax.ShapeDtypeStruct((8, 128), F32)],
        scratch_shapes=[pltpu.VMEM((8, 128), F32)],
        compiler_params=_cparams(("arbitrary",)),
        name="merge",
    )(x_all, ya, yb, yc, yd_lat, yd_ctx, pm, mod3, mod3, mod3, prm["norm2"], prm["w_branch"], prm["w_out"],
      prm["rw_hi"], prm["rw_lo"], prm["rb"])


DMA_UNROLL = 8


def _row_copies(n, make):
    def issue(blk, carry):
        for u in range(DMA_UNROLL):
            make(blk * DMA_UNROLL + u).start()
        return carry
    lax.fori_loop(0, n // DMA_UNROLL, issue, 0)

    def drain(blk, carry):
        for u in range(DMA_UNROLL):
            make(0).wait()
        return carry
    lax.fori_loop(0, n // DMA_UNROLL, drain, 0)


def _scatter_kernel(pos_ref, src_ref, init_ref, dst_ref, sem, *, tm):
    del init_ref
    base = pl.program_id(0) * tm

    def make(r):
        return pltpu.make_async_copy(src_ref.at[pl.ds(base + r, 1)], dst_ref.at[pl.ds(pos_ref[r], 1)], sem)
    _row_copies(tm, make)


def _moe_scatter(pos, hext, n_rows, sorted_rows, tm):
    w = hext.shape[1]
    return pl.pallas_call(
        functools.partial(_scatter_kernel, tm=tm),
        grid=(n_rows // tm,),
        in_specs=[pl.BlockSpec((tm,), lambda i: (i,), memory_space=pltpu.SMEM),
                  pl.BlockSpec(memory_space=pl.ANY), pl.BlockSpec(memory_space=pl.ANY)],
        out_specs=pl.BlockSpec(memory_space=pl.ANY),
        out_shape=jax.ShapeDtypeStruct((sorted_rows, w), F32),
        scratch_shapes=[pltpu.SemaphoreType.DMA(())],
        input_output_aliases={2: 0},
        compiler_params=_cparams(("arbitrary",)),
        name="moe_scatter",
    )(pos, hext, jnp.zeros((sorted_rows, w), F32))


def _moe_ffn_kernel(tg_ref, h_ref, w1_ref, w3_ref, w2_ref, o_ref, hb_ref):
    i = pl.program_id(0)
    j = pl.program_id(1)
    d = o_ref.shape[1]

    @pl.when(j == 0)
    def _():
        hb_ref[...] = h_ref[:, :d].astype(BF16)
        o_ref[...] = jnp.zeros_like(o_ref)

    h = hb_ref[...]
    a = _silu(_dot(h, w1_ref[0].astype(BF16))) * _dot(h, w3_ref[0].astype(BF16))
    y = _dot(a.astype(BF16), w2_ref[0].astype(BF16))
    rw = h_ref[:, d:]
    lane = lax.broadcasted_iota(jnp.int32, rw.shape, 1)
    e = tg_ref[i] * MOE_PER_GROUP + j
    we = jnp.sum(jnp.where(lane == e + MOE_GROUPS, rw, 0.0), axis=-1, keepdims=True)
    o_ref[...] += we * y


def _moe_ffn(tile_group, xs, prm, tm):
    rows, w = xs.shape
    d = w - 128
    expert = lambda i, j, tg: (tg[i] * MOE_PER_GROUP + j, 0, 0)
    return pl.pallas_call(
        _moe_ffn_kernel,
        grid_spec=pltpu.PrefetchScalarGridSpec(
            num_scalar_prefetch=1,
            grid=(rows // tm, MOE_PER_GROUP),
            in_specs=[
                pl.BlockSpec((tm, w), lambda i, j, tg: (i, 0)),
                pl.BlockSpec((1, d, MOE_FF), expert),
                pl.BlockSpec((1, d, MOE_FF), expert),
                pl.BlockSpec((1, MOE_FF, d), expert),
            ],
            out_specs=pl.BlockSpec((tm, d), lambda i, j, tg: (i, 0)),
            scratch_shapes=[pltpu.VMEM((tm, d), BF16)],
        ),
        out_shape=jax.ShapeDtypeStruct((rows, d), F32),
        compiler_params=_cparams(("arbitrary", "arbitrary")),
        name="moe_ffn",
    )(tile_group, xs, prm["exp_w1"], prm["exp_w3"], prm["exp_w2"])


def _combine_kernel(pos_ref, x_ref, g2_ref, ys_ref, o_ref, buf_ref, sem):
    tm = x_ref.shape[0]

    def make(r):
        return pltpu.make_async_copy(ys_ref.at[pl.ds(pos_ref[r], 1)], buf_ref.at[pl.ds(r, 1)], sem)
    _row_copies(tm, make)
    o_ref[...] = x_ref[...] + g2_ref[0] * buf_ref[...]


def _moe_combine(pos, x_mid, ys, mod3, dims, n_rows, tm):
    d = x_mid.shape[1]
    modrow = dims["modrow"]
    return pl.pallas_call(
        _combine_kernel,
        grid=(n_rows // tm,),
        in_specs=[pl.BlockSpec((tm,), lambda i: (i,), memory_space=pltpu.SMEM),
                  pl.BlockSpec((tm, d), lambda i: (i, 0)),
                  pl.BlockSpec((1, 1, d), lambda i: (modrow(i, tm) * 6 + 5, 0, 0)),
                  pl.BlockSpec(memory_space=pl.ANY)],
        out_specs=pl.BlockSpec((tm, d), lambda i: (i, 0)),
        out_shape=jax.ShapeDtypeStruct((n_rows, d), F32),
        scratch_shapes=[pltpu.VMEM((tm, d), F32), pltpu.SemaphoreType.DMA(())],
        compiler_params=_cparams(("arbitrary",)),
        name="moe_combine",
    )(pos, x_mid, mod3, ys)


def _moe(x_mid, hext, plan, cnt, mod3, prm, dims, n_rows):
    tm = dims["tm_moe"]
    counts = cnt[0, :MOE_GROUPS].astype(jnp.int32)
    ends = jnp.cumsum((counts + tm - 1) // tm * tm)
    starts = ends - (counts + tm - 1) // tm * tm
    gid = plan[:, 1].astype(jnp.int32)
    pos = plan[:, 0].astype(jnp.int32)
    for g in range(MOE_GROUPS):
        pos = pos + jnp.where(gid == g, starts[g], 0)
    n_tiles = n_rows // tm + MOE_GROUPS
    tile_group = jnp.minimum(jnp.sum(jnp.arange(n_tiles)[:, None] * tm >= ends[None, :], axis=1),
                             MOE_GROUPS - 1).astype(jnp.int32)
    xs = _moe_scatter(pos, hext, n_rows, n_tiles * tm, tm)
    ys = _moe_ffn(tile_group, xs, prm, tm)
    return _moe_combine(pos, x_mid, ys, mod3, dims, n_rows, tm)


def _hi_lo(w):
    hi = w.astype(BF16)
    return hi, (w - hi.astype(F32)).astype(BF16)


def _layer_params(l, p):
    d = p["w_in"].shape[1]
    w_in = p["w_in"][l]
    o = 0
    cols = {}
    for name, size in (("z", 512), ("xbc", 1024), ("dt", 16), ("gq", 512), ("gk", 512), ("gv", 512),
                       ("g1", 32), ("gr", 512), ("lx", 512), ("lg", 512), ("aq", 512), ("ak", 256),
                       ("av", 256), ("mg", 4 * d)):
        cols[name] = w_in[:, o:o + size]
        o += size
    out = {}
    out["w_main"] = jnp.concatenate([cols[n] for n in ("mg", "xbc", "z", "gq", "gk", "gv", "gr", "lx", "lg",
                                                        "aq", "ak", "av")], axis=1).astype(BF16)
    zpad = jnp.zeros((d, 128 - 8 - GLA_RANK), F32)
    w_small = jnp.concatenate([cols["dt"][:, :8], cols["g1"][:, :GLA_RANK], zpad,
                               cols["dt"][:, 8:], cols["g1"][:, GLA_RANK:], zpad], axis=1)
    out["ws_hi"], out["ws_lo"] = _hi_lo(w_small)
    out["wst_hi"], out["wst_lo"] = _hi_lo(w_small.T)
    out["norm1"] = p["norm1"][l][None]
    out["norm2"] = p["norm2"][l][None]

    out["ssd_cw"] = p["ssd_conv_w"][l]
    out["ssd_cb"] = p["ssd_conv_b"][l][None]
    pad8 = lambda v: jnp.pad(v, ((0, 0), (0, 128 - SSD_HEADS)))
    brow = pad8(p["ssd_dt_bias"][l])
    arow = pad8(-jnp.exp(p["ssd_a_log"][l]))
    out["ssd_brow"], out["ssd_arow"] = brow[:, None, :], arow[:, None, :]
    out["ssd_bcol"], out["ssd_acol"] = brow[:, :, None], arow[:, :, None]
    head_of_lane = jnp.arange(SSD_INNER) // SSD_HEAD_DIM
    out["ssd_e"] = (jnp.arange(128)[:, None] == head_of_lane[None, :]).astype(BF16)
    out["ssd_dexp"] = jnp.repeat(p["ssd_d"][l], SSD_HEAD_DIM)[None]
    out["ssd_norm"] = p["ssd_norm"][l][None]

    g2 = jnp.zeros((2, 128, GLA_HEADS * GLA_DK), F32).at[:, SM_G1:SM_G1 + GLA_RANK].set(p["gla_g2"][l])
    out["gla_g2h"], out["gla_g2l"] = _hi_lo(g2)
    out["gla_gb"] = p["gla_gb"][l][:, None, :]
    out["gla_norm"] = p["gla_norm"][l][None]

    def pairs(w):
        w = w.reshape(2, LRU_BLOCKS // 2, 2, LRU_BLOCK, LRU_BLOCK)
        z = jnp.zeros_like(w[:, :, 0])
        top = jnp.concatenate([w[:, :, 0], z], axis=-1)
        bot = jnp.concatenate([z, w[:, :, 1]], axis=-1)
        return jnp.concatenate([top, bot], axis=-2)
    out["lru_wg"] = jnp.concatenate([pairs(p["lru_wa"][l]), pairs(p["lru_wx"][l])], axis=-1).astype(BF16)
    out["lru_cw"] = p["lru_conv_w"][l]
    out["lru_cb"] = p["lru_conv_b"][l][None]
    out["lru_ba"] = p["lru_ba"][l][:, None, :]
    out["lru_bx"] = p["lru_bx"][l][:, None, :]
    out["lru_lam"] = p["lru_lambda"][l][:, None, :]

    out["att_qnorm"] = p["att_qnorm"][l][None]
    out["att_knorm"] = p["att_knorm"][l][None]
    out["w_branch"] = p["w_branch"][l].astype(BF16)
    out["w_out"] = p["w_out"][l].astype(BF16)
    rw = jnp.concatenate([p["router_wg"][l], p["router_we"][l],
                          jnp.zeros((d, 128 - MOE_GROUPS - MOE_EXPERTS), F32)], axis=1)
    out["rw_hi"], out["rw_lo"] = _hi_lo(rw)
    out["rb"] = jnp.concatenate([p["router_bg"][l], p["router_be"][l],
                                 jnp.zeros((128 - MOE_GROUPS - MOE_EXPERTS,), F32)])[None]
    out["exp_w1"] = p["exp_w1"][l]
    out["exp_w3"] = p["exp_w3"][l]
    out["exp_w2"] = p["exp_w2"][l]
    return out


def _rope_tables(l):
    f = ATT_HEAD_DIM // 4
    inv = ROPE_THETA ** (-jnp.arange(f, dtype=F32) / f)
    tpos = jnp.arange(l, dtype=jnp.int32)
    row = (tpos // GRID_W).astype(F32)[:, None] * inv
    col = (tpos % GRID_W).astype(F32)[:, None] * inv
    cos = jnp.concatenate([jnp.cos(row), jnp.cos(row), jnp.cos(col), jnp.cos(col)], axis=1)
    sin = jnp.concatenate([-jnp.sin(row), jnp.sin(row), -jnp.sin(col), jnp.sin(col)], axis=1)
    return cos, sin


def _pick(n, cands):
    for c in cands:
        if n % c == 0:
            return c
    raise ValueError(f"no tile size for {n}")


def kernel(x, c, ctx, c_ctx, ada_w, ada_b, norm1, norm2, w_in, ssd_conv_w, ssd_conv_b, ssd_dt_bias, ssd_a_log, ssd_d, ssd_norm, gla_g2, gla_gb, gla_norm, lru_conv_w, lru_conv_b, lru_wa, lru_ba, lru_wx, lru_bx, lru_lambda, att_qnorm, att_knorm, w_branch, w_out, router_wg, router_bg, router_we, router_be, exp_w1, exp_w3, exp_w2):
    nb, l, d = x.shape
    c_len = ctx.shape[1]
    depth = ada_w.shape[0]
    tl, tc = nb * l, nb * c_len
    assert l % LRU_R == 0 and c_len % LRU_R == 0 and nb + 1 <= 8 and l % GRID_W == 0
    params = dict(norm1=norm1, norm2=norm2, w_in=w_in, ssd_conv_w=ssd_conv_w, ssd_conv_b=ssd_conv_b,
                  ssd_dt_bias=ssd_dt_bias, ssd_a_log=ssd_a_log, ssd_d=ssd_d, ssd_norm=ssd_norm,
                  gla_g2=gla_g2, gla_gb=gla_gb, gla_norm=gla_norm, lru_conv_w=lru_conv_w,
                  lru_conv_b=lru_conv_b, lru_wa=lru_wa, lru_ba=lru_ba, lru_wx=lru_wx, lru_bx=lru_bx,
                  lru_lambda=lru_lambda, att_qnorm=att_qnorm, att_knorm=att_knorm, w_branch=w_branch,
                  w_out=w_out, router_wg=router_wg, router_bg=router_bg, router_we=router_we,
                  router_be=router_be, exp_w1=exp_w1, exp_w3=exp_w3, exp_w2=exp_w2)

    tile = _pick(math.gcd(l, tc), (1024, 512, 256))

    def modrow(i, tm):
        return jnp.where(i < tl // tm, i // (l // tm), nb)

    dims = dict(B=nb, L=l, C=c_len, modrow=modrow, tm_proj=tile, tm_merge=min(tile, 512), tm_moe=tile,
                tm_qk=min(tile, 512), tq=_pick(l, (512, 256)))

    cvec = jnp.zeros((8, d), F32).at[:nb].set(c).at[nb].set(c_ctx)
    mod_all = _modulation(cvec, ada_w, ada_b)
    cos, sin = _rope_tables(l)
    x_all = jnp.concatenate([x.reshape(tl, d), ctx.reshape(tc, d)], axis=0)

    for layer in range(depth):
        last = layer == depth - 1
        prm = _layer_params(layer, params)
        mod3 = mod_all[layer].reshape(8 * 6, 1, d)
        pm, sm, smt = _in_projection(x_all, prm["norm1"], mod3, prm["w_main"], prm["ws_hi"], prm["ws_lo"],
                                     prm["wst_hi"], prm["wst_lo"], dims)
        ya = _ssd_mixer(pm, sm, smt, prm, dims)
        yb = _gla_mixer(pm, sm, prm, dims)
        yc = _lru_mixer(pm, prm, dims)
        qn, kn = _qk_prep(pm, cos, sin, prm, dims)
        yd = _attention(qn, kn, pm, dims, latent=True)
        yd_ctx = yd if last else _attention(qn, kn, pm, dims, latent=False)
        n_rows = tl if last else tl + tc
        x_mid, hext, plan, cnt = _merge(x_all, ya, yb, yc, yd, yd_ctx, pm, mod3, prm, dims,
                                        n_rows // dims["tm_merge"])
        x_all = _moe(x_mid, hext, plan, cnt, mod3, prm, dims, n_rows)
    return x_all.reshape(nb, l, d)
```

```python
import functools
import math

import jax
import jax.numpy as jnp
from jax import lax
from jax.experimental import pallas as pl
from jax.experimental.pallas import tpu as pltpu

F32 = jnp.float32
BF16 = jnp.bfloat16

EPS = 1e-6
GRID_W = 64
ROPE_THETA = 10000.0

SSD_HEADS = 8
SSD_HEAD_DIM = 64
SSD_INNER = 512
SSD_GROUPS = 2
SSD_STATE = 128
SSD_XBC = 1024
CONV_W = 4
GLA_HEADS = 4
GLA_DK = 128
GLA_RANK = 16
GLA_GATE_NORM = 16.0
GLA_CHUNK = 64
LRU_W = 512
LRU_BLOCKS = 8
LRU_BLOCK = 64
LRU_C = 8.0
ATT_HEADS = 4
ATT_KV_HEADS = 2
ATT_HEAD_DIM = 128
MOE_GROUPS = 4
MOE_PER_GROUP = 4
MOE_EXPERTS = 16
MOE_FF = 512
BRANCH_W = 512

C_MG, C_XBC, C_Z, C_GQ, C_GK, C_GV, C_GR = 0, 4096, 5120, 5632, 6144, 6656, 7168
C_LX, C_LG, C_AQ, C_AK, C_AV, N_MAIN = 7680, 8192, 8704, 9216, 9472, 9728
N_SMALL = 256
SM_DT, SM_G1 = 0, 8

HALO = 16
SSD_Q = 128
GLA_R = 256
LRU_R = 256
LRU_SEG = LRU_R // 8

VMEM_LIMIT = 56 * 1024 * 1024

_NT = (((1,), (1,)), ((), ()))
_TN = (((0,), (0,)), ((), ()))


def _dot(a, b):
    return jnp.dot(a, b, preferred_element_type=F32)


def _dot_nt(a, b):
    return lax.dot_general(a, b, _NT, preferred_element_type=F32)


def _dot_tn(a, b):
    return lax.dot_general(a, b, _TN, preferred_element_type=F32)


def _split3(x):
    h = x.astype(BF16)
    r = x - h.astype(F32)
    m = r.astype(BF16)
    l = (r - m.astype(F32)).astype(BF16)
    return h, m, l


def _dot_mask_lhs(mask_bf16, x):
    h, m, l = _split3(x)
    return _dot(mask_bf16, h) + _dot(mask_bf16, m) + _dot(mask_bf16, l)


def _dot_mask_rhs(x, mask_bf16):
    h, m, l = _split3(x)
    return _dot(h, mask_bf16) + _dot(m, mask_bf16) + _dot(l, mask_bf16)


def _sigmoid(x):
    return 0.5 * jnp.tanh(0.5 * x) + 0.5


def _silu(x):
    return x * _sigmoid(x)


def _softplus(x):
    return jnp.maximum(x, 0.0) + jnp.log1p(jnp.exp(-jnp.abs(x)))


def _cparams(sem):
    return pltpu.CompilerParams(dimension_semantics=sem, vmem_limit_bytes=VMEM_LIMIT)


def _mod_kernel(c_ref, w_ref, b_ref, o_ref):
    s = _silu(c_ref[...])
    sh, sm, sl = _split3(s)
    w = w_ref[0]
    wh = w.astype(BF16)
    wl = (w - wh.astype(F32)).astype(BF16)
    o_ref[0] = _dot(sh, wh) + _dot(sm, wh) + _dot(sh, wl) + b_ref[0]


def _modulation(cvec, ada_w, ada_b):
    ld, d, n6 = ada_w.shape
    tn = 1536
    return pl.pallas_call(
        _mod_kernel,
        grid=(ld, n6 // tn),
        in_specs=[
            pl.BlockSpec((8, d), lambda l, j: (0, 0)),
            pl.BlockSpec((1, d, tn), lambda l, j: (l, 0, j)),
            pl.BlockSpec((1, 1, tn), lambda l, j: (l, 0, j)),
        ],
        out_specs=pl.BlockSpec((1, 8, tn), lambda l, j: (l, 0, j)),
        out_shape=jax.ShapeDtypeStruct((ld, 8, n6), F32),
        compiler_params=_cparams(("arbitrary", "arbitrary")),
        name="modulation",
    )(cvec, ada_w, ada_b.reshape(ld, 1, n6))


def _inproj_kernel(x_ref, g_ref, sh_ref, sc_ref, w_ref, wsh_ref, wsl_ref, wsth_ref, wstl_ref,
                   p_ref, s_ref, st_ref, h_ref):
    j = pl.program_id(1)

    @pl.when(j == 0)
    def _():
        x = x_ref[...]
        h = x * lax.rsqrt(jnp.mean(x * x, axis=-1, keepdims=True) + EPS) * g_ref[...]
        h = h * (1.0 + sc_ref[0]) + sh_ref[0]
        hh = h.astype(BF16)
        hl = (h - hh.astype(F32)).astype(BF16)
        h_ref[...] = hh
        wsh = wsh_ref[...]
        s_ref[...] = _dot(hh, wsh) + _dot(hl, wsh) + _dot(hh, wsl_ref[...])
        wsth = wsth_ref[...]
        st_ref[...] = _dot_nt(wsth, hh) + _dot_nt(wsth, hl) + _dot_nt(wstl_ref[...], hh)

    p_ref[...] = _dot(h_ref[...], w_ref[...]).astype(BF16)


def _in_projection(x_all, norm_g, mod3, w_main, ws_hi, ws_lo, wst_hi, wst_lo, dims):
    t, d = x_all.shape
    tm, tn = dims["tm_proj"], N_MAIN // 4
    modrow = dims["modrow"]
    return pl.pallas_call(
        _inproj_kernel,
        grid=(t // tm, N_MAIN // tn),
        in_specs=[
            pl.BlockSpec((tm, d), lambda i, j: (i, 0)),
            pl.BlockSpec((1, d), lambda i, j: (0, 0)),
            pl.BlockSpec((1, 1, d), lambda i, j: (modrow(i, tm) * 6 + 0, 0, 0)),
            pl.BlockSpec((1, 1, d), lambda i, j: (modrow(i, tm) * 6 + 1, 0, 0)),
            pl.BlockSpec((d, tn), lambda i, j: (0, j)),
            pl.BlockSpec((d, N_SMALL), lambda i, j: (0, 0)),
            pl.BlockSpec((d, N_SMALL), lambda i, j: (0, 0)),
            pl.BlockSpec((N_SMALL, d), lambda i, j: (0, 0)),
            pl.BlockSpec((N_SMALL, d), lambda i, j: (0, 0)),
        ],
        out_specs=[
            pl.BlockSpec((tm, tn), lambda i, j: (i, j)),
            pl.BlockSpec((tm, N_SMALL), lambda i, j: (i, 0)),
            pl.BlockSpec((N_SMALL, tm), lambda i, j: (0, i)),
        ],
        out_shape=[
            jax.ShapeDtypeStruct((t, N_MAIN), BF16),
            jax.ShapeDtypeStruct((t, N_SMALL), F32),
            jax.ShapeDtypeStruct((N_SMALL, t), F32),
        ],
        scratch_shapes=[pltpu.VMEM((tm, d), BF16)],
        compiler_params=_cparams(("arbitrary", "arbitrary")),
        name="in_projection",
    )(x_all, norm_g, mod3, mod3, w_main, ws_hi, ws_lo, wst_hi, wst_lo)


def _seq_pos(p, i, nctx, nlat):
    is_ctx = i < nctx
    k = jnp.where(is_ctx, jnp.where(p == 0, i, nctx - 1 - i),
                  jnp.where(p == 0, i - nctx, nlat - 1 - (i - nctx)))
    return k, jnp.where(is_ctx, nctx, nlat)


def _seq_chunk(b, p, i, nctx, nlat, nb):
    k, _ = _seq_pos(p, i, nctx, nlat)
    return jnp.where(i < nctx, nb * nlat + b * nctx + k, b * nlat + k)


def _seq_out_chunk(b, p, i, nctx, nlat, nb):
    return _seq_chunk(b, 1, jnp.where(p == 0, 0, i), nctx, nlat, nb)


def _fwd_slot(i, nctx, nlat):
    return jnp.where(i < nctx, nctx - 1 - i, nctx + nlat - 1 - (i - nctx))


def _conv4(prev_ref, cur_ref, next_ref, w, b, pv, nv, rows):
    xe = jnp.concatenate([prev_ref[...].astype(F32) * pv, cur_ref[...].astype(F32),
                          next_ref[...].astype(F32) * nv], axis=0)
    n = rows + 2 * HALO
    y = (pltpu.roll(xe, 2, 0) * w[0:1] + pltpu.roll(xe, 1, 0) * w[1:2] + xe * w[2:3]
         + pltpu.roll(xe, n - 1, 0) * w[3:4])
    return y[HALO:HALO + rows] + b


def _dir_mask(p, n):
    r = lax.broadcasted_iota(jnp.int32, (n, n), 0)
    c = lax.broadcasted_iota(jnp.int32, (n, n), 1)
    sgn = 1 - 2 * p
    return (r - c) * sgn >= 0


def _ssd_kernel(xc_ref, xp_ref, xn_ref, z_ref, sm_ref, smt_ref, cw_ref, cb_ref, brow_ref, arow_ref,
                bcol_ref, acol_ref, e_ref, dexp_ref, nrm_ref, o_ref, yf_ref, ht_ref, xbc_ref, *, nctx, nlat):
    q = SSD_Q
    p = pl.program_id(1)
    i = pl.program_id(2)
    k, n = _seq_pos(p, i, nctx, nlat)
    pv = (k > 0).astype(F32)
    nv = (k < n - 1).astype(F32)

    @pl.when(i == 0)
    def _():
        ht_ref[...] = jnp.zeros_like(ht_ref)

    @pl.when(p == 0)
    def _():
        xbc_ref[i] = _silu(_conv4(xp_ref, xc_ref, xn_ref, cw_ref[...], cb_ref[...], pv, nv, q)).astype(BF16)

    xbc = xbc_ref[jnp.where(p == 0, i, _fwd_slot(i, nctx, nlat))]
    xs = xbc[:, :SSD_INNER].astype(F32)
    bmat = xbc[:, SSD_INNER:SSD_INNER + 256]
    cmat = xbc[:, SSD_INNER + 256:]

    mask = _dir_mask(p, q)
    mask_t = _dir_mask(1 - p, q)
    tri = jnp.where(mask, 1.0, 0.0).astype(BF16)
    tri_t = jnp.where(mask_t, 1.0, 0.0).astype(BF16)

    dt_c = _softplus(sm_ref[...] + brow_ref[0])
    dta_c = dt_c * arow_ref[0]
    acum_c = _dot_mask_lhs(tri, dta_c)
    alast_c = jnp.sum(dta_c, axis=0, keepdims=True)
    dt_r = _softplus(smt_ref[...] + bcol_ref[0])
    acum_r = _dot_mask_rhs(dt_r * acol_ref[0], tri_t)

    e = e_ref[...]
    stack = jnp.concatenate([dt_c, jnp.exp(acum_c), dt_c * jnp.exp(alast_c - acum_c)], axis=0)
    ex = _dot(stack.astype(BF16), e)
    dt_e, ea_e, ds_e = ex[:q], ex[q:2 * q], ex[2 * q:]
    dec_e = _dot_mask_rhs(jnp.broadcast_to(jnp.exp(alast_c), (8, 128)), e)[0:1]

    dtx = (xs * dt_e).astype(BF16)
    dsx = (xs * ds_e).astype(BF16)
    lane = lax.broadcasted_iota(jnp.int32, (q, 128), 1)
    ydiag, yoff = [], []
    for g in range(SSD_GROUPS):
        bg = bmat[:, g * 128:(g + 1) * 128]
        cg = cmat[:, g * 128:(g + 1) * 128]
        cb = _dot_nt(cg, bg)
        htg = ht_ref[g]
        yoff.append(_dot(cg, htg.astype(BF16)))
        ht_ref[g] = htg * dec_e[:, g * 256:(g + 1) * 256] + _dot_tn(bg, dsx[:, g * 256:(g + 1) * 256])
        for pr in range(2):
            hd = g * 4 + pr * 2
            pair = dtx[:, hd * 64:hd * 64 + 128]
            outs = []
            for hh in (hd, hd + 1):
                seg = acum_c[:, hh:hh + 1] - acum_r[hh:hh + 1, :]
                lm = jnp.exp(jnp.where(mask, seg, -1e30))
                outs.append(_dot((cb * lm).astype(BF16), pair))
            ydiag.append(jnp.where(lane < 64, outs[0], outs[1]))
    y = jnp.concatenate(ydiag, axis=1) + jnp.concatenate(yoff, axis=1) * ea_e

    @pl.when(p == 0)
    def _():
        yf_ref[i] = y

    @pl.when(p == 1)
    def _():
        yt = (y + yf_ref[_fwd_slot(i, nctx, nlat)] + dexp_ref[...] * xs) * _silu(z_ref[...].astype(F32))
        yn = yt * lax.rsqrt(jnp.mean(yt * yt, axis=-1, keepdims=True) + EPS) * nrm_ref[...]
        o_ref[...] = yn.astype(BF16)


def _ssd_mixer(pm, sm, smt, prm, dims):
    t = pm.shape[0]
    nb, nctx, nlat = dims["B"], dims["C"] // SSD_Q, dims["L"] // SSD_Q
    nhalo = t // HALO
    per = SSD_Q // HALO
    ck = functools.partial(_seq_chunk, nctx=nctx, nlat=nlat, nb=nb)
    oc = functools.partial(_seq_out_chunk, nctx=nctx, nlat=nlat, nb=nb)
    full = lambda shape: pl.BlockSpec(shape, lambda b, p, i: (0,) * len(shape))
    bydir = lambda shape: pl.BlockSpec((1,) + shape, lambda b, p, i: (p,) + (0,) * len(shape))
    return pl.pallas_call(
        functools.partial(_ssd_kernel, nctx=nctx, nlat=nlat),
        grid=(nb, 2, nctx + nlat),
        in_specs=[
            pl.BlockSpec((SSD_Q, SSD_XBC), lambda b, p, i: (ck(b, p, i), C_XBC // SSD_XBC)),
            pl.BlockSpec((HALO, SSD_XBC),
                         lambda b, p, i: (jnp.maximum(ck(b, p, i) * per - 1, 0), C_XBC // SSD_XBC)),
            pl.BlockSpec((HALO, SSD_XBC),
                         lambda b, p, i: (jnp.minimum(ck(b, p, i) * per + per, nhalo - 1), C_XBC // SSD_XBC)),
            pl.BlockSpec((SSD_Q, SSD_INNER), lambda b, p, i: (ck(b, p, i), C_Z // SSD_INNER)),
            pl.BlockSpec((SSD_Q, 128), lambda b, p, i: (ck(b, p, i), p)),
            pl.BlockSpec((128, SSD_Q), lambda b, p, i: (p, ck(b, p, i))),
            full((CONV_W, SSD_XBC)), full((1, SSD_XBC)),
            bydir((1, 128)), bydir((1, 128)), bydir((128, 1)), bydir((128, 1)),
            full((128, SSD_INNER)), full((1, SSD_INNER)), full((1, SSD_INNER)),
        ],
        out_specs=pl.BlockSpec((SSD_Q, SSD_INNER), lambda b, p, i: (oc(b, p, i), 0)),
        out_shape=jax.ShapeDtypeStruct((t, SSD_INNER), BF16),
        scratch_shapes=[pltpu.VMEM((nctx + nlat, SSD_Q, SSD_INNER), F32),
                        pltpu.VMEM((SSD_GROUPS, SSD_STATE, 256), F32),
                        pltpu.VMEM((nctx + nlat, SSD_Q, SSD_XBC), BF16)],
        compiler_params=_cparams(("arbitrary", "arbitrary", "arbitrary")),
        name="ssd_mixer",
    )(pm, pm, pm, pm, sm, smt, prm["ssd_cw"], prm["ssd_cb"], prm["ssd_brow"], prm["ssd_arow"],
      prm["ssd_bcol"], prm["ssd_acol"], prm["ssd_e"], prm["ssd_dexp"], prm["ssd_norm"])


def _gla_kernel(q_ref, k_ref, v_ref, r_ref, sm_ref, g2h_ref, g2l_ref, gb_ref, nrm_ref,
                o_ref, yf_ref, st_ref, *, nctx, nlat):
    qc = GLA_CHUNK
    nsub = GLA_R // qc
    p = pl.program_id(1)
    i = pl.program_id(2)

    @pl.when(i == 0)
    def _():
        st_ref[...] = jnp.zeros_like(st_ref)

    r = GLA_R
    rr = lax.broadcasted_iota(jnp.int32, (r, r), 0)
    cc = lax.broadcasted_iota(jnp.int32, (r, r), 1)
    shift = qc.bit_length() - 1
    mask = (lax.shift_right_logical(rr, shift) == lax.shift_right_logical(cc, shift)) & ((rr - cc) * (1 - 2 * p) >= 0)
    tri = jnp.where(mask, 1.0, 0.0).astype(BF16)
    scale = GLA_DK ** -0.5

    sm = sm_ref[...]
    smh = sm.astype(BF16)
    sml = (sm - smh.astype(F32)).astype(BF16)
    g2h = g2h_ref[0]
    logit = _dot(smh, g2h) + _dot(sml, g2h) + _dot(smh, g2l_ref[0]) + gb_ref[0]
    g = -_softplus(-logit) * (1.0 / GLA_GATE_NORM)
    gc = _dot_mask_lhs(tri, g)
    glast = [jnp.sum(g[s * qc:(s + 1) * qc], axis=0, keepdims=True) for s in range(nsub)]
    bcast = lambda rows_: jnp.concatenate([jnp.broadcast_to(x, (qc, x.shape[1])) for x in rows_], axis=0)
    gref = bcast([gc[s * qc + qc // 2:s * qc + qc // 2 + 1] for s in range(nsub)])
    glast_f = bcast(glast)
    qf = q_ref[...].astype(F32) * scale
    kf = k_ref[...].astype(F32)
    vb = v_ref[...]
    qe = (qf * jnp.exp(gc - gref)).astype(BF16)
    ke = (kf * jnp.exp(gref - gc)).astype(BF16)
    qg = (qf * jnp.exp(gc)).astype(BF16)
    k2 = (kf * jnp.exp(glast_f - gc)).astype(BF16)
    o_intra = []
    for h in range(GLA_HEADS):
        hs = slice(h * GLA_DK, (h + 1) * GLA_DK)
        att = jnp.where(mask, _dot_nt(qe[:, hs], ke[:, hs]), 0.0).astype(BF16)
        o_intra.append(_dot(att, vb[:, hs]))

    def sweep(order, emit):
        st = [st_ref[h] for h in range(GLA_HEADS)]
        for s in order:
            rs = slice(s * qc, (s + 1) * qc)
            dec = jnp.exp(glast[s])
            outs = []
            for h in range(GLA_HEADS):
                hs = slice(h * GLA_DK, (h + 1) * GLA_DK)
                outs.append(o_intra[h][rs] + _dot_nt(qg[rs, hs], st[h].astype(BF16)))
                st[h] = st[h] * dec[:, hs] + _dot_tn(vb[rs, hs], k2[rs, hs])
            emit(rs, jnp.concatenate(outs, axis=1))
        for h in range(GLA_HEADS):
            st_ref[h] = st[h]

    @pl.when(p == 0)
    def _():
        def emit(rs, y):
            yf_ref[i, rs, :] = y
        sweep(range(nsub), emit)

    @pl.when(p == 1)
    def _():
        slot = _fwd_slot(i, nctx, nlat)
        nrm = nrm_ref[...]

        def emit(rs, y):
            yt = y + yf_ref[slot, rs, :]
            parts = []
            for h in range(GLA_HEADS):
                hs = slice(h * GLA_DK, (h + 1) * GLA_DK)
                yh = yt[:, hs]
                parts.append(yh * lax.rsqrt(jnp.mean(yh * yh, axis=-1, keepdims=True) + EPS) * nrm[:, hs])
            o_ref[rs, :] = (jnp.concatenate(parts, axis=1) * _silu(r_ref[rs, :].astype(F32))).astype(BF16)
        sweep(range(nsub - 1, -1, -1), emit)


def _gla_mixer(pm, sm, prm, dims):
    t = pm.shape[0]
    nb, nctx, nlat = dims["B"], dims["C"] // GLA_R, dims["L"] // GLA_R
    ck = functools.partial(_seq_chunk, nctx=nctx, nlat=nlat, nb=nb)
    oc = functools.partial(_seq_out_chunk, nctx=nctx, nlat=nlat, nb=nb)
    w = GLA_HEADS * GLA_DK
    col = lambda c: pl.BlockSpec((GLA_R, w), lambda b, p, i: (ck(b, p, i), c // w))
    bydir = lambda shape: pl.BlockSpec((1,) + shape, lambda b, p, i: (p,) + (0,) * len(shape))
    return pl.pallas_call(
        functools.partial(_gla_kernel, nctx=nctx, nlat=nlat),
        grid=(nb, 2, nctx + nlat),
        in_specs=[
            col(C_GQ), col(C_GK), col(C_GV), col(C_GR),
            pl.BlockSpec((GLA_R, 128), lambda b, p, i: (ck(b, p, i), p)),
            bydir((128, w)), bydir((128, w)), bydir((1, w)),
            pl.BlockSpec((1, w), lambda b, p, i: (0, 0)),
        ],
        out_specs=pl.BlockSpec((GLA_R, w), lambda b, p, i: (oc(b, p, i), 0)),
        out_shape=jax.ShapeDtypeStruct((t, w), BF16),
        scratch_shapes=[pltpu.VMEM((nctx + nlat, GLA_R, w), F32),
                        pltpu.VMEM((GLA_HEADS, GLA_DK, GLA_DK), F32)],
        compiler_params=_cparams(("arbitrary", "arbitrary", "arbitrary")),
        name="gla_mixer",
    )(pm, pm, pm, pm, sm, prm["gla_g2h"], prm["gla_g2l"], prm["gla_gb"], prm["gla_norm"])


def _lru_kernel(xc_ref, xp_ref, xn_ref, gl_ref, cw_ref, cb_ref, wg_ref, ba_ref, bx_ref, lam_ref,
                o_ref, yf_ref, h_ref, a_scr, v_scr, as_scr, hs_scr, *, nctx, nlat):
    r, seg = LRU_R, LRU_SEG
    p = pl.program_id(1)
    i = pl.program_id(2)
    k, n = _seq_pos(p, i, nctx, nlat)
    pv = (k > 0).astype(F32)
    nv = (k < n - 1).astype(F32)

    @pl.when(i == 0)
    def _():
        h_ref[...] = jnp.zeros_like(h_ref)

    u = _conv4(xp_ref, xc_ref, xn_ref, cw_ref[...], cb_ref[...], pv, nv, r)
    ub = u.astype(BF16)
    ra, ix = [], []
    for j in range(LRU_W // 128):
        gj = _dot(ub[:, j * 128:(j + 1) * 128], wg_ref[0, j])
        ra.append(gj[:, :128])
        ix.append(gj[:, 128:])
    rg = _sigmoid(jnp.concatenate(ra, axis=1) + ba_ref[0])
    ig = _sigmoid(jnp.concatenate(ix, axis=1) + bx_ref[0])
    log_a = (-LRU_C * _softplus(-lam_ref[0])) * rg
    a_all = jnp.exp(log_a)
    th = jnp.tanh(log_a)
    v_all = u * ig * jnp.sqrt(-2.0 * th / (1.0 - th))
    nslab = LRU_W // 128
    for j in range(nslab):
        a_scr[j] = a_all[:, j * 128:(j + 1) * 128]
        v_scr[j] = v_all[:, j * 128:(j + 1) * 128]

    def scan(order, seg_order):
        for j in range(nslab):
            ls = slice(j * 128, (j + 1) * 128)
            acc_a = jnp.ones((8, 128), F32)
            acc_h = jnp.zeros((8, 128), F32)
            for kk in order:
                ak = a_scr[j, pl.ds(kk, 8, stride=seg), :]
                vk = v_scr[j, pl.ds(kk, 8, stride=seg), :]
                acc_h = ak * acc_h + vk
                acc_a = ak * acc_a
                as_scr[j, kk * 8:(kk + 1) * 8, :] = acc_a
                hs_scr[j, kk * 8:(kk + 1) * 8, :] = acc_h
            carry = h_ref[0:1, ls]
            rows = [None] * 8
            for s in seg_order:
                rows[s] = carry
                carry = acc_a[s:s + 1, :] * carry + acc_h[s:s + 1, :]
            h_ref[0:1, ls] = carry
            cin = jnp.concatenate(rows, axis=0)
            for kk in order:
                ks = slice(kk * 8, (kk + 1) * 8)
                hs_scr[j, ks, :] = hs_scr[j, ks, :] + as_scr[j, ks, :] * cin

    @pl.when(p == 0)
    def _():
        scan(range(seg), range(8))

    @pl.when(p == 1)
    def _():
        scan(range(seg - 1, -1, -1), range(7, -1, -1))

    def natural(s):
        return jnp.concatenate([hs_scr[j, pl.ds(s, seg, stride=8), :] for j in range(nslab)], axis=1)

    @pl.when(p == 0)
    def _():
        for s in range(8):
            yf_ref[i, s * seg:(s + 1) * seg, :] = natural(s)

    @pl.when(p == 1)
    def _():
        slot = _fwd_slot(i, nctx, nlat)
        c0 = math.sqrt(2.0 / math.pi)
        for s in range(8):
            rs = slice(s * seg, (s + 1) * seg)
            gt = gl_ref[rs, :].astype(F32)
            gelu = 0.5 * gt * (1.0 + jnp.tanh(c0 * (gt + 0.044715 * (gt * gt * gt))))
            o_ref[rs, :] = ((natural(s) + yf_ref[slot, rs, :]) * gelu).astype(BF16)


def _lru_mixer(pm, prm, dims):
    t = pm.shape[0]
    nb, nctx, nlat = dims["B"], dims["C"] // LRU_R, dims["L"] // LRU_R
    nhalo = t // HALO
    per = LRU_R // HALO
    ck = functools.partial(_seq_chunk, nctx=nctx, nlat=nlat, nb=nb)
    oc = functools.partial(_seq_out_chunk, nctx=nctx, nlat=nlat, nb=nb)
    w = LRU_W
    full = lambda shape: pl.BlockSpec(shape, lambda b, p, i: (0,) * len(shape))
    bydir = lambda shape: pl.BlockSpec((1,) + shape, lambda b, p, i: (p,) + (0,) * len(shape))
    return pl.pallas_call(
        functools.partial(_lru_kernel, nctx=nctx, nlat=nlat),
        grid=(nb, 2, nctx + nlat),
        in_specs=[
            pl.BlockSpec((LRU_R, w), lambda b, p, i: (ck(b, p, i), C_LX // w)),
            pl.BlockSpec((HALO, w), lambda b, p, i: (jnp.maximum(ck(b, p, i) * per - 1, 0), C_LX // w)),
            pl.BlockSpec((HALO, w), lambda b, p, i: (jnp.minimum(ck(b, p, i) * per + per, nhalo - 1), C_LX // w)),
            pl.BlockSpec((LRU_R, w), lambda b, p, i: (ck(b, p, i), C_LG // w)),
            full((CONV_W, w)), full((1, w)),
            bydir((w // 128, 128, 256)), bydir((1, w)), bydir((1, w)), bydir((1, w)),
        ],
        out_specs=pl.BlockSpec((LRU_R, w), lambda b, p, i: (oc(b, p, i), 0)),
        out_shape=jax.ShapeDtypeStruct((t, w), BF16),
        scratch_shapes=[pltpu.VMEM((nctx + nlat, LRU_R, w), F32),
                        pltpu.VMEM((8, w), F32),
                        pltpu.VMEM((w // 128, LRU_R, 128), F32), pltpu.VMEM((w // 128, LRU_R, 128), F32),
                        pltpu.VMEM((w // 128, LRU_R, 128), F32), pltpu.VMEM((w // 128, LRU_R, 128), F32)],
        compiler_params=_cparams(("arbitrary", "arbitrary", "arbitrary")),
        name="lru_mixer",
    )(pm, pm, pm, pm, prm["lru_cw"], prm["lru_cb"], prm["lru_wg"], prm["lru_ba"], prm["lru_bx"],
      prm["lru_lam"])


def _qkprep_kernel(q_ref, k_ref, cos_ref, sin_ref, qn_ref, kn_ref, qo_ref, ko_ref, *, nlat_tiles):
    i = pl.program_id(0)
    is_ctx = i >= nlat_tiles
    cos = jnp.where(is_ctx, 1.0, cos_ref[...])
    sin = jnp.where(is_ctx, 0.0, sin_ref[...])
    lane = lax.broadcasted_iota(jnp.int32, cos.shape, 1)
    first = (lane % 64) < 32

    def prep(x, g, scale):
        xn = x * lax.rsqrt(jnp.mean(x * x, axis=-1, keepdims=True) + EPS) * g
        sw = jnp.where(first, pltpu.roll(xn, 96, 1), pltpu.roll(xn, 32, 1))
        return ((xn * cos + sw * sin) * scale).astype(BF16)

    qn, kn = qn_ref[...], kn_ref[...]
    for h in range(ATT_HEADS):
        hs = slice(h * ATT_HEAD_DIM, (h + 1) * ATT_HEAD_DIM)
        qo_ref[:, hs] = prep(q_ref[:, hs].astype(F32), qn, ATT_HEAD_DIM ** -0.5)
    for h in range(ATT_KV_HEADS):
        hs = slice(h * ATT_HEAD_DIM, (h + 1) * ATT_HEAD_DIM)
        ko_ref[:, hs] = prep(k_ref[:, hs].astype(F32), kn, 1.0)


def _qk_prep(pm, cos, sin, prm, dims):
    t = pm.shape[0]
    tm = dims["tm_qk"]
    nlat_tiles = dims["B"] * dims["L"] // tm
    per_seq = dims["L"] // tm
    return pl.pallas_call(
        functools.partial(_qkprep_kernel, nlat_tiles=nlat_tiles),
        grid=(t // tm,),
        in_specs=[
            pl.BlockSpec((tm, 512), lambda i: (i, C_AQ // 512)),
            pl.BlockSpec((tm, 256), lambda i: (i, C_AK // 256)),
            pl.BlockSpec((tm, 128), lambda i: (i % per_seq, 0)),
            pl.BlockSpec((tm, 128), lambda i: (i % per_seq, 0)),
            pl.BlockSpec((1, 128), lambda i: (0, 0)),
            pl.BlockSpec((1, 128), lambda i: (0, 0)),
        ],
        out_specs=[pl.BlockSpec((tm, 512), lambda i: (i, 0)), pl.BlockSpec((tm, 256), lambda i: (i, 0))],
        out_shape=[jax.ShapeDtypeStruct((t, 512), BF16), jax.ShapeDtypeStruct((t, 256), BF16)],
        compiler_params=_cparams(("arbitrary",)),
        name="qk_prep",
    )(pm, pm, cos, sin, prm["att_qnorm"], prm["att_knorm"])


def _attn_kernel(q_ref, kc_ref, vc_ref, *rest, tk, nlat_chunks):
    if nlat_chunks:
        kl_ref, vl_ref, o_ref = rest
    else:
        (o_ref,) = rest
    tq = q_ref.shape[0]
    dh = ATT_HEAD_DIM
    q = jnp.concatenate([q_ref[:, :dh], q_ref[:, dh:]], axis=0)

    chunks = [(kc_ref, vc_ref, slice(None))]
    chunks += [(kl_ref, vl_ref, slice(c * tk, (c + 1) * tk)) for c in range(nlat_chunks)]
    m = l = acc = None
    s_next = _dot_nt(q, kc_ref[...])
    for c, (_, v_ref, rows) in enumerate(chunks):
        s = s_next
        if c + 1 < len(chunks):
            k_ref, _, nrows = chunks[c + 1]
            s_next = _dot_nt(q, k_ref[nrows, :])
        mx = jnp.max(s, axis=-1, keepdims=True)
        if m is None:
            m = mx
            pmat = jnp.exp(s - m)
            l = jnp.sum(pmat, axis=-1, keepdims=True)
            acc = _dot(pmat.astype(BF16), v_ref[rows, :])
        else:
            m_new = jnp.maximum(m, mx)
            alpha = jnp.exp(m - m_new)
            pmat = jnp.exp(s - m_new)
            l = alpha * l + jnp.sum(pmat, axis=-1, keepdims=True)
            acc = alpha * acc + _dot(pmat.astype(BF16), v_ref[rows, :])
            m = m_new
    o = acc * (1.0 / l)
    o_ref[...] = jnp.concatenate([o[:tq], o[tq:]], axis=1).astype(BF16)


def _attention(qn, kn, pm, dims, latent):
    nb, l, c = dims["B"], dims["L"], dims["C"]
    dh = ATT_HEAD_DIM
    tq = dims["tq"] if latent else c
    nq = (l if latent else c) // tq
    ctx_blk = nb * l // c
    qrow = (lambda b, i: b * nq + i) if latent else (lambda b, i: ctx_blk + b)
    av = C_AV // dh
    in_specs = [
        pl.BlockSpec((tq, 2 * dh), lambda b, g, i: (qrow(b, i), g)),
        pl.BlockSpec((c, dh), lambda b, g, i: (ctx_blk + b, g)),
        pl.BlockSpec((c, dh), lambda b, g, i: (ctx_blk + b, av + g)),
    ]
    args = [qn, kn, pm]
    if latent:
        in_specs += [pl.BlockSpec((l, dh), lambda b, g, i: (b, g)),
                     pl.BlockSpec((l, dh), lambda b, g, i: (b, av + g))]
        args += [kn, pm]
    tk = _pick(l, (1024, 512, 256))
    return pl.pallas_call(
        functools.partial(_attn_kernel, tk=tk, nlat_chunks=(l // tk if latent else 0)),
        grid=(nb, ATT_KV_HEADS, nq),
        in_specs=in_specs,
        out_specs=pl.BlockSpec((tq, 2 * dh), lambda b, g, i: (b * nq + i, g)),
        out_shape=jax.ShapeDtypeStruct((nb * (l if latent else c), ATT_HEADS * dh), BF16),
        compiler_params=_cparams(("arbitrary", "arbitrary", "arbitrary")),
        name="attention_latent" if latent else "attention_context",
    )(*args)


def _route_weights(lg):
    lane = lax.broadcasted_iota(jnp.int32, lg.shape, 1)
    neg = -1e30
    is_g = lane < MOE_GROUPS
    gl = jnp.where(is_g, lg, neg)
    gmax = jnp.max(gl, axis=-1, keepdims=True)
    gsum = jnp.sum(jnp.where(is_g, jnp.exp(gl - gmax), 0.0), axis=-1, keepdims=True)
    gi = jnp.min(jnp.where(is_g & (gl == gmax), lane, 1 << 20), axis=-1, keepdims=True)
    pg_sel = 1.0 / gsum
    lo = MOE_GROUPS + gi * MOE_PER_GROUP
    in_grp = (lane >= lo) & (lane < lo + MOE_PER_GROUP)
    el = jnp.where(in_grp, lg, neg)
    emax = jnp.max(el, axis=-1, keepdims=True)
    ex = jnp.where(in_grp, jnp.exp(el - emax), 0.0)
    pe = ex / jnp.sum(ex, axis=-1, keepdims=True)
    pe_m = jnp.where(in_grp, pe, -1.0)
    v1 = jnp.max(pe_m, axis=-1, keepdims=True)
    i1 = jnp.min(jnp.where(pe_m == v1, lane, 1 << 20), axis=-1, keepdims=True)
    pe_m2 = jnp.where(lane == i1, -1.0, pe_m)
    v2 = jnp.max(pe_m2, axis=-1, keepdims=True)
    i2 = jnp.min(jnp.where(pe_m2 == v2, lane, 1 << 20), axis=-1, keepdims=True)
    tot = v1 + v2
    w = jnp.where(lane == i1, v1 / tot, 0.0) + jnp.where(lane == i2, v2 / tot, 0.0)
    return w * pg_sel, gi


def _merge_kernel(x_ref, ya_ref, yb_ref, yc_ref, ydl_ref, ydc_ref, mg_ref, g1_ref, sh_ref, sc_ref, n2_ref,
                  wb_ref, wo_ref, rwh_ref, rwl_ref, rb_ref, xo_ref, h_ref, plan_ref, cnt_ref, run_ref,
                  *, nlat_tiles):
    i = pl.program_id(0)
    tm, d = x_ref.shape

    @pl.when(i == 0)
    def _():
        run_ref[...] = jnp.zeros_like(run_ref)

    yd = jnp.where(i < nlat_tiles, ydl_ref[...], ydc_ref[...])
    acc = None
    for nbr, y in enumerate((ya_ref[...], yb_ref[...], yc_ref[...], yd)):
        gate = _sigmoid(mg_ref[:, nbr * d:(nbr + 1) * d].astype(F32))
        term = gate * _dot(y, wb_ref[nbr])
        acc = term if acc is None else acc + term
    xn = x_ref[...] + g1_ref[0] * _dot(acc.astype(BF16), wo_ref[...])
    xo_ref[...] = xn
    h = xn * lax.rsqrt(jnp.mean(xn * xn, axis=-1, keepdims=True) + EPS) * n2_ref[...]
    h = h * (1.0 + sc_ref[0]) + sh_ref[0]
    hh = h.astype(BF16)
    hl = (h - hh.astype(F32)).astype(BF16)
    rwh = rwh_ref[...]
    lg = _dot(hh, rwh) + _dot(hl, rwh) + _dot(hh, rwl_ref[...]) + rb_ref[...]
    rw, gi = _route_weights(lg)
    h_ref[:, :d] = h
    h_ref[:, d:] = rw

    lane = lax.broadcasted_iota(jnp.int32, (tm, 128), 1)
    onehot = jnp.where(lane == gi, 1.0, 0.0)
    r = lax.broadcasted_iota(jnp.int32, (tm, tm), 0)
    c = lax.broadcasted_iota(jnp.int32, (tm, tm), 1)
    before = jnp.where(c < r, 1.0, 0.0).astype(BF16)
    run = run_ref[0:1, :]
    rank = jnp.sum(onehot * (_dot(before, onehot.astype(BF16)) + run), axis=-1, keepdims=True)
    plan_ref[...] = jnp.where(lane == 0, rank, jnp.where(lane == 1, gi.astype(F32), 0.0))
    run = run + jnp.sum(onehot, axis=0, keepdims=True)
    run_ref[0:1, :] = run
    cnt_ref[...] = jnp.broadcast_to(run, cnt_ref.shape)


def _merge(x_all, ya, yb, yc, yd_lat, yd_ctx, pm, mod3, prm, dims, n_tiles):
    d = x_all.shape[1]
    tm = dims["tm_merge"]
    rows = n_tiles * tm
    nlat_tiles = dims["B"] * dims["L"] // tm
    modrow = dims["modrow"]
    row = lambda shape: pl.BlockSpec(shape, lambda i: (i, 0))
    full = lambda shape: pl.BlockSpec(shape, lambda i: (0,) * len(shape))
    modspec = lambda comp: pl.BlockSpec((1, 1, d), lambda i: (modrow(i, tm) * 6 + comp, 0, 0))
    return pl.pallas_call(
        functools.partial(_merge_kernel, nlat_tiles=nlat_tiles),
        grid=(n_tiles,),
        in_specs=[
            row((tm, d)), row((tm, BRANCH_W)), row((tm, BRANCH_W)), row((tm, BRANCH_W)),
            pl.BlockSpec((tm, BRANCH_W), lambda i: (jnp.minimum(i, nlat_tiles - 1), 0)),
            pl.BlockSpec((tm, BRANCH_W), lambda i: (jnp.maximum(i - nlat_tiles, 0), 0)),
            pl.BlockSpec((tm, 4 * d), lambda i: (i, C_MG // (4 * d))),
            modspec(2), modspec(3), modspec(4), full((1, d)),
            full((4, BRANCH_W, d)), full((d, d)), full((d, 128)), full((d, 128)), full((1, 128)),
        ],
        out_specs=[row((tm, d)), row((tm, d + 128)), row((tm, 128)), full((8, 128))],
        out_shape=[jax.ShapeDtypeStruct((rows, d), F32), jax.ShapeDtypeStruct((rows, d + 128), F32),
                   jax.ShapeDtypeStruct((rows, 128), F32), jax.ShapeDtypeStruct((8, 128), F32)],
        scratch_shapes=[pltpu.VMEM((8, 128), F32)],
        compiler_params=_cparams(("arbitrary",)),
        name="merge",
    )(x_all, ya, yb, yc, yd_lat, yd_ctx, pm, mod3, mod3, mod3, prm["norm2"], prm["w_branch"], prm["w_out"],
      prm["rw_hi"], prm["rw_lo"], prm["rb"])


DMA_UNROLL = 8


def _row_copies(n, make):
    def issue(blk, carry):
        for u in range(DMA_UNROLL):
            make(blk * DMA_UNROLL + u).start()
        return carry
    lax.fori_loop(0, n // DMA_UNROLL, issue, 0)

    def drain(blk, carry):
        for u in range(DMA_UNROLL):
            make(0).wait()
        return carry
    lax.fori_loop(0, n // DMA_UNROLL, drain, 0)


def _scatter_kernel(pos_ref, src_ref, init_ref, dst_ref, sem, *, tm):
    del init_ref

    def make(r):
        return pltpu.make_async_copy(src_ref.at[pl.ds(r, 1)], dst_ref.at[pl.ds(pos_ref[r], 1)], sem)
    _row_copies(tm, make)


def _moe_scatter(pos, hext, n_rows, sorted_rows, tm):
    w = hext.shape[1]
    return pl.pallas_call(
        functools.partial(_scatter_kernel, tm=tm),
        grid=(n_rows // tm,),
        in_specs=[pl.BlockSpec((tm,), lambda i: (i,), memory_space=pltpu.SMEM),
                  pl.BlockSpec((tm, w), lambda i: (i, 0)), pl.BlockSpec(memory_space=pl.ANY)],
        out_specs=pl.BlockSpec(memory_space=pl.ANY),
        out_shape=jax.ShapeDtypeStruct((sorted_rows, w), F32),
        scratch_shapes=[pltpu.SemaphoreType.DMA(())],
        input_output_aliases={2: 0},
        compiler_params=_cparams(("arbitrary",)),
        name="moe_scatter",
    )(pos, hext, jnp.zeros((sorted_rows, w), F32))


def _moe_ffn_kernel(tg_ref, h_ref, w1_ref, w3_ref, w2_ref, o_ref, hb_ref):
    i = pl.program_id(0)
    j = pl.program_id(1)
    d = o_ref.shape[1]

    @pl.when(j == 0)
    def _():
        hb_ref[...] = h_ref[:, :d].astype(BF16)
        o_ref[...] = jnp.zeros_like(o_ref)

    h = hb_ref[...]
    a = _silu(_dot(h, w1_ref[0].astype(BF16))) * _dot(h, w3_ref[0].astype(BF16))
    y = _dot(a.astype(BF16), w2_ref[0].astype(BF16))
    rw = h_ref[:, d:]
    lane = lax.broadcasted_iota(jnp.int32, rw.shape, 1)
    e = tg_ref[i] * MOE_PER_GROUP + j
    we = jnp.sum(jnp.where(lane == e + MOE_GROUPS, rw, 0.0), axis=-1, keepdims=True)
    o_ref[...] += we * y


def _moe_ffn(tile_group, xs, prm, tm):
    rows, w = xs.shape
    d = w - 128
    expert = lambda i, j, tg: (tg[i] * MOE_PER_GROUP + j, 0, 0)
    return pl.pallas_call(
        _moe_ffn_kernel,
        grid_spec=pltpu.PrefetchScalarGridSpec(
            num_scalar_prefetch=1,
            grid=(rows // tm, MOE_PER_GROUP),
            in_specs=[
                pl.BlockSpec((tm, w), lambda i, j, tg: (i, 0)),
                pl.BlockSpec((1, d, MOE_FF), expert),
                pl.BlockSpec((1, d, MOE_FF), expert),
                pl.BlockSpec((1, MOE_FF, d), expert),
            ],
            out_specs=pl.BlockSpec((tm, d), lambda i, j, tg: (i, 0)),
            scratch_shapes=[pltpu.VMEM((tm, d), BF16)],
        ),
        out_shape=jax.ShapeDtypeStruct((rows, d), F32),
        compiler_params=_cparams(("arbitrary", "arbitrary")),
        name="moe_ffn",
    )(tile_group, xs, prm["exp_w1"], prm["exp_w3"], prm["exp_w2"])


def _combine_kernel(pos_ref, x_ref, g2_ref, ys_ref, o_ref, buf_ref, sem):
    tm = x_ref.shape[0]

    def make(r):
        return pltpu.make_async_copy(ys_ref.at[pl.ds(pos_ref[r], 1)], buf_ref.at[pl.ds(r, 1)], sem)
    _row_copies(tm, make)
    o_ref[...] = x_ref[...] + g2_ref[0] * buf_ref[...]


def _moe_combine(pos, x_mid, ys, mod3, dims, n_rows, tm):
    d = x_mid.shape[1]
    modrow = dims["modrow"]
    return pl.pallas_call(
        _combine_kernel,
        grid=(n_rows // tm,),
        in_specs=[pl.BlockSpec((tm,), lambda i: (i,), memory_space=pltpu.SMEM),
                  pl.BlockSpec((tm, d), lambda i: (i, 0)),
                  pl.BlockSpec((1, 1, d), lambda i: (modrow(i, tm) * 6 + 5, 0, 0)),
                  pl.BlockSpec(memory_space=pl.ANY)],
        out_specs=pl.BlockSpec((tm, d), lambda i: (i, 0)),
        out_shape=jax.ShapeDtypeStruct((n_rows, d), F32),
        scratch_shapes=[pltpu.VMEM((tm, d), F32), pltpu.SemaphoreType.DMA(())],
        compiler_params=_cparams(("arbitrary",)),
        name="moe_combine",
    )(pos, x_mid, mod3, ys)


def _moe(x_mid, hext, plan, cnt, mod3, prm, dims, n_rows):
    tm = dims["tm_moe"]
    counts = cnt[0, :MOE_GROUPS].astype(jnp.int32)
    ends = jnp.cumsum((counts + tm - 1) // tm * tm)
    starts = ends - (counts + tm - 1) // tm * tm
    gid = plan[:, 1].astype(jnp.int32)
    pos = plan[:, 0].astype(jnp.int32)
    for g in range(MOE_GROUPS):
        pos = pos + jnp.where(gid == g, starts[g], 0)
    n_tiles = n_rows // tm + MOE_GROUPS
    tile_group = jnp.minimum(jnp.sum(jnp.arange(n_tiles)[:, None] * tm >= ends[None, :], axis=1),
                             MOE_GROUPS - 1).astype(jnp.int32)
    xs = _moe_scatter(pos, hext, n_rows, n_tiles * tm, tm)
    ys = _moe_ffn(tile_group, xs, prm, tm)
    return _moe_combine(pos, x_mid, ys, mod3, dims, n_rows, tm)


def _hi_lo(w):
    hi = w.astype(BF16)
    return hi, (w - hi.astype(F32)).astype(BF16)


def _layer_params(l, p):
    d = p["w_in"].shape[1]
    w_in = p["w_in"][l]
    o = 0
    cols = {}
    for name, size in (("z", 512), ("xbc", 1024), ("dt", 16), ("gq", 512), ("gk", 512), ("gv", 512),
                       ("g1", 32), ("gr", 512), ("lx", 512), ("lg", 512), ("aq", 512), ("ak", 256),
                       ("av", 256), ("mg", 4 * d)):
        cols[name] = w_in[:, o:o + size]
        o += size
    out = {}
    out["w_main"] = jnp.concatenate([cols[n] for n in ("mg", "xbc", "z", "gq", "gk", "gv", "gr", "lx", "lg",
                                                        "aq", "ak", "av")], axis=1).astype(BF16)
    zpad = jnp.zeros((d, 128 - 8 - GLA_RANK), F32)
    w_small = jnp.concatenate([cols["dt"][:, :8], cols["g1"][:, :GLA_RANK], zpad,
                               cols["dt"][:, 8:], cols["g1"][:, GLA_RANK:], zpad], axis=1)
    out["ws_hi"], out["ws_lo"] = _hi_lo(w_small)
    out["wst_hi"], out["wst_lo"] = _hi_lo(w_small.T)
    out["norm1"] = p["norm1"][l][None]
    out["norm2"] = p["norm2"][l][None]

    out["ssd_cw"] = p["ssd_conv_w"][l]
    out["ssd_cb"] = p["ssd_conv_b"][l][None]
    pad8 = lambda v: jnp.pad(v, ((0, 0), (0, 128 - SSD_HEADS)))
    brow = pad8(p["ssd_dt_bias"][l])
    arow = pad8(-jnp.exp(p["ssd_a_log"][l]))
    out["ssd_brow"], out["ssd_arow"] = brow[:, None, :], arow[:, None, :]
    out["ssd_bcol"], out["ssd_acol"] = brow[:, :, None], arow[:, :, None]
    head_of_lane = jnp.arange(SSD_INNER) // SSD_HEAD_DIM
    out["ssd_e"] = (jnp.arange(128)[:, None] == head_of_lane[None, :]).astype(BF16)
    out["ssd_dexp"] = jnp.repeat(p["ssd_d"][l], SSD_HEAD_DIM)[None]
    out["ssd_norm"] = p["ssd_norm"][l][None]

    g2 = jnp.zeros((2, 128, GLA_HEADS * GLA_DK), F32).at[:, SM_G1:SM_G1 + GLA_RANK].set(p["gla_g2"][l])
    out["gla_g2h"], out["gla_g2l"] = _hi_lo(g2)
    out["gla_gb"] = p["gla_gb"][l][:, None, :]
    out["gla_norm"] = p["gla_norm"][l][None]

    def pairs(w):
        w = w.reshape(2, LRU_BLOCKS // 2, 2, LRU_BLOCK, LRU_BLOCK)
        z = jnp.zeros_like(w[:, :, 0])
        top = jnp.concatenate([w[:, :, 0], z], axis=-1)
        bot = jnp.concatenate([z, w[:, :, 1]], axis=-1)
        return jnp.concatenate([top, bot], axis=-2)
    out["lru_wg"] = jnp.concatenate([pairs(p["lru_wa"][l]), pairs(p["lru_wx"][l])], axis=-1).astype(BF16)
    out["lru_cw"] = p["lru_conv_w"][l]
    out["lru_cb"] = p["lru_conv_b"][l][None]
    out["lru_ba"] = p["lru_ba"][l][:, None, :]
    out["lru_bx"] = p["lru_bx"][l][:, None, :]
    out["lru_lam"] = p["lru_lambda"][l][:, None, :]

    out["att_qnorm"] = p["att_qnorm"][l][None]
    out["att_knorm"] = p["att_knorm"][l][None]
    out["w_branch"] = p["w_branch"][l].astype(BF16)
    out["w_out"] = p["w_out"][l].astype(BF16)
    rw = jnp.concatenate([p["router_wg"][l], p["router_we"][l],
                          jnp.zeros((d, 128 - MOE_GROUPS - MOE_EXPERTS), F32)], axis=1)
    out["rw_hi"], out["rw_lo"] = _hi_lo(rw)
    out["rb"] = jnp.concatenate([p["router_bg"][l], p["router_be"][l],
                                 jnp.zeros((128 - MOE_GROUPS - MOE_EXPERTS,), F32)])[None]
    out["exp_w1"] = p["exp_w1"][l]
    out["exp_w3"] = p["exp_w3"][l]
    out["exp_w2"] = p["exp_w2"][l]
    return out


def _rope_tables(l):
    f = ATT_HEAD_DIM // 4
    inv = ROPE_THETA ** (-jnp.arange(f, dtype=F32) / f)
    tpos = jnp.arange(l, dtype=jnp.int32)
    row = (tpos // GRID_W).astype(F32)[:, None] * inv
    col = (tpos % GRID_W).astype(F32)[:, None] * inv
    cos = jnp.concatenate([jnp.cos(row), jnp.cos(row), jnp.cos(col), jnp.cos(col)], axis=1)
    sin = jnp.concatenate([-jnp.sin(row), jnp.sin(row), -jnp.sin(col), jnp.sin(col)], axis=1)
    return cos, sin


def _pick(n, cands):
    for c in cands:
        if n % c == 0:
            return c
    raise ValueError(f"no tile size for {n}")


def kernel(x, c, ctx, c_ctx, ada_w, ada_b, norm1, norm2, w_in, ssd_conv_w, ssd_conv_b, ssd_dt_bias, ssd_a_log, ssd_d, ssd_norm, gla_g2, gla_gb, gla_norm, lru_conv_w, lru_conv_b, lru_wa, lru_ba, lru_wx, lru_bx, lru_lambda, att_qnorm, att_knorm, w_branch, w_out, router_wg, router_bg, router_we, router_be, exp_w1, exp_w3, exp_w2):
    nb, l, d = x.shape
    c_len = ctx.shape[1]
    depth = ada_w.shape[0]
    tl, tc = nb * l, nb * c_len
    assert l % LRU_R == 0 and c_len % LRU_R == 0 and nb + 1 <= 8 and l % GRID_W == 0
    params = dict(norm1=norm1, norm2=norm2, w_in=w_in, ssd_conv_w=ssd_conv_w, ssd_conv_b=ssd_conv_b,
                  ssd_dt_bias=ssd_dt_bias, ssd_a_log=ssd_a_log, ssd_d=ssd_d, ssd_norm=ssd_norm,
                  gla_g2=gla_g2, gla_gb=gla_gb, gla_norm=gla_norm, lru_conv_w=lru_conv_w,
                  lru_conv_b=lru_conv_b, lru_wa=lru_wa, lru_ba=lru_ba, lru_wx=lru_wx, lru_bx=lru_bx,
                  lru_lambda=lru_lambda, att_qnorm=att_qnorm, att_knorm=att_knorm, w_branch=w_branch,
                  w_out=w_out, router_wg=router_wg, router_bg=router_bg, router_we=router_we,
                  router_be=router_be, exp_w1=exp_w1, exp_w3=exp_w3, exp_w2=exp_w2)

    tile = _pick(math.gcd(l, tc), (1024, 512, 256))

    def modrow(i, tm):
        return jnp.where(i < tl // tm, i // (l // tm), nb)

    dims = dict(B=nb, L=l, C=c_len, modrow=modrow, tm_proj=tile, tm_merge=min(tile, 512), tm_moe=tile,
                tm_qk=min(tile, 512), tq=_pick(l, (512, 256)))

    cvec = jnp.zeros((8, d), F32).at[:nb].set(c).at[nb].set(c_ctx)
    mod_all = _modulation(cvec, ada_w, ada_b)
    cos, sin = _rope_tables(l)
    x_all = jnp.concatenate([x.reshape(tl, d), ctx.reshape(tc, d)], axis=0)

    for layer in range(depth):
        last = layer == depth - 1
        prm = _layer_params(layer, params)
        mod3 = mod_all[layer].reshape(8 * 6, 1, d)
        pm, sm, smt = _in_projection(x_all, prm["norm1"], mod3, prm["w_main"], prm["ws_hi"], prm["ws_lo"],
                                     prm["wst_hi"], prm["wst_lo"], dims)
        ya = _ssd_mixer(pm, sm, smt, prm, dims)
        yb = _gla_mixer(pm, sm, prm, dims)
        yc = _lru_mixer(pm, prm, dims)
        qn, kn = _qk_prep(pm, cos, sin, prm, dims)
        yd = _attention(qn, kn, pm, dims, latent=True)
        yd_ctx = yd if last else _attention(qn, kn, pm, dims, latent=False)
        n_rows = tl if last else tl + tc
        x_mid, hext, plan, cnt = _merge(x_all, ya, yb, yc, yd, yd_ctx, pm, mod3, prm, dims,
                                        n_rows // dims["tm_merge"])
        x_all = _moe(x_mid, hext, plan, cnt, mod3, prm, dims, n_rows)
    return x_all.reshape(nb, l, d)
```

```python
import functools
import math

import jax
import jax.numpy as jnp
from jax import lax
from jax.experimental import pallas as pl
from jax.experimental.pallas import tpu as pltpu

F32 = jnp.float32
BF16 = jnp.bfloat16

EPS = 1e-6
GRID_W = 64
ROPE_THETA = 10000.0

SSD_HEADS = 8
SSD_HEAD_DIM = 64
SSD_INNER = 512
SSD_GROUPS = 2
SSD_STATE = 128
SSD_XBC = 1024
CONV_W = 4
GLA_HEADS = 4
GLA_DK = 128
GLA_RANK = 16
GLA_GATE_NORM = 16.0
GLA_CHUNK = 64
LRU_W = 512
LRU_BLOCKS = 8
LRU_BLOCK = 64
LRU_C = 8.0
ATT_HEADS = 4
ATT_KV_HEADS = 2
ATT_HEAD_DIM = 128
MOE_GROUPS = 4
MOE_PER_GROUP = 4
MOE_EXPERTS = 16
MOE_FF = 512
BRANCH_W = 512

C_MG, C_XBC, C_Z, C_GQ, C_GK, C_GV, C_GR = 0, 4096, 5120, 5632, 6144, 6656, 7168
C_LX, C_LG, C_AQ, C_AK, C_AV, N_MAIN = 7680, 8192, 8704, 9216, 9472, 9728
N_SMALL = 256
SM_DT, SM_G1 = 0, 8

HALO = 16
SSD_Q = 128
GLA_R = 256
LRU_R = 256
LRU_SEG = LRU_R // 8

VMEM_LIMIT = 56 * 1024 * 1024

_NT = (((1,), (1,)), ((), ()))
_TN = (((0,), (0,)), ((), ()))


def _dot(a, b):
    return jnp.dot(a, b, preferred_element_type=F32)


def _dot_nt(a, b):
    return lax.dot_general(a, b, _NT, preferred_element_type=F32)


def _dot_tn(a, b):
    return lax.dot_general(a, b, _TN, preferred_element_type=F32)


def _split3(x):
    h = x.astype(BF16)
    r = x - h.astype(F32)
    m = r.astype(BF16)
    l = (r - m.astype(F32)).astype(BF16)
    return h, m, l


def _dot_mask_lhs(mask_bf16, x):
    h, m, l = _split3(x)
    return _dot(mask_bf16, h) + _dot(mask_bf16, m) + _dot(mask_bf16, l)


def _dot_mask_rhs(x, mask_bf16):
    h, m, l = _split3(x)
    return _dot(h, mask_bf16) + _dot(m, mask_bf16) + _dot(l, mask_bf16)


def _sigmoid(x):
    return 0.5 * jnp.tanh(0.5 * x) + 0.5


def _silu(x):
    return x * _sigmoid(x)


def _softplus(x):
    return jnp.maximum(x, 0.0) + jnp.log1p(jnp.exp(-jnp.abs(x)))


def _cparams(sem):
    return pltpu.CompilerParams(dimension_semantics=sem, vmem_limit_bytes=VMEM_LIMIT)


def _mod_kernel(c_ref, w_ref, b_ref, o_ref):
    s = _silu(c_ref[...])
    sh, sm, sl = _split3(s)
    w = w_ref[0]
    wh = w.astype(BF16)
    wl = (w - wh.astype(F32)).astype(BF16)
    o_ref[0] = _dot(sh, wh) + _dot(sm, wh) + _dot(sh, wl) + b_ref[0]


def _modulation(cvec, ada_w, ada_b):
    ld, d, n6 = ada_w.shape
    tn = 1536
    return pl.pallas_call(
        _mod_kernel,
        grid=(ld, n6 // tn),
        in_specs=[
            pl.BlockSpec((8, d), lambda l, j: (0, 0)),
            pl.BlockSpec((1, d, tn), lambda l, j: (l, 0, j)),
            pl.BlockSpec((1, 1, tn), lambda l, j: (l, 0, j)),
        ],
        out_specs=pl.BlockSpec((1, 8, tn), lambda l, j: (l, 0, j)),
        out_shape=jax.ShapeDtypeStruct((ld, 8, n6), F32),
        compiler_params=_cparams(("arbitrary", "arbitrary")),
        name="modulation",
    )(cvec, ada_w, ada_b.reshape(ld, 1, n6))


def _inproj_kernel(x_ref, g_ref, sh_ref, sc_ref, w_ref, wsh_ref, wsl_ref, p_ref, s_ref, h_ref):
    j = pl.program_id(1)

    @pl.when(j == 0)
    def _():
        x = x_ref[...]
        h = x * lax.rsqrt(jnp.mean(x * x, axis=-1, keepdims=True) + EPS) * g_ref[...]
        h = h * (1.0 + sc_ref[0]) + sh_ref[0]
        hh = h.astype(BF16)
        hl = (h - hh.astype(F32)).astype(BF16)
        h_ref[...] = hh
        wsh = wsh_ref[...]
        s_ref[...] = _dot(hh, wsh) + _dot(hl, wsh) + _dot(hh, wsl_ref[...])

    p_ref[...] = _dot(h_ref[...], w_ref[...]).astype(BF16)


def _in_projection(x_all, norm_g, mod3, w_main, ws_hi, ws_lo, dims):
    t, d = x_all.shape
    tm, tn = dims["tm_proj"], N_MAIN // 4
    modrow = dims["modrow"]
    return pl.pallas_call(
        _inproj_kernel,
        grid=(t // tm, N_MAIN // tn),
        in_specs=[
            pl.BlockSpec((tm, d), lambda i, j: (i, 0)),
            pl.BlockSpec((1, d), lambda i, j: (0, 0)),
            pl.BlockSpec((1, 1, d), lambda i, j: (modrow(i, tm) * 6 + 0, 0, 0)),
            pl.BlockSpec((1, 1, d), lambda i, j: (modrow(i, tm) * 6 + 1, 0, 0)),
            pl.BlockSpec((d, tn), lambda i, j: (0, j)),
            pl.BlockSpec((d, N_SMALL), lambda i, j: (0, 0)),
            pl.BlockSpec((d, N_SMALL), lambda i, j: (0, 0)),
        ],
        out_specs=[
            pl.BlockSpec((tm, tn), lambda i, j: (i, j)),
            pl.BlockSpec((tm, N_SMALL), lambda i, j: (i, 0)),
        ],
        out_shape=[
            jax.ShapeDtypeStruct((t, N_MAIN), BF16),
            jax.ShapeDtypeStruct((t, N_SMALL), F32),
        ],
        scratch_shapes=[pltpu.VMEM((tm, d), BF16)],
        compiler_params=_cparams(("arbitrary", "arbitrary")),
        name="in_projection",
    )(x_all, norm_g, mod3, mod3, w_main, ws_hi, ws_lo)


def _seq_pos(p, i, nctx, nlat):
    is_ctx = i < nctx
    k = jnp.where(is_ctx, jnp.where(p == 0, i, nctx - 1 - i),
                  jnp.where(p == 0, i - nctx, nlat - 1 - (i - nctx)))
    return k, jnp.where(is_ctx, nctx, nlat)


def _seq_chunk(b, p, i, nctx, nlat, nb):
    k, _ = _seq_pos(p, i, nctx, nlat)
    return jnp.where(i < nctx, nb * nlat + b * nctx + k, b * nlat + k)


def _seq_out_chunk(b, p, i, nctx, nlat, nb):
    return _seq_chunk(b, 1, jnp.where(p == 0, 0, i), nctx, nlat, nb)


def _fwd_slot(i, nctx, nlat):
    return jnp.where(i < nctx, nctx - 1 - i, nctx + nlat - 1 - (i - nctx))


def _conv4(prev_ref, cur_ref, next_ref, w, b, pv, nv, rows):
    xe = jnp.concatenate([prev_ref[...].astype(F32) * pv, cur_ref[...].astype(F32),
                          next_ref[...].astype(F32) * nv], axis=0)
    n = rows + 2 * HALO
    y = (pltpu.roll(xe, 2, 0) * w[0:1] + pltpu.roll(xe, 1, 0) * w[1:2] + xe * w[2:3]
         + pltpu.roll(xe, n - 1, 0) * w[3:4])
    return y[HALO:HALO + rows] + b


def _dir_mask(p, n):
    r = lax.broadcasted_iota(jnp.int32, (n, n), 0)
    c = lax.broadcasted_iota(jnp.int32, (n, n), 1)
    sgn = 1 - 2 * p
    return (r - c) * sgn >= 0


def _ssd_kernel(xc_ref, xp_ref, xn_ref, z_ref, sm_ref, cw_ref, cb_ref, brow_ref, arow_ref,
                e_ref, dexp_ref, nrm_ref, o_ref, yf_ref, ht_ref, xbc_ref, *, nctx, nlat):
    q = SSD_Q
    p = pl.program_id(1)
    i = pl.program_id(2)
    k, n = _seq_pos(p, i, nctx, nlat)
    pv = (k > 0).astype(F32)
    nv = (k < n - 1).astype(F32)

    @pl.when(i == 0)
    def _():
        ht_ref[...] = jnp.zeros_like(ht_ref)

    @pl.when(p == 0)
    def _():
        xbc_ref[i] = _silu(_conv4(xp_ref, xc_ref, xn_ref, cw_ref[...], cb_ref[...], pv, nv, q)).astype(BF16)

    xbc = xbc_ref[jnp.where(p == 0, i, _fwd_slot(i, nctx, nlat))]
    xs = xbc[:, :SSD_INNER].astype(F32)
    bmat = xbc[:, SSD_INNER:SSD_INNER + 256]
    cmat = xbc[:, SSD_INNER + 256:]

    mask = _dir_mask(p, q)
    tri = jnp.where(mask, 1.0, 0.0).astype(BF16)

    dt_c = _softplus(sm_ref[...] + brow_ref[0])
    dta_c = dt_c * arow_ref[0]
    acum_c = _dot_mask_lhs(tri, dta_c)
    alast_c = jnp.sum(dta_c, axis=0, keepdims=True)
    acum_r = acum_c.T

    e = e_ref[...]
    stack = jnp.concatenate([dt_c, jnp.exp(acum_c), dt_c * jnp.exp(alast_c - acum_c)], axis=0)
    ex = _dot(stack.astype(BF16), e)
    dt_e, ea_e, ds_e = ex[:q], ex[q:2 * q], ex[2 * q:]
    dec_e = _dot_mask_rhs(jnp.broadcast_to(jnp.exp(alast_c), (8, 128)), e)[0:1]

    dtx = (xs * dt_e).astype(BF16)
    dsx = (xs * ds_e).astype(BF16)
    lane = lax.broadcasted_iota(jnp.int32, (q, 128), 1)
    ydiag, yoff = [], []
    for g in range(SSD_GROUPS):
        bg = bmat[:, g * 128:(g + 1) * 128]
        cg = cmat[:, g * 128:(g + 1) * 128]
        cb = _dot_nt(cg, bg)
        htg = ht_ref[g]
        yoff.append(_dot(cg, htg.astype(BF16)))
        ht_ref[g] = htg * dec_e[:, g * 256:(g + 1) * 256] + _dot_tn(bg, dsx[:, g * 256:(g + 1) * 256])
        for pr in range(2):
            hd = g * 4 + pr * 2
            pair = dtx[:, hd * 64:hd * 64 + 128]
            outs = []
            for hh in (hd, hd + 1):
                seg = acum_c[:, hh:hh + 1] - acum_r[hh:hh + 1, :]
                lm = jnp.exp(jnp.where(mask, seg, -1e30))
                outs.append(_dot((cb * lm).astype(BF16), pair))
            ydiag.append(jnp.where(lane < 64, outs[0], outs[1]))
    y = jnp.concatenate(ydiag, axis=1) + jnp.concatenate(yoff, axis=1) * ea_e

    @pl.when(p == 0)
    def _():
        yf_ref[i] = y

    @pl.when(p == 1)
    def _():
        yt = (y + yf_ref[_fwd_slot(i, nctx, nlat)] + dexp_ref[...] * xs) * _silu(z_ref[...].astype(F32))
        yn = yt * lax.rsqrt(jnp.mean(yt * yt, axis=-1, keepdims=True) + EPS) * nrm_ref[...]
        o_ref[...] = yn.astype(BF16)


def _ssd_mixer(pm, sm, prm, dims):
    t = pm.shape[0]
    nb, nctx, nlat = dims["B"], dims["C"] // SSD_Q, dims["L"] // SSD_Q
    nhalo = t // HALO
    per = SSD_Q // HALO
    ck = functools.partial(_seq_chunk, nctx=nctx, nlat=nlat, nb=nb)
    oc = functools.partial(_seq_out_chunk, nctx=nctx, nlat=nlat, nb=nb)
    full = lambda shape: pl.BlockSpec(shape, lambda b, p, i: (0,) * len(shape))
    bydir = lambda shape: pl.BlockSpec((1,) + shape, lambda b, p, i: (p,) + (0,) * len(shape))
    return pl.pallas_call(
        functools.partial(_ssd_kernel, nctx=nctx, nlat=nlat),
        grid=(nb, 2, nctx + nlat),
        in_specs=[
            pl.BlockSpec((SSD_Q, SSD_XBC), lambda b, p, i: (ck(b, p, i), C_XBC // SSD_XBC)),
            pl.BlockSpec((HALO, SSD_XBC),
                         lambda b, p, i: (jnp.maximum(ck(b, p, i) * per - 1, 0), C_XBC // SSD_XBC)),
            pl.BlockSpec((HALO, SSD_XBC),
                         lambda b, p, i: (jnp.minimum(ck(b, p, i) * per + per, nhalo - 1), C_XBC // SSD_XBC)),
            pl.BlockSpec((SSD_Q, SSD_INNER), lambda b, p, i: (ck(b, p, i), C_Z // SSD_INNER)),
            pl.BlockSpec((SSD_Q, 128), lambda b, p, i: (ck(b, p, i), p)),
            full((CONV_W, SSD_XBC)), full((1, SSD_XBC)),
            bydir((1, 128)), bydir((1, 128)),
            full((128, SSD_INNER)), full((1, SSD_INNER)), full((1, SSD_INNER)),
        ],
        out_specs=pl.BlockSpec((SSD_Q, SSD_INNER), lambda b, p, i: (oc(b, p, i), 0)),
        out_shape=jax.ShapeDtypeStruct((t, SSD_INNER), BF16),
        scratch_shapes=[pltpu.VMEM((nctx + nlat, SSD_Q, SSD_INNER), F32),
                        pltpu.VMEM((SSD_GROUPS, SSD_STATE, 256), F32),
                        pltpu.VMEM((nctx + nlat, SSD_Q, SSD_XBC), BF16)],
        compiler_params=_cparams(("arbitrary", "arbitrary", "arbitrary")),
        name="ssd_mixer",
    )(pm, pm, pm, pm, sm, prm["ssd_cw"], prm["ssd_cb"], prm["ssd_brow"], prm["ssd_arow"],
      prm["ssd_e"], prm["ssd_dexp"], prm["ssd_norm"])


def _gla_kernel(q_ref, k_ref, v_ref, r_ref, sm_ref, g2h_ref, g2l_ref, gb_ref, nrm_ref,
                o_ref, yf_ref, st_ref, *, nctx, nlat):
    qc = GLA_CHUNK
    nsub = GLA_R // qc
    p = pl.program_id(1)
    i = pl.program_id(2)

    @pl.when(i == 0)
    def _():
        st_ref[...] = jnp.zeros_like(st_ref)

    r = GLA_R
    rr = lax.broadcasted_iota(jnp.int32, (r, r), 0)
    cc = lax.broadcasted_iota(jnp.int32, (r, r), 1)
    shift = qc.bit_length() - 1
    mask = (lax.shift_right_logical(rr, shift) == lax.shift_right_logical(cc, shift)) & ((rr - cc) * (1 - 2 * p) >= 0)
    tri = jnp.where(mask, 1.0, 0.0).astype(BF16)
    scale = GLA_DK ** -0.5

    sm = sm_ref[...]
    smh = sm.astype(BF16)
    sml = (sm - smh.astype(F32)).astype(BF16)
    g2h = g2h_ref[0]
    logit = _dot(smh, g2h) + _dot(sml, g2h) + _dot(smh, g2l_ref[0]) + gb_ref[0]
    g = -_softplus(-logit) * (1.0 / GLA_GATE_NORM)
    gc = _dot_mask_lhs(tri, g)
    glast = [jnp.sum(g[s * qc:(s + 1) * qc], axis=0, keepdims=True) for s in range(nsub)]
    bcast = lambda rows_: jnp.concatenate([jnp.broadcast_to(x, (qc, x.shape[1])) for x in rows_], axis=0)
    gref = bcast([gc[s * qc + qc // 2:s * qc + qc // 2 + 1] for s in range(nsub)])
    glast_f = bcast(glast)
    qf = q_ref[...].astype(F32) * scale
    kf = k_ref[...].astype(F32)
    vb = v_ref[...]
    qe = (qf * jnp.exp(gc - gref)).astype(BF16)
    ke = (kf * jnp.exp(gref - gc)).astype(BF16)
    qg = (qf * jnp.exp(gc)).astype(BF16)
    k2 = (kf * jnp.exp(glast_f - gc)).astype(BF16)
    o_intra = []
    for h in range(GLA_HEADS):
        hs = slice(h * GLA_DK, (h + 1) * GLA_DK)
        att = jnp.where(mask, _dot_nt(qe[:, hs], ke[:, hs]), 0.0).astype(BF16)
        o_intra.append(_dot(att, vb[:, hs]))

    def sweep(order, emit):
        st = [st_ref[h] for h in range(GLA_HEADS)]
        for s in order:
            rs = slice(s * qc, (s + 1) * qc)
            dec = jnp.exp(glast[s])
            outs = []
            for h in range(GLA_HEADS):
                hs = slice(h * GLA_DK, (h + 1) * GLA_DK)
                outs.append(o_intra[h][rs] + _dot_nt(qg[rs, hs], st[h].astype(BF16)))
                st[h] = st[h] * dec[:, hs] + _dot_tn(vb[rs, hs], k2[rs, hs])
            emit(rs, jnp.concatenate(outs, axis=1))
        for h in range(GLA_HEADS):
            st_ref[h] = st[h]

    @pl.when(p == 0)
    def _():
        def emit(rs, y):
            yf_ref[i, rs, :] = y
        sweep(range(nsub), emit)

    @pl.when(p == 1)
    def _():
        slot = _fwd_slot(i, nctx, nlat)
        nrm = nrm_ref[...]

        def emit(rs, y):
            yt = y + yf_ref[slot, rs, :]
            parts = []
            for h in range(GLA_HEADS):
                hs = slice(h * GLA_DK, (h + 1) * GLA_DK)
                yh = yt[:, hs]
                parts.append(yh * lax.rsqrt(jnp.mean(yh * yh, axis=-1, keepdims=True) + EPS) * nrm[:, hs])
            o_ref[rs, :] = (jnp.concatenate(parts, axis=1) * _silu(r_ref[rs, :].astype(F32))).astype(BF16)
        sweep(range(nsub - 1, -1, -1), emit)


def _gla_mixer(pm, sm, prm, dims):
    t = pm.shape[0]
    nb, nctx, nlat = dims["B"], dims["C"] // GLA_R, dims["L"] // GLA_R
    ck = functools.partial(_seq_chunk, nctx=nctx, nlat=nlat, nb=nb)
    oc = functools.partial(_seq_out_chunk, nctx=nctx, nlat=nlat, nb=nb)
    w = GLA_HEADS * GLA_DK
    col = lambda c: pl.BlockSpec((GLA_R, w), lambda b, p, i: (ck(b, p, i), c // w))
    bydir = lambda shape: pl.BlockSpec((1,) + shape, lambda b, p, i: (p,) + (0,) * len(shape))
    return pl.pallas_call(
        functools.partial(_gla_kernel, nctx=nctx, nlat=nlat),
        grid=(nb, 2, nctx + nlat),
        in_specs=[
            col(C_GQ), col(C_GK), col(C_GV), col(C_GR),
            pl.BlockSpec((GLA_R, 128), lambda b, p, i: (ck(b, p, i), p)),
            bydir((128, w)), bydir((128, w)), bydir((1, w)),
            pl.BlockSpec((1, w), lambda b, p, i: (0, 0)),
        ],
        out_specs=pl.BlockSpec((GLA_R, w), lambda b, p, i: (oc(b, p, i), 0)),
        out_shape=jax.ShapeDtypeStruct((t, w), BF16),
        scratch_shapes=[pltpu.VMEM((nctx + nlat, GLA_R, w), F32),
                        pltpu.VMEM((GLA_HEADS, GLA_DK, GLA_DK), F32)],
        compiler_params=_cparams(("arbitrary", "arbitrary", "arbitrary")),
        name="gla_mixer",
    )(pm, pm, pm, pm, sm, prm["gla_g2h"], prm["gla_g2l"], prm["gla_gb"], prm["gla_norm"])


def _lru_kernel(xc_ref, xp_ref, xn_ref, gl_ref, cw_ref, cb_ref, wg_ref, ba_ref, bx_ref, lam_ref,
                o_ref, yf_ref, h_ref, a_scr, v_scr, as_scr, hs_scr, *, nctx, nlat):
    r, seg = LRU_R, LRU_SEG
    p = pl.program_id(1)
    i = pl.program_id(2)
    k, n = _seq_pos(p, i, nctx, nlat)
    pv = (k > 0).astype(F32)
    nv = (k < n - 1).astype(F32)

    @pl.when(i == 0)
    def _():
        h_ref[...] = jnp.zeros_like(h_ref)

    u = _conv4(xp_ref, xc_ref, xn_ref, cw_ref[...], cb_ref[...], pv, nv, r)
    ub = u.astype(BF16)
    ra, ix = [], []
    for j in range(LRU_W // 128):
        gj = _dot(ub[:, j * 128:(j + 1) * 128], wg_ref[0, j])
        ra.append(gj[:, :128])
        ix.append(gj[:, 128:])
    rg = _sigmoid(jnp.concatenate(ra, axis=1) + ba_ref[0])
    ig = _sigmoid(jnp.concatenate(ix, axis=1) + bx_ref[0])
    log_a = (-LRU_C * _softplus(-lam_ref[0])) * rg
    a_all = jnp.exp(log_a)
    th = jnp.tanh(log_a)
    v_all = u * ig * jnp.sqrt(-2.0 * th / (1.0 - th))
    nslab = LRU_W // 128
    for j in range(nslab):
        a_scr[j] = a_all[:, j * 128:(j + 1) * 128]
        v_scr[j] = v_all[:, j * 128:(j + 1) * 128]

    def scan(order, seg_order):
        for j in range(nslab):
            ls = slice(j * 128, (j + 1) * 128)
            acc_a = jnp.ones((8, 128), F32)
            acc_h = jnp.zeros((8, 128), F32)
            for kk in order:
                ak = a_scr[j, pl.ds(kk, 8, stride=seg), :]
                vk = v_scr[j, pl.ds(kk, 8, stride=seg), :]
                acc_h = ak * acc_h + vk
                acc_a = ak * acc_a
                as_scr[j, kk * 8:(kk + 1) * 8, :] = acc_a
                hs_scr[j, kk * 8:(kk + 1) * 8, :] = acc_h
            carry = h_ref[0:1, ls]
            rows = [None] * 8
            for s in seg_order:
                rows[s] = carry
                carry = acc_a[s:s + 1, :] * carry + acc_h[s:s + 1, :]
            h_ref[0:1, ls] = carry
            cin = jnp.concatenate(rows, axis=0)
            for kk in order:
                ks = slice(kk * 8, (kk + 1) * 8)
                hs_scr[j, ks, :] = hs_scr[j, ks, :] + as_scr[j, ks, :] * cin

    @pl.when(p == 0)
    def _():
        scan(range(seg), range(8))

    @pl.when(p == 1)
    def _():
        scan(range(seg - 1, -1, -1), range(7, -1, -1))

    def natural(s):
        return jnp.concatenate([hs_scr[j, pl.ds(s, seg, stride=8), :] for j in range(nslab)], axis=1)

    @pl.when(p == 0)
    def _():
        for s in range(8):
            yf_ref[i, s * seg:(s + 1) * seg, :] = natural(s)

    @pl.when(p == 1)
    def _():
        slot = _fwd_slot(i, nctx, nlat)
        c0 = math.sqrt(2.0 / math.pi)
        for s in range(8):
            rs = slice(s * seg, (s + 1) * seg)
            gt = gl_ref[rs, :].astype(F32)
            gelu = 0.5 * gt * (1.0 + jnp.tanh(c0 * (gt + 0.044715 * (gt * gt * gt))))
            o_ref[rs, :] = ((natural(s) + yf_ref[slot, rs, :]) * gelu).astype(BF16)


def _lru_mixer(pm, prm, dims):
    t = pm.shape[0]
    nb, nctx, nlat = dims["B"], dims["C"] // LRU_R, dims["L"] // LRU_R
    nhalo = t // HALO
    per = LRU_R // HALO
    ck = functools.partial(_seq_chunk, nctx=nctx, nlat=nlat, nb=nb)
    oc = functools.partial(_seq_out_chunk, nctx=nctx, nlat=nlat, nb=nb)
    w = LRU_W
    full = lambda shape: pl.BlockSpec(shape, lambda b, p, i: (0,) * len(shape))
    bydir = lambda shape: pl.BlockSpec((1,) + shape, lambda b, p, i: (p,) + (0,) * len(shape))
    return pl.pallas_call(
        functools.partial(_lru_kernel, nctx=nctx, nlat=nlat),
        grid=(nb, 2, nctx + nlat),
        in_specs=[
            pl.BlockSpec((LRU_R, w), lambda b, p, i: (ck(b, p, i), C_LX // w)),
            pl.BlockSpec((HALO, w), lambda b, p, i: (jnp.maximum(ck(b, p, i) * per - 1, 0), C_LX // w)),
            pl.BlockSpec((HALO, w), lambda b, p, i: (jnp.minimum(ck(b, p, i) * per + per, nhalo - 1), C_LX // w)),
            pl.BlockSpec((LRU_R, w), lambda b, p, i: (ck(b, p, i), C_LG // w)),
            full((CONV_W, w)), full((1, w)),
            bydir((w // 128, 128, 256)), bydir((1, w)), bydir((1, w)), bydir((1, w)),
        ],
        out_specs=pl.BlockSpec((LRU_R, w), lambda b, p, i: (oc(b, p, i), 0)),
        out_shape=jax.ShapeDtypeStruct((t, w), BF16),
        scratch_shapes=[pltpu.VMEM((nctx + nlat, LRU_R, w), F32),
                        pltpu.VMEM((8, w), F32),
                        pltpu.VMEM((w // 128, LRU_R, 128), F32), pltpu.VMEM((w // 128, LRU_R, 128), F32),
                        pltpu.VMEM((w // 128, LRU_R, 128), F32), pltpu.VMEM((w // 128, LRU_R, 128), F32)],
        compiler_params=_cparams(("arbitrary", "arbitrary", "arbitrary")),
        name="lru_mixer",
    )(pm, pm, pm, pm, prm["lru_cw"], prm["lru_cb"], prm["lru_wg"], prm["lru_ba"], prm["lru_bx"],
      prm["lru_lam"])


def _qkprep_kernel(q_ref, k_ref, cos_ref, sin_ref, qn_ref, kn_ref, qo_ref, ko_ref, *, nlat_tiles):
    i = pl.program_id(0)
    is_ctx = i >= nlat_tiles
    cos = jnp.where(is_ctx, 1.0, cos_ref[...])
    sin = jnp.where(is_ctx, 0.0, sin_ref[...])
    lane = lax.broadcasted_iota(jnp.int32, cos.shape, 1)
    first = (lane % 64) < 32

    def prep(x, g, scale):
        xn = x * lax.rsqrt(jnp.mean(x * x, axis=-1, keepdims=True) + EPS) * g
        sw = jnp.where(first, pltpu.roll(xn, 96, 1), pltpu.roll(xn, 32, 1))
        return ((xn * cos + sw * sin) * scale).astype(BF16)

    qn, kn = qn_ref[...], kn_ref[...]
    for h in range(ATT_HEADS):
        hs = slice(h * ATT_HEAD_DIM, (h + 1) * ATT_HEAD_DIM)
        qo_ref[:, hs] = prep(q_ref[:, hs].astype(F32), qn, ATT_HEAD_DIM ** -0.5)
    for h in range(ATT_KV_HEADS):
        hs = slice(h * ATT_HEAD_DIM, (h + 1) * ATT_HEAD_DIM)
        ko_ref[:, hs] = prep(k_ref[:, hs].astype(F32), kn, 1.0)


def _qk_prep(pm, cos, sin, prm, dims):
    t = pm.shape[0]
    tm = dims["tm_qk"]
    nlat_tiles = dims["B"] * dims["L"] // tm
    per_seq = dims["L"] // tm
    return pl.pallas_call(
        functools.partial(_qkprep_kernel, nlat_tiles=nlat_tiles),
        grid=(t // tm,),
        in_specs=[
            pl.BlockSpec((tm, 512), lambda i: (i, C_AQ // 512)),
            pl.BlockSpec((tm, 256), lambda i: (i, C_AK // 256)),
            pl.BlockSpec((tm, 128), lambda i: (i % per_seq, 0)),
            pl.BlockSpec((tm, 128), lambda i: (i % per_seq, 0)),
            pl.BlockSpec((1, 128), lambda i: (0, 0)),
            pl.BlockSpec((1, 128), lambda i: (0, 0)),
        ],
        out_specs=[pl.BlockSpec((tm, 512), lambda i: (i, 0)), pl.BlockSpec((tm, 256), lambda i: (i, 0))],
        out_shape=[jax.ShapeDtypeStruct((t, 512), BF16), jax.ShapeDtypeStruct((t, 256), BF16)],
        compiler_params=_cparams(("arbitrary",)),
        name="qk_prep",
    )(pm, pm, cos, sin, prm["att_qnorm"], prm["att_knorm"])


def _attn_kernel(q_ref, kc_ref, vc_ref, *rest, tk, nlat_chunks):
    if nlat_chunks:
        kl_ref, vl_ref, o_ref = rest
    else:
        (o_ref,) = rest
    tq = q_ref.shape[0]
    dh = ATT_HEAD_DIM
    q = jnp.concatenate([q_ref[:, :dh], q_ref[:, dh:]], axis=0)

    chunks = [(kc_ref, vc_ref, slice(None))]
    chunks += [(kl_ref, vl_ref, slice(c * tk, (c + 1) * tk)) for c in range(nlat_chunks)]
    m = l = acc = None
    s_next = _dot_nt(q, kc_ref[...])
    for c, (_, v_ref, rows) in enumerate(chunks):
        s = s_next
        if c + 1 < len(chunks):
            k_ref, _, nrows = chunks[c + 1]
            s_next = _dot_nt(q, k_ref[nrows, :])
        mx = jnp.max(s, axis=-1, keepdims=True)
        if m is None:
            m = mx
            pmat = jnp.exp(s - m)
            l = jnp.sum(pmat, axis=-1, keepdims=True)
            acc = _dot(pmat.astype(BF16), v_ref[rows, :])
        else:
            m_new = jnp.maximum(m, mx)
            alpha = jnp.exp(m - m_new)
            pmat = jnp.exp(s - m_new)
            l = alpha * l + jnp.sum(pmat, axis=-1, keepdims=True)
            acc = alpha * acc + _dot(pmat.astype(BF16), v_ref[rows, :])
            m = m_new
    o = acc * (1.0 / l)
    o_ref[...] = jnp.concatenate([o[:tq], o[tq:]], axis=1).astype(BF16)


def _attention(qn, kn, pm, dims, latent):
    nb, l, c = dims["B"], dims["L"], dims["C"]
    dh = ATT_HEAD_DIM
    tq = dims["tq"] if latent else c
    nq = (l if latent else c) // tq
    ctx_blk = nb * l // c
    qrow = (lambda b, i: b * nq + i) if latent else (lambda b, i: ctx_blk + b)
    av = C_AV // dh
    in_specs = [
        pl.BlockSpec((tq, 2 * dh), lambda b, g, i: (qrow(b, i), g)),
        pl.BlockSpec((c, dh), lambda b, g, i: (ctx_blk + b, g)),
        pl.BlockSpec((c, dh), lambda b, g, i: (ctx_blk + b, av + g)),
    ]
    args = [qn, kn, pm]
    if latent:
        in_specs += [pl.BlockSpec((l, dh), lambda b, g, i: (b, g)),
                     pl.BlockSpec((l, dh), lambda b, g, i: (b, av + g))]
        args += [kn, pm]
    tk = _pick(l, (1024, 512, 256))
    return pl.pallas_call(
        functools.partial(_attn_kernel, tk=tk, nlat_chunks=(l // tk if latent else 0)),
        grid=(nb, ATT_KV_HEADS, nq),
        in_specs=in_specs,
        out_specs=pl.BlockSpec((tq, 2 * dh), lambda b, g, i: (b * nq + i, g)),
        out_shape=jax.ShapeDtypeStruct((nb * (l if latent else c), ATT_HEADS * dh), BF16),
        compiler_params=_cparams(("arbitrary", "arbitrary", "arbitrary")),
        name="attention_latent" if latent else "attention_context",
    )(*args)


def _route_weights(lg):
    lane = lax.broadcasted_iota(jnp.int32, lg.shape, 1)
    neg = -1e30
    is_g = lane < MOE_GROUPS
    gl = jnp.where(is_g, lg, neg)
    gmax = jnp.max(gl, axis=-1, keepdims=True)
    gsum = jnp.sum(jnp.where(is_g, jnp.exp(gl - gmax), 0.0), axis=-1, keepdims=True)
    gi = jnp.min(jnp.where(is_g & (gl == gmax), lane, 1 << 20), axis=-1, keepdims=True)
    pg_sel = 1.0 / gsum
    lo = MOE_GROUPS + gi * MOE_PER_GROUP
    in_grp = (lane >= lo) & (lane < lo + MOE_PER_GROUP)
    el = jnp.where(in_grp, lg, neg)
    emax = jnp.max(el, axis=-1, keepdims=True)
    ex = jnp.where(in_grp, jnp.exp(el - emax), 0.0)
    pe = ex / jnp.sum(ex, axis=-1, keepdims=True)
    pe_m = jnp.where(in_grp, pe, -1.0)
    v1 = jnp.max(pe_m, axis=-1, keepdims=True)
    i1 = jnp.min(jnp.where(pe_m == v1, lane, 1 << 20), axis=-1, keepdims=True)
    pe_m2 = jnp.where(lane == i1, -1.0, pe_m)
    v2 = jnp.max(pe_m2, axis=-1, keepdims=True)
    i2 = jnp.min(jnp.where(pe_m2 == v2, lane, 1 << 20), axis=-1, keepdims=True)
    tot = v1 + v2
    w = jnp.where(lane == i1, v1 / tot, 0.0) + jnp.where(lane == i2, v2 / tot, 0.0)
    return w * pg_sel, gi


def _merge_kernel(x_ref, ya_ref, yb_ref, yc_ref, ydl_ref, ydc_ref, mg_ref, g1_ref, sh_ref, sc_ref, n2_ref,
                  wb_ref, wo_ref, rwh_ref, rwl_ref, rb_ref, xo_ref, h_ref, plan_ref, cnt_ref, run_ref,
                  *, nlat_tiles):
    i = pl.program_id(0)
    tm, d = x_ref.shape

    @pl.when(i == 0)
    def _():
        run_ref[...] = jnp.zeros_like(run_ref)

    yd = jnp.where(i < nlat_tiles, ydl_ref[...], ydc_ref[...])
    acc = None
    for nbr, y in enumerate((ya_ref[...], yb_ref[...], yc_ref[...], yd)):
        gate = _sigmoid(mg_ref[:, nbr * d:(nbr + 1) * d].astype(F32))
        term = gate * _dot(y, wb_ref[nbr])
        acc = term if acc is None else acc + term
    xn = x_ref[...] + g1_ref[0] * _dot(acc.astype(BF16), wo_ref[...])
    xo_ref[...] = xn
    h = xn * lax.rsqrt(jnp.mean(xn * xn, axis=-1, keepdims=True) + EPS) * n2_ref[...]
    h = h * (1.0 + sc_ref[0]) + sh_ref[0]
    hh = h.astype(BF16)
    hl = (h - hh.astype(F32)).astype(BF16)
    rwh = rwh_ref[...]
    lg = _dot(hh, rwh) + _dot(hl, rwh) + _dot(hh, rwl_ref[...]) + rb_ref[...]
    rw, gi = _route_weights(lg)
    h_ref[:, :d] = h
    h_ref[:, d:] = rw

    lane = lax.broadcasted_iota(jnp.int32, (tm, 128), 1)
    onehot = jnp.where(lane == gi, 1.0, 0.0)
    r = lax.broadcasted_iota(jnp.int32, (tm, tm), 0)
    c = lax.broadcasted_iota(jnp.int32, (tm, tm), 1)
    before = jnp.where(c < r, 1.0, 0.0).astype(BF16)
    run = run_ref[0:1, :]
    rank = jnp.sum(onehot * (_dot(before, onehot.astype(BF16)) + run), axis=-1, keepdims=True)
    plan_ref[...] = jnp.where(lane == 0, rank, jnp.where(lane == 1, gi.astype(F32), 0.0))
    run = run + jnp.sum(onehot, axis=0, keepdims=True)
    run_ref[0:1, :] = run
    cnt_ref[...] = jnp.broadcast_to(run, cnt_ref.shape)


def _merge(x_all, ya, yb, yc, yd_lat, yd_ctx, pm, mod3, prm, dims, n_tiles):
    d = x_all.shape[1]
    tm = dims["tm_merge"]
    rows = n_tiles * tm
    nlat_tiles = dims["B"] * dims["L"] // tm
    modrow = dims["modrow"]
    row = lambda shape: pl.BlockSpec(shape, lambda i: (i, 0))
    full = lambda shape: pl.BlockSpec(shape, lambda i: (0,) * len(shape))
    modspec = lambda comp: pl.BlockSpec((1, 1, d), lambda i: (modrow(i, tm) * 6 + comp, 0, 0))
    return pl.pallas_call(
        functools.partial(_merge_kernel, nlat_tiles=nlat_tiles),
        grid=(n_tiles,),
        in_specs=[
            row((tm, d)), row((tm, BRANCH_W)), row((tm, BRANCH_W)), row((tm, BRANCH_W)),
            pl.BlockSpec((tm, BRANCH_W), lambda i: (jnp.minimum(i, nlat_tiles - 1), 0)),
            pl.BlockSpec((tm, BRANCH_W), lambda i: (jnp.maximum(i - nlat_tiles, 0), 0)),
            pl.BlockSpec((tm, 4 * d), lambda i: (i, C_MG // (4 * d))),
            modspec(2), modspec(3), modspec(4), full((1, d)),
            full((4, BRANCH_W, d)), full((d, d)), full((d, 128)), full((d, 128)), full((1, 128)),
        ],
        out_specs=[row((tm, d)), row((tm, d + 128)), row((tm, 128)), full((8, 128))],
        out_shape=[jax.ShapeDtypeStruct((rows, d), F32), jax.ShapeDtypeStruct((rows, d + 128), F32),
                   jax.ShapeDtypeStruct((rows, 128), F32), jax.ShapeDtypeStruct((8, 128), F32)],
        scratch_shapes=[pltpu.VMEM((8, 128), F32)],
        compiler_params=_cparams(("arbitrary",)),
        name="merge",
    )(x_all, ya, yb, yc, yd_lat, yd_ctx, pm, mod3, mod3, mod3, prm["norm2"], prm["w_branch"], prm["w_out"],
      prm["rw_hi"], prm["rw_lo"], prm["rb"])


DMA_UNROLL = 8


def _row_copies(n, make):
    def issue(blk, carry):
        for u in range(DMA_UNROLL):
            make(blk * DMA_UNROLL + u).start()
        return carry
    lax.fori_loop(0, n // DMA_UNROLL, issue, 0)

    def drain(blk, carry):
        for u in range(DMA_UNROLL):
            make(0).wait()
        return carry
    lax.fori_loop(0, n // DMA_UNROLL, drain, 0)


def _scatter_kernel(pos_ref, src_ref, init_ref, dst_ref, sem, *, tm):
    del init_ref

    def make(r):
        return pltpu.make_async_copy(src_ref.at[pl.ds(r, 1)], dst_ref.at[pl.ds(pos_ref[r], 1)], sem)
    _row_copies(tm, make)


def _moe_scatter(pos, hext, n_rows, sorted_rows, tm):
    w = hext.shape[1]
    return pl.pallas_call(
        functools.partial(_scatter_kernel, tm=tm),
        grid=(n_rows // tm,),
        in_specs=[pl.BlockSpec((tm,), lambda i: (i,), memory_space=pltpu.SMEM),
                  pl.BlockSpec((tm, w), lambda i: (i, 0)), pl.BlockSpec(memory_space=pl.ANY)],
        out_specs=pl.BlockSpec(memory_space=pl.ANY),
        out_shape=jax.ShapeDtypeStruct((sorted_rows, w), F32),
        scratch_shapes=[pltpu.SemaphoreType.DMA(())],
        input_output_aliases={2: 0},
        compiler_params=_cparams(("arbitrary",)),
        name="moe_scatter",
    )(pos, hext, jnp.zeros((sorted_rows, w), F32))


def _moe_ffn_kernel(tg_ref, h_ref, w1_ref, w3_ref, w2_ref, o_ref, hb_ref):
    i = pl.program_id(0)
    j = pl.program_id(1)
    d = o_ref.shape[1]

    @pl.when(j == 0)
    def _():
        hb_ref[...] = h_ref[:, :d].astype(BF16)
        o_ref[...] = jnp.zeros_like(o_ref)

    h = hb_ref[...]
    a = _silu(_dot(h, w1_ref[0].astype(BF16))) * _dot(h, w3_ref[0].astype(BF16))
    y = _dot(a.astype(BF16), w2_ref[0].astype(BF16))
    rw = h_ref[:, d:]
    lane = lax.broadcasted_iota(jnp.int32, rw.shape, 1)
    e = tg_ref[i] * MOE_PER_GROUP + j
    we = jnp.sum(jnp.where(lane == e + MOE_GROUPS, rw, 0.0), axis=-1, keepdims=True)
    o_ref[...] += we * y


def _moe_ffn(tile_group, xs, layer, params, tm):
    rows, w = xs.shape
    d = w - 128
    stacked = lambda a: a.reshape((a.shape[0] * a.shape[1],) + a.shape[2:])
    expert = lambda i, j, tg: (layer * MOE_EXPERTS + tg[i] * MOE_PER_GROUP + j, 0, 0)
    return pl.pallas_call(
        _moe_ffn_kernel,
        grid_spec=pltpu.PrefetchScalarGridSpec(
            num_scalar_prefetch=1,
            grid=(rows // tm, MOE_PER_GROUP),
            in_specs=[
                pl.BlockSpec((tm, w), lambda i, j, tg: (i, 0)),
                pl.BlockSpec((1, d, MOE_FF), expert),
                pl.BlockSpec((1, d, MOE_FF), expert),
                pl.BlockSpec((1, MOE_FF, d), expert),
            ],
            out_specs=pl.BlockSpec((tm, d), lambda i, j, tg: (i, 0)),
            scratch_shapes=[pltpu.VMEM((tm, d), BF16)],
        ),
        out_shape=jax.ShapeDtypeStruct((rows, d), F32),
        compiler_params=_cparams(("arbitrary", "arbitrary")),
        name="moe_ffn",
    )(tile_group, xs, stacked(params["exp_w1"]), stacked(params["exp_w3"]), stacked(params["exp_w2"]))


def _combine_kernel(pos_ref, x_ref, g2_ref, ys_ref, o_ref, buf_ref, sem):
    tm = x_ref.shape[0]

    def make(r):
        return pltpu.make_async_copy(ys_ref.at[pl.ds(pos_ref[r], 1)], buf_ref.at[pl.ds(r, 1)], sem)
    _row_copies(tm, make)
    o_ref[...] = x_ref[...] + g2_ref[0] * buf_ref[...]


def _moe_combine(pos, x_mid, ys, mod3, dims, n_rows, tm):
    d = x_mid.shape[1]
    modrow = dims["modrow"]
    return pl.pallas_call(
        _combine_kernel,
        grid=(n_rows // tm,),
        in_specs=[pl.BlockSpec((tm,), lambda i: (i,), memory_space=pltpu.SMEM),
                  pl.BlockSpec((tm, d), lambda i: (i, 0)),
                  pl.BlockSpec((1, 1, d), lambda i: (modrow(i, tm) * 6 + 5, 0, 0)),
                  pl.BlockSpec(memory_space=pl.ANY)],
        out_specs=pl.BlockSpec((tm, d), lambda i: (i, 0)),
        out_shape=jax.ShapeDtypeStruct((n_rows, d), F32),
        scratch_shapes=[pltpu.VMEM((tm, d), F32), pltpu.SemaphoreType.DMA(())],
        compiler_params=_cparams(("arbitrary",)),
        name="moe_combine",
    )(pos, x_mid, mod3, ys)


def _moe(x_mid, hext, plan, cnt, mod3, layer, params, dims, n_rows):
    tm = dims["tm_moe"]
    counts = cnt[0, :MOE_GROUPS].astype(jnp.int32)
    ends = jnp.cumsum((counts + tm - 1) // tm * tm)
    starts = ends - (counts + tm - 1) // tm * tm
    gid = plan[:, 1].astype(jnp.int32)
    pos = plan[:, 0].astype(jnp.int32)
    for g in range(MOE_GROUPS):
        pos = pos + jnp.where(gid == g, starts[g], 0)
    n_tiles = n_rows // tm + MOE_GROUPS
    tile_group = jnp.minimum(jnp.sum(jnp.arange(n_tiles)[:, None] * tm >= ends[None, :], axis=1),
                             MOE_GROUPS - 1).astype(jnp.int32)
    xs = _moe_scatter(pos, hext, n_rows, n_tiles * tm, tm)
    ys = _moe_ffn(tile_group, xs, layer, params, tm)
    return _moe_combine(pos, x_mid, ys, mod3, dims, n_rows, tm)


def _hi_lo(w):
    hi = w.astype(BF16)
    return hi, (w - hi.astype(F32)).astype(BF16)


def _layer_params(l, p):
    d = p["w_in"].shape[1]
    w_in = p["w_in"][l]
    o = 0
    cols = {}
    for name, size in (("z", 512), ("xbc", 1024), ("dt", 16), ("gq", 512), ("gk", 512), ("gv", 512),
                       ("g1", 32), ("gr", 512), ("lx", 512), ("lg", 512), ("aq", 512), ("ak", 256),
                       ("av", 256), ("mg", 4 * d)):
        cols[name] = w_in[:, o:o + size]
        o += size
    out = {}
    out["w_main"] = jnp.concatenate([cols[n] for n in ("mg", "xbc", "z", "gq", "gk", "gv", "gr", "lx", "lg",
                                                        "aq", "ak", "av")], axis=1).astype(BF16)
    zpad = jnp.zeros((d, 128 - 8 - GLA_RANK), F32)
    w_small = jnp.concatenate([cols["dt"][:, :8], cols["g1"][:, :GLA_RANK], zpad,
                               cols["dt"][:, 8:], cols["g1"][:, GLA_RANK:], zpad], axis=1)
    out["ws_hi"], out["ws_lo"] = _hi_lo(w_small)
    out["norm1"] = p["norm1"][l][None]
    out["norm2"] = p["norm2"][l][None]

    out["ssd_cw"] = p["ssd_conv_w"][l]
    out["ssd_cb"] = p["ssd_conv_b"][l][None]
    pad8 = lambda v: jnp.pad(v, ((0, 0), (0, 128 - SSD_HEADS)))
    brow = pad8(p["ssd_dt_bias"][l])
    arow = pad8(-jnp.exp(p["ssd_a_log"][l]))
    out["ssd_brow"], out["ssd_arow"] = brow[:, None, :], arow[:, None, :]
    head_of_lane = jnp.arange(SSD_INNER) // SSD_HEAD_DIM
    out["ssd_e"] = (jnp.arange(128)[:, None] == head_of_lane[None, :]).astype(BF16)
    out["ssd_dexp"] = jnp.repeat(p["ssd_d"][l], SSD_HEAD_DIM)[None]
    out["ssd_norm"] = p["ssd_norm"][l][None]

    g2 = jnp.zeros((2, 128, GLA_HEADS * GLA_DK), F32).at[:, SM_G1:SM_G1 + GLA_RANK].set(p["gla_g2"][l])
    out["gla_g2h"], out["gla_g2l"] = _hi_lo(g2)
    out["gla_gb"] = p["gla_gb"][l][:, None, :]
    out["gla_norm"] = p["gla_norm"][l][None]

    def pairs(w):
        w = w.reshape(2, LRU_BLOCKS // 2, 2, LRU_BLOCK, LRU_BLOCK)
        z = jnp.zeros_like(w[:, :, 0])
        top = jnp.concatenate([w[:, :, 0], z], axis=-1)
        bot = jnp.concatenate([z, w[:, :, 1]], axis=-1)
        return jnp.concatenate([top, bot], axis=-2)
    out["lru_wg"] = jnp.concatenate([pairs(p["lru_wa"][l]), pairs(p["lru_wx"][l])], axis=-1).astype(BF16)
    out["lru_cw"] = p["lru_conv_w"][l]
    out["lru_cb"] = p["lru_conv_b"][l][None]
    out["lru_ba"] = p["lru_ba"][l][:, None, :]
    out["lru_bx"] = p["lru_bx"][l][:, None, :]
    out["lru_lam"] = p["lru_lambda"][l][:, None, :]

    out["att_qnorm"] = p["att_qnorm"][l][None]
    out["att_knorm"] = p["att_knorm"][l][None]
    out["w_branch"] = p["w_branch"][l].astype(BF16)
    out["w_out"] = p["w_out"][l].astype(BF16)
    rw = jnp.concatenate([p["router_wg"][l], p["router_we"][l],
                          jnp.zeros((d, 128 - MOE_GROUPS - MOE_EXPERTS), F32)], axis=1)
    out["rw_hi"], out["rw_lo"] = _hi_lo(rw)
    out["rb"] = jnp.concatenate([p["router_bg"][l], p["router_be"][l],
                                 jnp.zeros((128 - MOE_GROUPS - MOE_EXPERTS,), F32)])[None]
    return out


def _rope_tables(l):
    f = ATT_HEAD_DIM // 4
    inv = ROPE_THETA ** (-jnp.arange(f, dtype=F32) / f)
    tpos = jnp.arange(l, dtype=jnp.int32)
    row = (tpos // GRID_W).astype(F32)[:, None] * inv
    col = (tpos % GRID_W).astype(F32)[:, None] * inv
    cos = jnp.concatenate([jnp.cos(row), jnp.cos(row), jnp.cos(col), jnp.cos(col)], axis=1)
    sin = jnp.concatenate([-jnp.sin(row), jnp.sin(row), -jnp.sin(col), jnp.sin(col)], axis=1)
    return cos, sin


def _pick(n, cands):
    for c in cands:
        if n % c == 0:
            return c
    raise ValueError(f"no tile size for {n}")


def kernel(x, c, ctx, c_ctx, ada_w, ada_b, norm1, norm2, w_in, ssd_conv_w, ssd_conv_b, ssd_dt_bias, ssd_a_log, ssd_d, ssd_norm, gla_g2, gla_gb, gla_norm, lru_conv_w, lru_conv_b, lru_wa, lru_ba, lru_wx, lru_bx, lru_lambda, att_qnorm, att_knorm, w_branch, w_out, router_wg, router_bg, router_we, router_be, exp_w1, exp_w3, exp_w2):
    nb, l, d = x.shape
    c_len = ctx.shape[1]
    depth = ada_w.shape[0]
    tl, tc = nb * l, nb * c_len
    assert l % LRU_R == 0 and c_len % LRU_R == 0 and nb + 1 <= 8 and l % GRID_W == 0
    params = dict(norm1=norm1, norm2=norm2, w_in=w_in, ssd_conv_w=ssd_conv_w, ssd_conv_b=ssd_conv_b,
                  ssd_dt_bias=ssd_dt_bias, ssd_a_log=ssd_a_log, ssd_d=ssd_d, ssd_norm=ssd_norm,
                  gla_g2=gla_g2, gla_gb=gla_gb, gla_norm=gla_norm, lru_conv_w=lru_conv_w,
                  lru_conv_b=lru_conv_b, lru_wa=lru_wa, lru_ba=lru_ba, lru_wx=lru_wx, lru_bx=lru_bx,
                  lru_lambda=lru_lambda, att_qnorm=att_qnorm, att_knorm=att_knorm, w_branch=w_branch,
                  w_out=w_out, router_wg=router_wg, router_bg=router_bg, router_we=router_we,
                  router_be=router_be, exp_w1=exp_w1, exp_w3=exp_w3, exp_w2=exp_w2)

    tile = _pick(math.gcd(l, tc), (1024, 512, 256))

    def modrow(i, tm):
        return jnp.where(i < tl // tm, i // (l // tm), nb)

    dims = dict(B=nb, L=l, C=c_len, modrow=modrow, tm_proj=tile, tm_merge=min(tile, 512), tm_moe=tile,
                tm_qk=min(tile, 512), tq=_pick(l, (512, 256)))

    cvec = jnp.zeros((8, d), F32).at[:nb].set(c).at[nb].set(c_ctx)
    mod_all = _modulation(cvec, ada_w, ada_b)
    cos, sin = _rope_tables(l)
    x_all = jnp.concatenate([x.reshape(tl, d), ctx.reshape(tc, d)], axis=0)

    for layer in range(depth):
        last = layer == depth - 1
        prm = _layer_params(layer, params)
        mod3 = mod_all[layer].reshape(8 * 6, 1, d)
        pm, sm = _in_projection(x_all, prm["norm1"], mod3, prm["w_main"], prm["ws_hi"], prm["ws_lo"], dims)
        ya = _ssd_mixer(pm, sm, prm, dims)
        yb = _gla_mixer(pm, sm, prm, dims)
        yc = _lru_mixer(pm, prm, dims)
        qn, kn = _qk_prep(pm, cos, sin, prm, dims)
        yd = _attention(qn, kn, pm, dims, latent=True)
        yd_ctx = yd if last else _attention(qn, kn, pm, dims, latent=False)
        n_rows = tl if last else tl + tc
        x_mid, hext, plan, cnt = _merge(x_all, ya, yb, yc, yd, yd_ctx, pm, mod3, prm, dims,
                                        n_rows // dims["tm_merge"])
        x_all = _moe(x_mid, hext, plan, cnt, mod3, layer, params, dims, n_rows)
    return x_all.reshape(nb, l, d)
```

```python
import functools
import math

import jax
import jax.numpy as jnp
from jax import lax
from jax.experimental import pallas as pl
from jax.experimental.pallas import tpu as pltpu

F32 = jnp.float32
BF16 = jnp.bfloat16

EPS = 1e-6
GRID_W = 64
ROPE_THETA = 10000.0

SSD_HEADS = 8
SSD_HEAD_DIM = 64
SSD_INNER = 512
SSD_GROUPS = 2
SSD_STATE = 128
SSD_XBC = 1024
CONV_W = 4
GLA_HEADS = 4
GLA_DK = 128
GLA_RANK = 16
GLA_GATE_NORM = 16.0
GLA_CHUNK = 64
LRU_W = 512
LRU_BLOCKS = 8
LRU_BLOCK = 64
LRU_C = 8.0
ATT_HEADS = 4
ATT_KV_HEADS = 2
ATT_HEAD_DIM = 128
MOE_GROUPS = 4
MOE_PER_GROUP = 4
MOE_EXPERTS = 16
MOE_FF = 512
BRANCH_W = 512

C_MG, C_XBC, C_Z, C_GQ, C_GK, C_GV, C_GR = 0, 4096, 5120, 5632, 6144, 6656, 7168
C_LX, C_LG, C_AQ, C_AK, C_AV, N_MAIN = 7680, 8192, 8704, 9216, 9472, 9728
N_SMALL = 256
SM_DT, SM_G1 = 0, 8

HALO = 16
SSD_Q = 128
GLA_R = 256
LRU_R = 256
LRU_SEG = LRU_R // 8

VMEM_LIMIT = 56 * 1024 * 1024

_NT = (((1,), (1,)), ((), ()))
_TN = (((0,), (0,)), ((), ()))


def _dot(a, b):
    return jnp.dot(a, b, preferred_element_type=F32)


def _dot_nt(a, b):
    return lax.dot_general(a, b, _NT, preferred_element_type=F32)


def _dot_tn(a, b):
    return lax.dot_general(a, b, _TN, preferred_element_type=F32)


def _split3(x):
    h = x.astype(BF16)
    r = x - h.astype(F32)
    m = r.astype(BF16)
    l = (r - m.astype(F32)).astype(BF16)
    return h, m, l


def _dot_mask_lhs(mask_bf16, x):
    h, m, l = _split3(x)
    return _dot(mask_bf16, h) + _dot(mask_bf16, m) + _dot(mask_bf16, l)


def _dot_mask_rhs(x, mask_bf16):
    h, m, l = _split3(x)
    return _dot(h, mask_bf16) + _dot(m, mask_bf16) + _dot(l, mask_bf16)


def _sigmoid(x):
    return 0.5 * jnp.tanh(0.5 * x) + 0.5


def _silu(x):
    return x * _sigmoid(x)


def _softplus(x):
    return jnp.maximum(x, 0.0) + jnp.log1p(jnp.exp(-jnp.abs(x)))


def _cparams(sem):
    return pltpu.CompilerParams(dimension_semantics=sem, vmem_limit_bytes=VMEM_LIMIT)


def _mod_kernel(c_ref, w_ref, b_ref, o_ref):
    s = _silu(c_ref[...])
    sh, sm, sl = _split3(s)
    w = w_ref[0]
    wh = w.astype(BF16)
    wl = (w - wh.astype(F32)).astype(BF16)
    o_ref[0] = _dot(sh, wh) + _dot(sm, wh) + _dot(sh, wl) + b_ref[0]


def _modulation(cvec, ada_w, ada_b):
    ld, d, n6 = ada_w.shape
    tn = 1536
    return pl.pallas_call(
        _mod_kernel,
        grid=(ld, n6 // tn),
        in_specs=[
            pl.BlockSpec((8, d), lambda l, j: (0, 0)),
            pl.BlockSpec((1, d, tn), lambda l, j: (l, 0, j)),
            pl.BlockSpec((1, 1, tn), lambda l, j: (l, 0, j)),
        ],
        out_specs=pl.BlockSpec((1, 8, tn), lambda l, j: (l, 0, j)),
        out_shape=jax.ShapeDtypeStruct((ld, 8, n6), F32),
        compiler_params=_cparams(("arbitrary", "arbitrary")),
        name="modulation",
    )(cvec, ada_w, ada_b.reshape(ld, 1, n6))


def _inproj_kernel(x_ref, g_ref, sh_ref, sc_ref, w_ref, wsh_ref, wsl_ref, p_ref, s_ref, h_ref):
    j = pl.program_id(1)

    @pl.when(j == 0)
    def _():
        x = x_ref[...]
        h = x * lax.rsqrt(jnp.mean(x * x, axis=-1, keepdims=True) + EPS) * g_ref[...]
        h = h * (1.0 + sc_ref[0]) + sh_ref[0]
        hh = h.astype(BF16)
        hl = (h - hh.astype(F32)).astype(BF16)
        h_ref[...] = hh
        wsh = wsh_ref[...]
        s_ref[...] = _dot(hh, wsh) + _dot(hl, wsh) + _dot(hh, wsl_ref[...])

    p_ref[...] = _dot(h_ref[...], w_ref[...]).astype(BF16)


def _in_projection(x_all, norm_g, mod3, w_main, ws_hi, ws_lo, dims):
    t, d = x_all.shape
    tm, tn = dims["tm_proj"], N_MAIN // 4
    modrow = dims["modrow"]
    return pl.pallas_call(
        _inproj_kernel,
        grid=(t // tm, N_MAIN // tn),
        in_specs=[
            pl.BlockSpec((tm, d), lambda i, j: (i, 0)),
            pl.BlockSpec((1, d), lambda i, j: (0, 0)),
            pl.BlockSpec((1, 1, d), lambda i, j: (modrow(i, tm) * 6 + 0, 0, 0)),
            pl.BlockSpec((1, 1, d), lambda i, j: (modrow(i, tm) * 6 + 1, 0, 0)),
            pl.BlockSpec((d, tn), lambda i, j: (0, j)),
            pl.BlockSpec((d, N_SMALL), lambda i, j: (0, 0)),
            pl.BlockSpec((d, N_SMALL), lambda i, j: (0, 0)),
        ],
        out_specs=[
            pl.BlockSpec((tm, tn), lambda i, j: (i, j)),
            pl.BlockSpec((tm, N_SMALL), lambda i, j: (i, 0)),
        ],
        out_shape=[
            jax.ShapeDtypeStruct((t, N_MAIN), BF16),
            jax.ShapeDtypeStruct((t, N_SMALL), F32),
        ],
        scratch_shapes=[pltpu.VMEM((tm, d), BF16)],
        compiler_params=_cparams(("arbitrary", "arbitrary")),
        name="in_projection",
    )(x_all, norm_g, mod3, mod3, w_main, ws_hi, ws_lo)


def _seq_pos(p, i, nctx, nlat):
    is_ctx = i < nctx
    k = jnp.where(is_ctx, jnp.where(p == 0, i, nctx - 1 - i),
                  jnp.where(p == 0, i - nctx, nlat - 1 - (i - nctx)))
    return k, jnp.where(is_ctx, nctx, nlat)


def _seq_chunk(b, p, i, nctx, nlat, nb):
    k, _ = _seq_pos(p, i, nctx, nlat)
    return jnp.where(i < nctx, nb * nlat + b * nctx + k, b * nlat + k)


def _seq_out_chunk(b, p, i, nctx, nlat, nb):
    return _seq_chunk(b, 1, jnp.where(p == 0, 0, i), nctx, nlat, nb)


def _fwd_slot(i, nctx, nlat):
    return jnp.where(i < nctx, nctx - 1 - i, nctx + nlat - 1 - (i - nctx))


def _conv4(prev_ref, cur_ref, next_ref, w, b, pv, nv, rows):
    xe = jnp.concatenate([prev_ref[...].astype(F32) * pv, cur_ref[...].astype(F32),
                          next_ref[...].astype(F32) * nv], axis=0)
    n = rows + 2 * HALO
    y = (pltpu.roll(xe, 2, 0) * w[0:1] + pltpu.roll(xe, 1, 0) * w[1:2] + xe * w[2:3]
         + pltpu.roll(xe, n - 1, 0) * w[3:4])
    return y[HALO:HALO + rows] + b


def _dir_mask(p, n):
    r = lax.broadcasted_iota(jnp.int32, (n, n), 0)
    c = lax.broadcasted_iota(jnp.int32, (n, n), 1)
    sgn = 1 - 2 * p
    return (r - c) * sgn >= 0


def _ssd_kernel(xc_ref, xp_ref, xn_ref, z_ref, sm_ref, cw_ref, cb_ref, brow_ref, arow_ref,
                e_ref, dexp_ref, nrm_ref, o_ref, yf_ref, ht_ref, xbc_ref, *, nctx, nlat):
    q = SSD_Q
    p = pl.program_id(1)
    i = pl.program_id(2)
    k, n = _seq_pos(p, i, nctx, nlat)
    pv = (k > 0).astype(F32)
    nv = (k < n - 1).astype(F32)

    @pl.when(i == 0)
    def _():
        ht_ref[...] = jnp.zeros_like(ht_ref)

    @pl.when(p == 0)
    def _():
        xbc_ref[i] = _silu(_conv4(xp_ref, xc_ref, xn_ref, cw_ref[...], cb_ref[...], pv, nv, q)).astype(BF16)

    xbc = xbc_ref[jnp.where(p == 0, i, _fwd_slot(i, nctx, nlat))]
    xs = xbc[:, :SSD_INNER].astype(F32)
    bmat = xbc[:, SSD_INNER:SSD_INNER + 256]
    cmat = xbc[:, SSD_INNER + 256:]

    mask = _dir_mask(p, q)
    tri = jnp.where(mask, 1.0, 0.0).astype(BF16)

    dt_c = _softplus(sm_ref[...] + brow_ref[0])
    dta_c = dt_c * arow_ref[0]
    acum_c = _dot_mask_lhs(tri, dta_c)
    alast_c = jnp.sum(dta_c, axis=0, keepdims=True)
    acum_r = acum_c.T

    e = e_ref[...]
    stack = jnp.concatenate([dt_c, jnp.exp(acum_c), dt_c * jnp.exp(alast_c - acum_c)], axis=0)
    ex = _dot(stack.astype(BF16), e)
    dt_e, ea_e, ds_e = ex[:q], ex[q:2 * q], ex[2 * q:]
    dec_e = _dot_mask_rhs(jnp.broadcast_to(jnp.exp(alast_c), (8, 128)), e)[0:1]

    dtx = (xs * dt_e).astype(BF16)
    dsx = (xs * ds_e).astype(BF16)
    lane = lax.broadcasted_iota(jnp.int32, (q, 128), 1)
    ydiag, yoff = [], []
    for g in range(SSD_GROUPS):
        bg = bmat[:, g * 128:(g + 1) * 128]
        cg = cmat[:, g * 128:(g + 1) * 128]
        cb = _dot_nt(cg, bg)
        htg = ht_ref[g]
        yoff.append(_dot(cg, htg.astype(BF16)))
        ht_ref[g] = htg * dec_e[:, g * 256:(g + 1) * 256] + _dot_tn(bg, dsx[:, g * 256:(g + 1) * 256])
        for pr in range(2):
            hd = g * 4 + pr * 2
            pair = dtx[:, hd * 64:hd * 64 + 128]
            outs = []
            for hh in (hd, hd + 1):
                seg = acum_c[:, hh:hh + 1] - acum_r[hh:hh + 1, :]
                lm = jnp.exp(jnp.where(mask, seg, -1e30))
                outs.append(_dot((cb * lm).astype(BF16), pair))
            ydiag.append(jnp.where(lane < 64, outs[0], outs[1]))
    y = jnp.concatenate(ydiag, axis=1) + jnp.concatenate(yoff, axis=1) * ea_e

    @pl.when(p == 0)
    def _():
        yf_ref[i] = y

    @pl.when(p == 1)
    def _():
        yt = (y + yf_ref[_fwd_slot(i, nctx, nlat)] + dexp_ref[...] * xs) * _silu(z_ref[...].astype(F32))
        yn = yt * lax.rsqrt(jnp.mean(yt * yt, axis=-1, keepdims=True) + EPS) * nrm_ref[...]
        o_ref[...] = yn.astype(BF16)


def _ssd_mixer(pm, sm, prm, dims):
    t = pm.shape[0]
    nb, nctx, nlat = dims["B"], dims["C"] // SSD_Q, dims["L"] // SSD_Q
    nhalo = t // HALO
    per = SSD_Q // HALO
    ck = functools.partial(_seq_chunk, nctx=nctx, nlat=nlat, nb=nb)
    oc = functools.partial(_seq_out_chunk, nctx=nctx, nlat=nlat, nb=nb)
    full = lambda shape: pl.BlockSpec(shape, lambda b, p, i: (0,) * len(shape))
    bydir = lambda shape: pl.BlockSpec((1,) + shape, lambda b, p, i: (p,) + (0,) * len(shape))
    return pl.pallas_call(
        functools.partial(_ssd_kernel, nctx=nctx, nlat=nlat),
        grid=(nb, 2, nctx + nlat),
        in_specs=[
            pl.BlockSpec((SSD_Q, SSD_XBC), lambda b, p, i: (ck(b, p, i), C_XBC // SSD_XBC)),
            pl.BlockSpec((HALO, SSD_XBC),
                         lambda b, p, i: (jnp.maximum(ck(b, p, i) * per - 1, 0), C_XBC // SSD_XBC)),
            pl.BlockSpec((HALO, SSD_XBC),
                         lambda b, p, i: (jnp.minimum(ck(b, p, i) * per + per, nhalo - 1), C_XBC // SSD_XBC)),
            pl.BlockSpec((SSD_Q, SSD_INNER), lambda b, p, i: (ck(b, p, i), C_Z // SSD_INNER)),
            pl.BlockSpec((SSD_Q, 128), lambda b, p, i: (ck(b, p, i), p)),
            full((CONV_W, SSD_XBC)), full((1, SSD_XBC)),
            bydir((1, 128)), bydir((1, 128)),
            full((128, SSD_INNER)), full((1, SSD_INNER)), full((1, SSD_INNER)),
        ],
        out_specs=pl.BlockSpec((SSD_Q, SSD_INNER), lambda b, p, i: (oc(b, p, i), 0)),
        out_shape=jax.ShapeDtypeStruct((t, SSD_INNER), BF16),
        scratch_shapes=[pltpu.VMEM((nctx + nlat, SSD_Q, SSD_INNER), F32),
                        pltpu.VMEM((SSD_GROUPS, SSD_STATE, 256), F32),
                        pltpu.VMEM((nctx + nlat, SSD_Q, SSD_XBC), BF16)],
        compiler_params=_cparams(("arbitrary", "arbitrary", "arbitrary")),
        name="ssd_mixer",
    )(pm, pm, pm, pm, sm, prm["ssd_cw"], prm["ssd_cb"], prm["ssd_brow"], prm["ssd_arow"],
      prm["ssd_e"], prm["ssd_dexp"], prm["ssd_norm"])


def _gla_kernel(q_ref, k_ref, v_ref, r_ref, sm_ref, g2h_ref, g2l_ref, gb_ref, nrm_ref,
                o_ref, yf_ref, st_ref, *, nctx, nlat):
    qc = GLA_CHUNK
    nsub = GLA_R // qc
    p = pl.program_id(1)
    i = pl.program_id(2)

    @pl.when(i == 0)
    def _():
        st_ref[...] = jnp.zeros_like(st_ref)

    r = GLA_R
    rr = lax.broadcasted_iota(jnp.int32, (r, r), 0)
    cc = lax.broadcasted_iota(jnp.int32, (r, r), 1)
    shift = qc.bit_length() - 1
    mask = (lax.shift_right_logical(rr, shift) == lax.shift_right_logical(cc, shift)) & ((rr - cc) * (1 - 2 * p) >= 0)
    tri = jnp.where(mask, 1.0, 0.0).astype(BF16)
    scale = GLA_DK ** -0.5

    sm = sm_ref[...]
    smh = sm.astype(BF16)
    sml = (sm - smh.astype(F32)).astype(BF16)
    g2h = g2h_ref[0]
    logit = _dot(smh, g2h) + _dot(sml, g2h) + _dot(smh, g2l_ref[0]) + gb_ref[0]
    g = -_softplus(-logit) * (1.0 / GLA_GATE_NORM)
    gc = _dot_mask_lhs(tri, g)
    glast = [jnp.sum(g[s * qc:(s + 1) * qc], axis=0, keepdims=True) for s in range(nsub)]
    bcast = lambda rows_: jnp.concatenate([jnp.broadcast_to(x, (qc, x.shape[1])) for x in rows_], axis=0)
    gref = bcast([gc[s * qc + qc // 2:s * qc + qc // 2 + 1] for s in range(nsub)])
    glast_f = bcast(glast)
    qf = q_ref[...].astype(F32) * scale
    kf = k_ref[...].astype(F32)
    vb = v_ref[...]
    qe = (qf * jnp.exp(gc - gref)).astype(BF16)
    ke = (kf * jnp.exp(gref - gc)).astype(BF16)
    qg = (qf * jnp.exp(gc)).astype(BF16)
    k2 = (kf * jnp.exp(glast_f - gc)).astype(BF16)
    o_intra = []
    for h in range(GLA_HEADS):
        hs = slice(h * GLA_DK, (h + 1) * GLA_DK)
        att = jnp.where(mask, _dot_nt(qe[:, hs], ke[:, hs]), 0.0).astype(BF16)
        o_intra.append(_dot(att, vb[:, hs]))

    def sweep(order, emit):
        st = [st_ref[h] for h in range(GLA_HEADS)]
        for s in order:
            rs = slice(s * qc, (s + 1) * qc)
            dec = jnp.exp(glast[s])
            outs = []
            for h in range(GLA_HEADS):
                hs = slice(h * GLA_DK, (h + 1) * GLA_DK)
                outs.append(o_intra[h][rs] + _dot_nt(qg[rs, hs], st[h].astype(BF16)))
                st[h] = st[h] * dec[:, hs] + _dot_tn(vb[rs, hs], k2[rs, hs])
            emit(rs, jnp.concatenate(outs, axis=1))
        for h in range(GLA_HEADS):
            st_ref[h] = st[h]

    @pl.when(p == 0)
    def _():
        def emit(rs, y):
            yf_ref[i, rs, :] = y
        sweep(range(nsub), emit)

    @pl.when(p == 1)
    def _():
        slot = _fwd_slot(i, nctx, nlat)
        nrm = nrm_ref[...]

        def emit(rs, y):
            yt = y + yf_ref[slot, rs, :]
            parts = []
            for h in range(GLA_HEADS):
                hs = slice(h * GLA_DK, (h + 1) * GLA_DK)
                yh = yt[:, hs]
                parts.append(yh * lax.rsqrt(jnp.mean(yh * yh, axis=-1, keepdims=True) + EPS) * nrm[:, hs])
            o_ref[rs, :] = (jnp.concatenate(parts, axis=1) * _silu(r_ref[rs, :].astype(F32))).astype(BF16)
        sweep(range(nsub - 1, -1, -1), emit)


def _gla_mixer(pm, sm, prm, dims):
    t = pm.shape[0]
    nb, nctx, nlat = dims["B"], dims["C"] // GLA_R, dims["L"] // GLA_R
    ck = functools.partial(_seq_chunk, nctx=nctx, nlat=nlat, nb=nb)
    oc = functools.partial(_seq_out_chunk, nctx=nctx, nlat=nlat, nb=nb)
    w = GLA_HEADS * GLA_DK
    col = lambda c: pl.BlockSpec((GLA_R, w), lambda b, p, i: (ck(b, p, i), c // w))
    bydir = lambda shape: pl.BlockSpec((1,) + shape, lambda b, p, i: (p,) + (0,) * len(shape))
    return pl.pallas_call(
        functools.partial(_gla_kernel, nctx=nctx, nlat=nlat),
        grid=(nb, 2, nctx + nlat),
        in_specs=[
            col(C_GQ), col(C_GK), col(C_GV), col(C_GR),
            pl.BlockSpec((GLA_R, 128), lambda b, p, i: (ck(b, p, i), p)),
            bydir((128, w)), bydir((128, w)), bydir((1, w)),
            pl.BlockSpec((1, w), lambda b, p, i: (0, 0)),
        ],
        out_specs=pl.BlockSpec((GLA_R, w), lambda b, p, i: (oc(b, p, i), 0)),
        out_shape=jax.ShapeDtypeStruct((t, w), BF16),
        scratch_shapes=[pltpu.VMEM((nctx + nlat, GLA_R, w), F32),
                        pltpu.VMEM((GLA_HEADS, GLA_DK, GLA_DK), F32)],
        compiler_params=_cparams(("arbitrary", "arbitrary", "arbitrary")),
        name="gla_mixer",
    )(pm, pm, pm, pm, sm, prm["gla_g2h"], prm["gla_g2l"], prm["gla_gb"], prm["gla_norm"])


def _lru_kernel(xc_ref, xp_ref, xn_ref, gl_ref, cw_ref, cb_ref, wg_ref, ba_ref, bx_ref, lam_ref,
                o_ref, yf_ref, h_ref, a_scr, v_scr, as_scr, hs_scr, *, nctx, nlat):
    r, seg = LRU_R, LRU_SEG
    p = pl.program_id(1)
    i = pl.program_id(2)
    k, n = _seq_pos(p, i, nctx, nlat)
    pv = (k > 0).astype(F32)
    nv = (k < n - 1).astype(F32)

    @pl.when(i == 0)
    def _():
        h_ref[...] = jnp.zeros_like(h_ref)

    u = _conv4(xp_ref, xc_ref, xn_ref, cw_ref[...], cb_ref[...], pv, nv, r)
    ub = u.astype(BF16)
    ra, ix = [], []
    for j in range(LRU_W // 128):
        gj = _dot(ub[:, j * 128:(j + 1) * 128], wg_ref[0, j])
        ra.append(gj[:, :128])
        ix.append(gj[:, 128:])
    rg = _sigmoid(jnp.concatenate(ra, axis=1) + ba_ref[0])
    ig = _sigmoid(jnp.concatenate(ix, axis=1) + bx_ref[0])
    log_a = (-LRU_C * _softplus(-lam_ref[0])) * rg
    a_all = jnp.exp(log_a)
    th = jnp.tanh(log_a)
    v_all = u * ig * jnp.sqrt(-2.0 * th / (1.0 - th))
    nslab = LRU_W // 128
    for j in range(nslab):
        a_scr[j] = a_all[:, j * 128:(j + 1) * 128]
        v_scr[j] = v_all[:, j * 128:(j + 1) * 128]

    def scan(order, seg_order):
        for j in range(nslab):
            ls = slice(j * 128, (j + 1) * 128)
            acc_a = jnp.ones((8, 128), F32)
            acc_h = jnp.zeros((8, 128), F32)
            for kk in order:
                ak = a_scr[j, pl.ds(kk, 8, stride=seg), :]
                vk = v_scr[j, pl.ds(kk, 8, stride=seg), :]
                acc_h = ak * acc_h + vk
                acc_a = ak * acc_a
                as_scr[j, kk * 8:(kk + 1) * 8, :] = acc_a
                hs_scr[j, kk * 8:(kk + 1) * 8, :] = acc_h
            carry = h_ref[0:1, ls]
            rows = [None] * 8
            for s in seg_order:
                rows[s] = carry
                carry = acc_a[s:s + 1, :] * carry + acc_h[s:s + 1, :]
            h_ref[0:1, ls] = carry
            cin = jnp.concatenate(rows, axis=0)
            for kk in order:
                ks = slice(kk * 8, (kk + 1) * 8)
                hs_scr[j, ks, :] = hs_scr[j, ks, :] + as_scr[j, ks, :] * cin

    @pl.when(p == 0)
    def _():
        scan(range(seg), range(8))

    @pl.when(p == 1)
    def _():
        scan(range(seg - 1, -1, -1), range(7, -1, -1))

    def natural(s):
        return jnp.concatenate([hs_scr[j, pl.ds(s, seg, stride=8), :] for j in range(nslab)], axis=1)

    @pl.when(p == 0)
    def _():
        for s in range(8):
            yf_ref[i, s * seg:(s + 1) * seg, :] = natural(s)

    @pl.when(p == 1)
    def _():
        slot = _fwd_slot(i, nctx, nlat)
        c0 = math.sqrt(2.0 / math.pi)
        for s in range(8):
            rs = slice(s * seg, (s + 1) * seg)
            gt = gl_ref[rs, :].astype(F32)
            gelu = 0.5 * gt * (1.0 + jnp.tanh(c0 * (gt + 0.044715 * (gt * gt * gt))))
            o_ref[rs, :] = ((natural(s) + yf_ref[slot, rs, :]) * gelu).astype(BF16)


def _lru_mixer(pm, prm, dims):
    t = pm.shape[0]
    nb, nctx, nlat = dims["B"], dims["C"] // LRU_R, dims["L"] // LRU_R
    nhalo = t // HALO
    per = LRU_R // HALO
    ck = functools.partial(_seq_chunk, nctx=nctx, nlat=nlat, nb=nb)
    oc = functools.partial(_seq_out_chunk, nctx=nctx, nlat=nlat, nb=nb)
    w = LRU_W
    full = lambda shape: pl.BlockSpec(shape, lambda b, p, i: (0,) * len(shape))
    bydir = lambda shape: pl.BlockSpec((1,) + shape, lambda b, p, i: (p,) + (0,) * len(shape))
    return pl.pallas_call(
        functools.partial(_lru_kernel, nctx=nctx, nlat=nlat),
        grid=(nb, 2, nctx + nlat),
        in_specs=[
            pl.BlockSpec((LRU_R, w), lambda b, p, i: (ck(b, p, i), C_LX // w)),
            pl.BlockSpec((HALO, w), lambda b, p, i: (jnp.maximum(ck(b, p, i) * per - 1, 0), C_LX // w)),
            pl.BlockSpec((HALO, w), lambda b, p, i: (jnp.minimum(ck(b, p, i) * per + per, nhalo - 1), C_LX // w)),
            pl.BlockSpec((LRU_R, w), lambda b, p, i: (ck(b, p, i), C_LG // w)),
            full((CONV_W, w)), full((1, w)),
            bydir((w // 128, 128, 256)), bydir((1, w)), bydir((1, w)), bydir((1, w)),
        ],
        out_specs=pl.BlockSpec((LRU_R, w), lambda b, p, i: (oc(b, p, i), 0)),
        out_shape=jax.ShapeDtypeStruct((t, w), BF16),
        scratch_shapes=[pltpu.VMEM((nctx + nlat, LRU_R, w), F32),
                        pltpu.VMEM((8, w), F32),
                        pltpu.VMEM((w // 128, LRU_R, 128), F32), pltpu.VMEM((w // 128, LRU_R, 128), F32),
                        pltpu.VMEM((w // 128, LRU_R, 128), F32), pltpu.VMEM((w // 128, LRU_R, 128), F32)],
        compiler_params=_cparams(("arbitrary", "arbitrary", "arbitrary")),
        name="lru_mixer",
    )(pm, pm, pm, pm, prm["lru_cw"], prm["lru_cb"], prm["lru_wg"], prm["lru_ba"], prm["lru_bx"],
      prm["lru_lam"])


def _qkprep_kernel(q_ref, k_ref, cos_ref, sin_ref, qn_ref, kn_ref, qo_ref, ko_ref, *, nlat_tiles):
    i = pl.program_id(0)
    is_ctx = i >= nlat_tiles
    cos = jnp.where(is_ctx, 1.0, cos_ref[...])
    sin = jnp.where(is_ctx, 0.0, sin_ref[...])
    lane = lax.broadcasted_iota(jnp.int32, cos.shape, 1)
    first = (lane % 64) < 32

    def prep(x, g, scale):
        xn = x * lax.rsqrt(jnp.mean(x * x, axis=-1, keepdims=True) + EPS) * g
        sw = jnp.where(first, pltpu.roll(xn, 96, 1), pltpu.roll(xn, 32, 1))
        return ((xn * cos + sw * sin) * scale).astype(BF16)

    qn, kn = qn_ref[...], kn_ref[...]
    for h in range(ATT_HEADS):
        hs = slice(h * ATT_HEAD_DIM, (h + 1) * ATT_HEAD_DIM)
        qo_ref[:, hs] = prep(q_ref[:, hs].astype(F32), qn, ATT_HEAD_DIM ** -0.5)
    for h in range(ATT_KV_HEADS):
        hs = slice(h * ATT_HEAD_DIM, (h + 1) * ATT_HEAD_DIM)
        ko_ref[:, hs] = prep(k_ref[:, hs].astype(F32), kn, 1.0)


def _qk_prep(pm, cos, sin, prm, dims):
    t = pm.shape[0]
    tm = dims["tm_qk"]
    nlat_tiles = dims["B"] * dims["L"] // tm
    per_seq = dims["L"] // tm
    return pl.pallas_call(
        functools.partial(_qkprep_kernel, nlat_tiles=nlat_tiles),
        grid=(t // tm,),
        in_specs=[
            pl.BlockSpec((tm, 512), lambda i: (i, C_AQ // 512)),
            pl.BlockSpec((tm, 256), lambda i: (i, C_AK // 256)),
            pl.BlockSpec((tm, 128), lambda i: (i % per_seq, 0)),
            pl.BlockSpec((tm, 128), lambda i: (i % per_seq, 0)),
            pl.BlockSpec((1, 128), lambda i: (0, 0)),
            pl.BlockSpec((1, 128), lambda i: (0, 0)),
        ],
        out_specs=[pl.BlockSpec((tm, 512), lambda i: (i, 0)), pl.BlockSpec((tm, 256), lambda i: (i, 0))],
        out_shape=[jax.ShapeDtypeStruct((t, 512), BF16), jax.ShapeDtypeStruct((t, 256), BF16)],
        compiler_params=_cparams(("arbitrary",)),
        name="qk_prep",
    )(pm, pm, cos, sin, prm["att_qnorm"], prm["att_knorm"])


def _attn_kernel(q_ref, kc_ref, vc_ref, *rest, tk, nlat_chunks):
    if nlat_chunks:
        kl_ref, vl_ref, o_ref = rest
    else:
        (o_ref,) = rest
    tq = q_ref.shape[0]
    dh = ATT_HEAD_DIM
    q = jnp.concatenate([q_ref[:, :dh], q_ref[:, dh:]], axis=0)

    chunks = [(kc_ref, vc_ref, slice(None))]
    chunks += [(kl_ref, vl_ref, slice(c * tk, (c + 1) * tk)) for c in range(nlat_chunks)]
    m = l = acc = None
    s_next = _dot_nt(q, kc_ref[...])
    for c, (_, v_ref, rows) in enumerate(chunks):
        s = s_next
        if c + 1 < len(chunks):
            k_ref, _, nrows = chunks[c + 1]
            s_next = _dot_nt(q, k_ref[nrows, :])
        mx = jnp.max(s, axis=-1, keepdims=True)
        if m is None:
            m = mx
            pmat = jnp.exp(s - m)
            l = jnp.sum(pmat, axis=-1, keepdims=True)
            acc = _dot(pmat.astype(BF16), v_ref[rows, :])
        else:
            m_new = jnp.maximum(m, mx)
            alpha = jnp.exp(m - m_new)
            pmat = jnp.exp(s - m_new)
            l = alpha * l + jnp.sum(pmat, axis=-1, keepdims=True)
            acc = alpha * acc + _dot(pmat.astype(BF16), v_ref[rows, :])
            m = m_new
    o = acc * (1.0 / l)
    o_ref[...] = jnp.concatenate([o[:tq], o[tq:]], axis=1).astype(BF16)


def _attention(qn, kn, pm, dims, latent):
    nb, l, c = dims["B"], dims["L"], dims["C"]
    dh = ATT_HEAD_DIM
    tq = dims["tq"] if latent else c
    nq = (l if latent else c) // tq
    ctx_blk = nb * l // c
    qrow = (lambda b, i: b * nq + i) if latent else (lambda b, i: ctx_blk + b)
    av = C_AV // dh
    in_specs = [
        pl.BlockSpec((tq, 2 * dh), lambda b, g, i: (qrow(b, i), g)),
        pl.BlockSpec((c, dh), lambda b, g, i: (ctx_blk + b, g)),
        pl.BlockSpec((c, dh), lambda b, g, i: (ctx_blk + b, av + g)),
    ]
    args = [qn, kn, pm]
    if latent:
        in_specs += [pl.BlockSpec((l, dh), lambda b, g, i: (b, g)),
                     pl.BlockSpec((l, dh), lambda b, g, i: (b, av + g))]
        args += [kn, pm]
    tk = _pick(l, (1024, 512, 256))
    return pl.pallas_call(
        functools.partial(_attn_kernel, tk=tk, nlat_chunks=(l // tk if latent else 0)),
        grid=(nb, ATT_KV_HEADS, nq),
        in_specs=in_specs,
        out_specs=pl.BlockSpec((tq, 2 * dh), lambda b, g, i: (b * nq + i, g)),
        out_shape=jax.ShapeDtypeStruct((nb * (l if latent else c), ATT_HEADS * dh), BF16),
        compiler_params=_cparams(("arbitrary", "arbitrary", "arbitrary")),
        name="attention_latent" if latent else "attention_context",
    )(*args)


def _route_weights(lg):
    lane = lax.broadcasted_iota(jnp.int32, lg.shape, 1)
    neg = -1e30
    is_g = lane < MOE_GROUPS
    gl = jnp.where(is_g, lg, neg)
    gmax = jnp.max(gl, axis=-1, keepdims=True)
    gsum = jnp.sum(jnp.where(is_g, jnp.exp(gl - gmax), 0.0), axis=-1, keepdims=True)
    gi = jnp.min(jnp.where(is_g & (gl == gmax), lane, 1 << 20), axis=-1, keepdims=True)
    pg_sel = 1.0 / gsum
    lo = MOE_GROUPS + gi * MOE_PER_GROUP
    in_grp = (lane >= lo) & (lane < lo + MOE_PER_GROUP)
    el = jnp.where(in_grp, lg, neg)
    emax = jnp.max(el, axis=-1, keepdims=True)
    ex = jnp.where(in_grp, jnp.exp(el - emax), 0.0)
    pe = ex / jnp.sum(ex, axis=-1, keepdims=True)
    pe_m = jnp.where(in_grp, pe, -1.0)
    v1 = jnp.max(pe_m, axis=-1, keepdims=True)
    i1 = jnp.min(jnp.where(pe_m == v1, lane, 1 << 20), axis=-1, keepdims=True)
    pe_m2 = jnp.where(lane == i1, -1.0, pe_m)
    v2 = jnp.max(pe_m2, axis=-1, keepdims=True)
    i2 = jnp.min(jnp.where(pe_m2 == v2, lane, 1 << 20), axis=-1, keepdims=True)
    tot = v1 + v2
    w = jnp.where(lane == i1, v1 / tot, 0.0) + jnp.where(lane == i2, v2 / tot, 0.0)
    return w * pg_sel, gi


def _merge_kernel(x_ref, ya_ref, yb_ref, yc_ref, ydl_ref, ydc_ref, mg_ref, g1_ref, sh_ref, sc_ref, n2_ref,
                  wb_ref, wo_ref, rwh_ref, rwl_ref, rb_ref, xo_ref, h_ref, plan_ref, cnt_ref, run_ref,
                  *, nlat_tiles):
    i = pl.program_id(0)
    tm, d = x_ref.shape

    @pl.when(i == 0)
    def _():
        run_ref[...] = jnp.zeros_like(run_ref)

    yd = jnp.where(i < nlat_tiles, ydl_ref[...], ydc_ref[...])
    acc = None
    for nbr, y in enumerate((ya_ref[...], yb_ref[...], yc_ref[...], yd)):
        gate = _sigmoid(mg_ref[:, nbr * d:(nbr + 1) * d].astype(F32))
        term = gate * _dot(y, wb_ref[nbr])
        acc = term if acc is None else acc + term
    xn = x_ref[...] + g1_ref[0] * _dot(acc.astype(BF16), wo_ref[...])
    xo_ref[...] = xn
    h = xn * lax.rsqrt(jnp.mean(xn * xn, axis=-1, keepdims=True) + EPS) * n2_ref[...]
    h = h * (1.0 + sc_ref[0]) + sh_ref[0]
    hh = h.astype(BF16)
    hl = (h - hh.astype(F32)).astype(BF16)
    rwh = rwh_ref[...]
    lg = _dot(hh, rwh) + _dot(hl, rwh) + _dot(hh, rwl_ref[...]) + rb_ref[...]
    rw, gi = _route_weights(lg)
    h_ref[:, :d] = h
    h_ref[:, d:] = rw

    lane = lax.broadcasted_iota(jnp.int32, (tm, 128), 1)
    onehot = jnp.where(lane == gi, 1.0, 0.0)
    r = lax.broadcasted_iota(jnp.int32, (tm, tm), 0)
    c = lax.broadcasted_iota(jnp.int32, (tm, tm), 1)
    before = jnp.where(c < r, 1.0, 0.0).astype(BF16)
    run = run_ref[0:1, :]
    rank = jnp.sum(onehot * (_dot(before, onehot.astype(BF16)) + run), axis=-1, keepdims=True)
    plan_ref[...] = jnp.where(lane == 0, rank, jnp.where(lane == 1, gi.astype(F32), 0.0))
    run = run + jnp.sum(onehot, axis=0, keepdims=True)
    run_ref[0:1, :] = run
    cnt_ref[...] = jnp.broadcast_to(run, cnt_ref.shape)


def _merge(x_all, ya, yb, yc, yd_lat, yd_ctx, pm, mod3, prm, dims, n_tiles):
    d = x_all.shape[1]
    tm = dims["tm_merge"]
    rows = n_tiles * tm
    nlat_tiles = dims["B"] * dims["L"] // tm
    modrow = dims["modrow"]
    row = lambda shape: pl.BlockSpec(shape, lambda i: (i, 0))
    full = lambda shape: pl.BlockSpec(shape, lambda i: (0,) * len(shape))
    modspec = lambda comp: pl.BlockSpec((1, 1, d), lambda i: (modrow(i, tm) * 6 + comp, 0, 0))
    return pl.pallas_call(
        functools.partial(_merge_kernel, nlat_tiles=nlat_tiles),
        grid=(n_tiles,),
        in_specs=[
            row((tm, d)), row((tm, BRANCH_W)), row((tm, BRANCH_W)), row((tm, BRANCH_W)),
            pl.BlockSpec((tm, BRANCH_W), lambda i: (jnp.minimum(i, nlat_tiles - 1), 0)),
            pl.BlockSpec((tm, BRANCH_W), lambda i: (jnp.maximum(i - nlat_tiles, 0), 0)),
            pl.BlockSpec((tm, 4 * d), lambda i: (i, C_MG // (4 * d))),
            modspec(2), modspec(3), modspec(4), full((1, d)),
            full((4, BRANCH_W, d)), full((d, d)), full((d, 128)), full((d, 128)), full((1, 128)),
        ],
        out_specs=[row((tm, d)), row((tm, d + 128)), row((tm, 128)), full((8, 128))],
        out_shape=[jax.ShapeDtypeStruct((rows, d), F32), jax.ShapeDtypeStruct((rows, d + 128), F32),
                   jax.ShapeDtypeStruct((rows, 128), F32), jax.ShapeDtypeStruct((8, 128), F32)],
        scratch_shapes=[pltpu.VMEM((8, 128), F32)],
        compiler_params=_cparams(("arbitrary",)),
        name="merge",
    )(x_all, ya, yb, yc, yd_lat, yd_ctx, pm, mod3, mod3, mod3, prm["norm2"], prm["w_branch"], prm["w_out"],
      prm["rw_hi"], prm["rw_lo"], prm["rb"])


DMA_UNROLL = 8


def _row_copies(n, make):
    def issue(blk, carry):
        for u in range(DMA_UNROLL):
            make(blk * DMA_UNROLL + u).start()
        return carry
    lax.fori_loop(0, n // DMA_UNROLL, issue, 0)

    def drain(blk, carry):
        for u in range(DMA_UNROLL):
            make(0).wait()
        return carry
    lax.fori_loop(0, n // DMA_UNROLL, drain, 0)


def _scatter_kernel(pos_ref, ztile_ref, src_ref, dst_ref, zero_ref, sem, zsem, *, tm):
    @pl.when(pl.program_id(0) == 0)
    def _():
        zero_ref[...] = jnp.zeros_like(zero_ref)
        for k in range(ztile_ref.shape[0]):
            cp = pltpu.make_async_copy(
                zero_ref, dst_ref.at[pl.ds(pl.multiple_of(ztile_ref[k] * tm, tm), tm)], zsem)
            cp.start()
            cp.wait()

    def make(r):
        return pltpu.make_async_copy(src_ref.at[pl.ds(r, 1)], dst_ref.at[pl.ds(pos_ref[r], 1)], sem)
    _row_copies(tm, make)


def _moe_scatter(pos, ztile, hext, n_rows, sorted_rows, tm):
    w = hext.shape[1]
    return pl.pallas_call(
        functools.partial(_scatter_kernel, tm=tm),
        grid=(n_rows // tm,),
        in_specs=[pl.BlockSpec((tm,), lambda i: (i,), memory_space=pltpu.SMEM),
                  pl.BlockSpec(memory_space=pltpu.SMEM),
                  pl.BlockSpec((tm, w), lambda i: (i, 0))],
        out_specs=pl.BlockSpec(memory_space=pl.ANY),
        out_shape=jax.ShapeDtypeStruct((sorted_rows, w), F32),
        scratch_shapes=[pltpu.VMEM((tm, w), F32), pltpu.SemaphoreType.DMA(()), pltpu.SemaphoreType.DMA(())],
        compiler_params=_cparams(("arbitrary",)),
        name="moe_scatter",
    )(pos, ztile, hext)


def _moe_ffn_kernel(tg_ref, h_ref, w1_ref, w3_ref, w2_ref, o_ref, hb_ref):
    i = pl.program_id(0)
    j = pl.program_id(1)
    d = o_ref.shape[1]

    @pl.when(j == 0)
    def _():
        hb_ref[...] = h_ref[:, :d].astype(BF16)
        o_ref[...] = jnp.zeros_like(o_ref)

    h = hb_ref[...]
    a = _silu(_dot(h, w1_ref[0].astype(BF16))) * _dot(h, w3_ref[0].astype(BF16))
    y = _dot(a.astype(BF16), w2_ref[0].astype(BF16))
    rw = h_ref[:, d:]
    lane = lax.broadcasted_iota(jnp.int32, rw.shape, 1)
    e = tg_ref[i] * MOE_PER_GROUP + j
    we = jnp.sum(jnp.where(lane == e + MOE_GROUPS, rw, 0.0), axis=-1, keepdims=True)
    o_ref[...] += we * y


def _moe_ffn(tile_group, xs, layer, params, tm):
    rows, w = xs.shape
    d = w - 128
    stacked = lambda a: a.reshape((a.shape[0] * a.shape[1],) + a.shape[2:])
    expert = lambda i, j, tg: (layer * MOE_EXPERTS + tg[i] * MOE_PER_GROUP + j, 0, 0)
    return pl.pallas_call(
        _moe_ffn_kernel,
        grid_spec=pltpu.PrefetchScalarGridSpec(
            num_scalar_prefetch=1,
            grid=(rows // tm, MOE_PER_GROUP),
            in_specs=[
                pl.BlockSpec((tm, w), lambda i, j, tg: (i, 0)),
                pl.BlockSpec((1, d, MOE_FF), expert),
                pl.BlockSpec((1, d, MOE_FF), expert),
                pl.BlockSpec((1, MOE_FF, d), expert),
            ],
            out_specs=pl.BlockSpec((tm, d), lambda i, j, tg: (i, 0)),
            scratch_shapes=[pltpu.VMEM((tm, d), BF16)],
        ),
        out_shape=jax.ShapeDtypeStruct((rows, d), F32),
        compiler_params=_cparams(("arbitrary", "arbitrary")),
        name="moe_ffn",
    )(tile_group, xs, stacked(params["exp_w1"]), stacked(params["exp_w3"]), stacked(params["exp_w2"]))


def _combine_kernel(pos_ref, x_ref, g2_ref, ys_ref, o_ref, buf_ref, sem):
    tm = x_ref.shape[0]

    def make(r):
        return pltpu.make_async_copy(ys_ref.at[pl.ds(pos_ref[r], 1)], buf_ref.at[pl.ds(r, 1)], sem)
    _row_copies(tm, make)
    o_ref[...] = x_ref[...] + g2_ref[0] * buf_ref[...]


def _moe_combine(pos, x_mid, ys, mod3, dims, n_rows, tm):
    d = x_mid.shape[1]
    modrow = dims["modrow"]
    return pl.pallas_call(
        _combine_kernel,
        grid=(n_rows // tm,),
        in_specs=[pl.BlockSpec((tm,), lambda i: (i,), memory_space=pltpu.SMEM),
                  pl.BlockSpec((tm, d), lambda i: (i, 0)),
                  pl.BlockSpec((1, 1, d), lambda i: (modrow(i, tm) * 6 + 5, 0, 0)),
                  pl.BlockSpec(memory_space=pl.ANY)],
        out_specs=pl.BlockSpec((tm, d), lambda i: (i, 0)),
        out_shape=jax.ShapeDtypeStruct((n_rows, d), F32),
        scratch_shapes=[pltpu.VMEM((tm, d), F32), pltpu.SemaphoreType.DMA(())],
        compiler_params=_cparams(("arbitrary",)),
        name="moe_combine",
    )(pos, x_mid, mod3, ys)


def _moe(x_mid, hext, plan, cnt, mod3, layer, params, dims, n_rows):
    tm = dims["tm_moe"]
    counts = cnt[0, :MOE_GROUPS].astype(jnp.int32)
    ends = jnp.cumsum((counts + tm - 1) // tm * tm)
    starts = ends - (counts + tm - 1) // tm * tm
    gid = plan[:, 1].astype(jnp.int32)
    pos = plan[:, 0].astype(jnp.int32)
    for g in range(MOE_GROUPS):
        pos = pos + jnp.where(gid == g, starts[g], 0)
    n_tiles = n_rows // tm + MOE_GROUPS
    tile_group = jnp.minimum(jnp.sum(jnp.arange(n_tiles)[:, None] * tm >= ends[None, :], axis=1),
                             MOE_GROUPS - 1).astype(jnp.int32)
    ztile = jnp.concatenate([jnp.maximum(ends // tm - 1, 0),
                             jnp.arange(n_tiles - MOE_GROUPS, n_tiles)]).astype(jnp.int32)
    xs = _moe_scatter(pos, ztile, hext, n_rows, n_tiles * tm, tm)
    ys = _moe_ffn(tile_group, xs, layer, params, tm)
    return _moe_combine(pos, x_mid, ys, mod3, dims, n_rows, tm)


def _hi_lo(w):
    hi = w.astype(BF16)
    return hi, (w - hi.astype(F32)).astype(BF16)


def _layer_params(l, p):
    d = p["w_in"].shape[1]
    w_in = p["w_in"][l]
    o = 0
    cols = {}
    for name, size in (("z", 512), ("xbc", 1024), ("dt", 16), ("gq", 512), ("gk", 512), ("gv", 512),
                       ("g1", 32), ("gr", 512), ("lx", 512), ("lg", 512), ("aq", 512), ("ak", 256),
                       ("av", 256), ("mg", 4 * d)):
        cols[name] = w_in[:, o:o + size]
        o += size
    out = {}
    out["w_main"] = jnp.concatenate([cols[n] for n in ("mg", "xbc", "z", "gq", "gk", "gv", "gr", "lx", "lg",
                                                        "aq", "ak", "av")], axis=1).astype(BF16)
    zpad = jnp.zeros((d, 128 - 8 - GLA_RANK), F32)
    w_small = jnp.concatenate([cols["dt"][:, :8], cols["g1"][:, :GLA_RANK], zpad,
                               cols["dt"][:, 8:], cols["g1"][:, GLA_RANK:], zpad], axis=1)
    out["ws_hi"], out["ws_lo"] = _hi_lo(w_small)
    out["norm1"] = p["norm1"][l][None]
    out["norm2"] = p["norm2"][l][None]

    out["ssd_cw"] = p["ssd_conv_w"][l]
    out["ssd_cb"] = p["ssd_conv_b"][l][None]
    pad8 = lambda v: jnp.pad(v, ((0, 0), (0, 128 - SSD_HEADS)))
    brow = pad8(p["ssd_dt_bias"][l])
    arow = pad8(-jnp.exp(p["ssd_a_log"][l]))
    out["ssd_brow"], out["ssd_arow"] = brow[:, None, :], arow[:, None, :]
    head_of_lane = jnp.arange(SSD_INNER) // SSD_HEAD_DIM
    out["ssd_e"] = (jnp.arange(128)[:, None] == head_of_lane[None, :]).astype(BF16)
    out["ssd_dexp"] = jnp.repeat(p["ssd_d"][l], SSD_HEAD_DIM)[None]
    out["ssd_norm"] = p["ssd_norm"][l][None]

    g2 = jnp.zeros((2, 128, GLA_HEADS * GLA_DK), F32).at[:, SM_G1:SM_G1 + GLA_RANK].set(p["gla_g2"][l])
    out["gla_g2h"], out["gla_g2l"] = _hi_lo(g2)
    out["gla_gb"] = p["gla_gb"][l][:, None, :]
    out["gla_norm"] = p["gla_norm"][l][None]

    def pairs(w):
        w = w.reshape(2, LRU_BLOCKS // 2, 2, LRU_BLOCK, LRU_BLOCK)
        z = jnp.zeros_like(w[:, :, 0])
        top = jnp.concatenate([w[:, :, 0], z], axis=-1)
        bot = jnp.concatenate([z, w[:, :, 1]], axis=-1)
        return jnp.concatenate([top, bot], axis=-2)
    out["lru_wg"] = jnp.concatenate([pairs(p["lru_wa"][l]), pairs(p["lru_wx"][l])], axis=-1).astype(BF16)
    out["lru_cw"] = p["lru_conv_w"][l]
    out["lru_cb"] = p["lru_conv_b"][l][None]
    out["lru_ba"] = p["lru_ba"][l][:, None, :]
    out["lru_bx"] = p["lru_bx"][l][:, None, :]
    out["lru_lam"] = p["lru_lambda"][l][:, None, :]

    out["att_qnorm"] = p["att_qnorm"][l][None]
    out["att_knorm"] = p["att_knorm"][l][None]
    out["w_branch"] = p["w_branch"][l].astype(BF16)
    out["w_out"] = p["w_out"][l].astype(BF16)
    rw = jnp.concatenate([p["router_wg"][l], p["router_we"][l],
                          jnp.zeros((d, 128 - MOE_GROUPS - MOE_EXPERTS), F32)], axis=1)
    out["rw_hi"], out["rw_lo"] = _hi_lo(rw)
    out["rb"] = jnp.concatenate([p["router_bg"][l], p["router_be"][l],
                                 jnp.zeros((128 - MOE_GROUPS - MOE_EXPERTS,), F32)])[None]
    return out


def _rope_tables(l):
    f = ATT_HEAD_DIM // 4
    inv = ROPE_THETA ** (-jnp.arange(f, dtype=F32) / f)
    tpos = jnp.arange(l, dtype=jnp.int32)
    row = (tpos // GRID_W).astype(F32)[:, None] * inv
    col = (tpos % GRID_W).astype(F32)[:, None] * inv
    cos = jnp.concatenate([jnp.cos(row), jnp.cos(row), jnp.cos(col), jnp.cos(col)], axis=1)
    sin = jnp.concatenate([-jnp.sin(row), jnp.sin(row), -jnp.sin(col), jnp.sin(col)], axis=1)
    return cos, sin


def _pick(n, cands):
    for c in cands:
        if n % c == 0:
            return c
    raise ValueError(f"no tile size for {n}")


def kernel(x, c, ctx, c_ctx, ada_w, ada_b, norm1, norm2, w_in, ssd_conv_w, ssd_conv_b, ssd_dt_bias, ssd_a_log, ssd_d, ssd_norm, gla_g2, gla_gb, gla_norm, lru_conv_w, lru_conv_b, lru_wa, lru_ba, lru_wx, lru_bx, lru_lambda, att_qnorm, att_knorm, w_branch, w_out, router_wg, router_bg, router_we, router_be, exp_w1, exp_w3, exp_w2):
    nb, l, d = x.shape
    c_len = ctx.shape[1]
    depth = ada_w.shape[0]
    tl, tc = nb * l, nb * c_len
    assert l % LRU_R == 0 and c_len % LRU_R == 0 and nb + 1 <= 8 and l % GRID_W == 0
    params = dict(norm1=norm1, norm2=norm2, w_in=w_in, ssd_conv_w=ssd_conv_w, ssd_conv_b=ssd_conv_b,
                  ssd_dt_bias=ssd_dt_bias, ssd_a_log=ssd_a_log, ssd_d=ssd_d, ssd_norm=ssd_norm,
                  gla_g2=gla_g2, gla_gb=gla_gb, gla_norm=gla_norm, lru_conv_w=lru_conv_w,
                  lru_conv_b=lru_conv_b, lru_wa=lru_wa, lru_ba=lru_ba, lru_wx=lru_wx, lru_bx=lru_bx,
                  lru_lambda=lru_lambda, att_qnorm=att_qnorm, att_knorm=att_knorm, w_branch=w_branch,
                  w_out=w_out, router_wg=router_wg, router_bg=router_bg, router_we=router_we,
                  router_be=router_be, exp_w1=exp_w1, exp_w3=exp_w3, exp_w2=exp_w2)

    tile = _pick(math.gcd(l, tc), (1024, 512, 256))

    def modrow(i, tm):
        return jnp.where(i < tl // tm, i // (l // tm), nb)

    dims = dict(B=nb, L=l, C=c_len, modrow=modrow, tm_proj=tile, tm_merge=min(tile, 512), tm_moe=tile,
                tm_qk=min(tile, 512), tq=_pick(l, (512, 256)))

    cvec = jnp.zeros((8, d), F32).at[:nb].set(c).at[nb].set(c_ctx)
    mod_all = _modulation(cvec, ada_w, ada_b)
    cos, sin = _rope_tables(l)
    x_all = jnp.concatenate([x.reshape(tl, d), ctx.reshape(tc, d)], axis=0)

    for layer in range(depth):
        last = layer == depth - 1
        prm = _layer_params(layer, params)
        mod3 = mod_all[layer].reshape(8 * 6, 1, d)
        pm, sm = _in_projection(x_all, prm["norm1"], mod3, prm["w_main"], prm["ws_hi"], prm["ws_lo"], dims)
        ya = _ssd_mixer(pm, sm, prm, dims)
        yb = _gla_mixer(pm, sm, prm, dims)
        yc = _lru_mixer(pm, prm, dims)
        qn, kn = _qk_prep(pm, cos, sin, prm, dims)
        yd = _attention(qn, kn, pm, dims, latent=True)
        yd_ctx = yd if last else _attention(qn, kn, pm, dims, latent=False)
        n_rows = tl if last else tl + tc
        x_mid, hext, plan, cnt = _merge(x_all, ya, yb, yc, yd, yd_ctx, pm, mod3, prm, dims,
                                        n_rows // dims["tm_merge"])
        x_all = _moe(x_mid, hext, plan, cnt, mod3, layer, params, dims, n_rows)
    return x_all.reshape(nb, l, d)
```

```python
import functools
import math

import jax
import jax.numpy as jnp
from jax import lax
from jax.experimental import pallas as pl
from jax.experimental.pallas import tpu as pltpu

F32 = jnp.float32
BF16 = jnp.bfloat16

EPS = 1e-6
GRID_W = 64
ROPE_THETA = 10000.0

SSD_HEADS = 8
SSD_HEAD_DIM = 64
SSD_INNER = 512
SSD_GROUPS = 2
SSD_STATE = 128
SSD_XBC = 1024
CONV_W = 4
GLA_HEADS = 4
GLA_DK = 128
GLA_RANK = 16
GLA_GATE_NORM = 16.0
GLA_CHUNK = 64
LRU_W = 512
LRU_BLOCKS = 8
LRU_BLOCK = 64
LRU_C = 8.0
ATT_HEADS = 4
ATT_KV_HEADS = 2
ATT_HEAD_DIM = 128
MOE_GROUPS = 4
MOE_PER_GROUP = 4
MOE_EXPERTS = 16
MOE_FF = 512
BRANCH_W = 512

C_MG, C_XBC, C_Z, C_GQ, C_GK, C_GV, C_GR = 0, 4096, 5120, 5632, 6144, 6656, 7168
C_LX, C_LG, C_AQ, C_AK, C_AV, N_MAIN = 7680, 8192, 8704, 9216, 9472, 9728
N_SMALL = 256
SM_DT, SM_G1 = 0, 8

HALO = 16
SSD_Q = 128
GLA_R = 256
LRU_R = 256
LRU_SEG = LRU_R // 8

VMEM_LIMIT = 56 * 1024 * 1024

_NT = (((1,), (1,)), ((), ()))
_TN = (((0,), (0,)), ((), ()))


def _dot(a, b):
    return jnp.dot(a, b, preferred_element_type=F32)


def _dot_nt(a, b):
    return lax.dot_general(a, b, _NT, preferred_element_type=F32)


def _dot_tn(a, b):
    return lax.dot_general(a, b, _TN, preferred_element_type=F32)


def _split3(x):
    h = x.astype(BF16)
    r = x - h.astype(F32)
    m = r.astype(BF16)
    l = (r - m.astype(F32)).astype(BF16)
    return h, m, l


def _dot_mask_lhs(mask_bf16, x):
    h, m, l = _split3(x)
    return _dot(mask_bf16, h) + _dot(mask_bf16, m) + _dot(mask_bf16, l)


def _dot_mask_rhs(x, mask_bf16):
    h, m, l = _split3(x)
    return _dot(h, mask_bf16) + _dot(m, mask_bf16) + _dot(l, mask_bf16)


def _sigmoid(x):
    return 0.5 * jnp.tanh(0.5 * x) + 0.5


def _silu(x):
    return x * _sigmoid(x)


def _softplus(x):
    return jnp.maximum(x, 0.0) + jnp.log1p(jnp.exp(-jnp.abs(x)))


def _cparams(sem):
    return pltpu.CompilerParams(dimension_semantics=sem, vmem_limit_bytes=VMEM_LIMIT)


def _mod_kernel(c_ref, w_ref, b_ref, o_ref):
    s = _silu(c_ref[...])
    sh, sm, sl = _split3(s)
    w = w_ref[0]
    wh = w.astype(BF16)
    wl = (w - wh.astype(F32)).astype(BF16)
    o_ref[0] = _dot(sh, wh) + _dot(sm, wh) + _dot(sh, wl) + b_ref[0]


def _modulation(cvec, ada_w, ada_b):
    ld, d, n6 = ada_w.shape
    tn = 1536
    return pl.pallas_call(
        _mod_kernel,
        grid=(ld, n6 // tn),
        in_specs=[
            pl.BlockSpec((8, d), lambda l, j: (0, 0)),
            pl.BlockSpec((1, d, tn), lambda l, j: (l, 0, j)),
            pl.BlockSpec((1, 1, tn), lambda l, j: (l, 0, j)),
        ],
        out_specs=pl.BlockSpec((1, 8, tn), lambda l, j: (l, 0, j)),
        out_shape=jax.ShapeDtypeStruct((ld, 8, n6), F32),
        compiler_params=_cparams(("arbitrary", "arbitrary")),
        name="modulation",
    )(cvec, ada_w, ada_b.reshape(ld, 1, n6))


def _prenorm_kernel(x_ref, g_ref, sh_ref, sc_ref, wsh_ref, wsl_ref, h_ref, s_ref):
    x = x_ref[...]
    h = x * lax.rsqrt(jnp.mean(x * x, axis=-1, keepdims=True) + EPS) * g_ref[...]
    h = h * (1.0 + sc_ref[0]) + sh_ref[0]
    hh = h.astype(BF16)
    hl = (h - hh.astype(F32)).astype(BF16)
    h_ref[...] = hh
    wsh = wsh_ref[...]
    s_ref[...] = _dot(hh, wsh) + _dot(hl, wsh) + _dot(hh, wsl_ref[...])


def _inproj_kernel(h_ref, w_ref, p_ref):
    p_ref[...] = _dot(h_ref[...], w_ref[...]).astype(BF16)


def _in_projection(x_all, norm_g, mod3, w_main, ws_hi, ws_lo, dims):
    t, d = x_all.shape
    tm, tn = dims["tm_proj"], N_MAIN // 4
    modrow = dims["modrow"]
    hh, sm = pl.pallas_call(
        _prenorm_kernel,
        grid=(t // tm,),
        in_specs=[
            pl.BlockSpec((tm, d), lambda i: (i, 0)),
            pl.BlockSpec((1, d), lambda i: (0, 0)),
            pl.BlockSpec((1, 1, d), lambda i: (modrow(i, tm) * 6 + 0, 0, 0)),
            pl.BlockSpec((1, 1, d), lambda i: (modrow(i, tm) * 6 + 1, 0, 0)),
            pl.BlockSpec((d, N_SMALL), lambda i: (0, 0)),
            pl.BlockSpec((d, N_SMALL), lambda i: (0, 0)),
        ],
        out_specs=[pl.BlockSpec((tm, d), lambda i: (i, 0)), pl.BlockSpec((tm, N_SMALL), lambda i: (i, 0))],
        out_shape=[jax.ShapeDtypeStruct((t, d), BF16), jax.ShapeDtypeStruct((t, N_SMALL), F32)],
        compiler_params=_cparams(("arbitrary",)),
        name="pre_norm",
    )(x_all, norm_g, mod3, mod3, ws_hi, ws_lo)
    pm = pl.pallas_call(
        _inproj_kernel,
        grid=(t // tm, N_MAIN // tn),
        in_specs=[pl.BlockSpec((tm, d), lambda i, j: (i, 0)), pl.BlockSpec((d, tn), lambda i, j: (0, j))],
        out_specs=pl.BlockSpec((tm, tn), lambda i, j: (i, j)),
        out_shape=jax.ShapeDtypeStruct((t, N_MAIN), BF16),
        compiler_params=_cparams(("arbitrary", "arbitrary")),
        name="in_projection",
    )(hh, w_main)
    return pm, sm


def _seq_pos(p, i, nctx, nlat):
    is_ctx = i < nctx
    k = jnp.where(is_ctx, jnp.where(p == 0, i, nctx - 1 - i),
                  jnp.where(p == 0, i - nctx, nlat - 1 - (i - nctx)))
    return k, jnp.where(is_ctx, nctx, nlat)


def _seq_chunk(b, p, i, nctx, nlat, nb):
    k, _ = _seq_pos(p, i, nctx, nlat)
    return jnp.where(i < nctx, nb * nlat + b * nctx + k, b * nlat + k)


def _seq_out_chunk(b, p, i, nctx, nlat, nb):
    return _seq_chunk(b, 1, jnp.where(p == 0, 0, i), nctx, nlat, nb)


def _fwd_slot(i, nctx, nlat):
    return jnp.where(i < nctx, nctx - 1 - i, nctx + nlat - 1 - (i - nctx))


def _conv4(prev_ref, cur_ref, next_ref, w, b, pv, nv, rows):
    xe = jnp.concatenate([prev_ref[...].astype(F32) * pv, cur_ref[...].astype(F32),
                          next_ref[...].astype(F32) * nv], axis=0)
    n = rows + 2 * HALO
    y = (pltpu.roll(xe, 2, 0) * w[0:1] + pltpu.roll(xe, 1, 0) * w[1:2] + xe * w[2:3]
         + pltpu.roll(xe, n - 1, 0) * w[3:4])
    return y[HALO:HALO + rows] + b


def _dir_mask(p, n):
    r = lax.broadcasted_iota(jnp.int32, (n, n), 0)
    c = lax.broadcasted_iota(jnp.int32, (n, n), 1)
    sgn = 1 - 2 * p
    return (r - c) * sgn >= 0


def _ssd_kernel(xc_ref, xp_ref, xn_ref, z_ref, sm_ref, cw_ref, cb_ref, brow_ref, arow_ref,
                e_ref, dexp_ref, nrm_ref, o_ref, yf_ref, ht_ref, xbc_ref, *, nctx, nlat):
    q = SSD_Q
    p = pl.program_id(1)
    i = pl.program_id(2)
    k, n = _seq_pos(p, i, nctx, nlat)
    pv = (k > 0).astype(F32)
    nv = (k < n - 1).astype(F32)

    @pl.when(i == 0)
    def _():
        ht_ref[...] = jnp.zeros_like(ht_ref)

    @pl.when(p == 0)
    def _():
        xbc_ref[i] = _silu(_conv4(xp_ref, xc_ref, xn_ref, cw_ref[...], cb_ref[...], pv, nv, q)).astype(BF16)

    xbc = xbc_ref[jnp.where(p == 0, i, _fwd_slot(i, nctx, nlat))]
    xs = xbc[:, :SSD_INNER].astype(F32)
    bmat = xbc[:, SSD_INNER:SSD_INNER + 256]
    cmat = xbc[:, SSD_INNER + 256:]

    mask = _dir_mask(p, q)
    tri = jnp.where(mask, 1.0, 0.0).astype(BF16)

    dt_c = _softplus(sm_ref[...] + brow_ref[0])
    dta_c = dt_c * arow_ref[0]
    acum_c = _dot_mask_lhs(tri, dta_c)
    alast_c = jnp.sum(dta_c, axis=0, keepdims=True)
    acum_r = acum_c.T

    e = e_ref[...]
    stack = jnp.concatenate([dt_c, jnp.exp(acum_c), dt_c * jnp.exp(alast_c - acum_c)], axis=0)
    ex = _dot(stack.astype(BF16), e)
    dt_e, ea_e, ds_e = ex[:q], ex[q:2 * q], ex[2 * q:]
    dec_e = _dot_mask_rhs(jnp.broadcast_to(jnp.exp(alast_c), (8, 128)), e)[0:1]

    dtx = (xs * dt_e).astype(BF16)
    dsx = (xs * ds_e).astype(BF16)
    lane = lax.broadcasted_iota(jnp.int32, (q, 128), 1)
    ydiag, yoff = [], []
    for g in range(SSD_GROUPS):
        bg = bmat[:, g * 128:(g + 1) * 128]
        cg = cmat[:, g * 128:(g + 1) * 128]
        cb = _dot_nt(cg, bg)
        htg = ht_ref[g]
        yoff.append(_dot(cg, htg.astype(BF16)))
        ht_ref[g] = htg * dec_e[:, g * 256:(g + 1) * 256] + _dot_tn(bg, dsx[:, g * 256:(g + 1) * 256])
        for pr in range(2):
            hd = g * 4 + pr * 2
            pair = dtx[:, hd * 64:hd * 64 + 128]
            outs = []
            for hh in (hd, hd + 1):
                seg = acum_c[:, hh:hh + 1] - acum_r[hh:hh + 1, :]
                lm = jnp.exp(jnp.where(mask, seg, -1e30))
                outs.append(_dot((cb * lm).astype(BF16), pair))
            ydiag.append(jnp.where(lane < 64, outs[0], outs[1]))
    y = jnp.concatenate(ydiag, axis=1) + jnp.concatenate(yoff, axis=1) * ea_e

    @pl.when(p == 0)
    def _():
        yf_ref[i] = y

    @pl.when(p == 1)
    def _():
        yt = (y + yf_ref[_fwd_slot(i, nctx, nlat)] + dexp_ref[...] * xs) * _silu(z_ref[...].astype(F32))
        yn = yt * lax.rsqrt(jnp.mean(yt * yt, axis=-1, keepdims=True) + EPS) * nrm_ref[...]
        o_ref[...] = yn.astype(BF16)


def _ssd_mixer(pm, sm, prm, dims):
    t = pm.shape[0]
    nb, nctx, nlat = dims["B"], dims["C"] // SSD_Q, dims["L"] // SSD_Q
    nhalo = t // HALO
    per = SSD_Q // HALO
    ck = functools.partial(_seq_chunk, nctx=nctx, nlat=nlat, nb=nb)
    oc = functools.partial(_seq_out_chunk, nctx=nctx, nlat=nlat, nb=nb)
    full = lambda shape: pl.BlockSpec(shape, lambda b, p, i: (0,) * len(shape))
    bydir = lambda shape: pl.BlockSpec((1,) + shape, lambda b, p, i: (p,) + (0,) * len(shape))
    return pl.pallas_call(
        functools.partial(_ssd_kernel, nctx=nctx, nlat=nlat),
        grid=(nb, 2, nctx + nlat),
        in_specs=[
            pl.BlockSpec((SSD_Q, SSD_XBC), lambda b, p, i: (ck(b, p, i), C_XBC // SSD_XBC)),
            pl.BlockSpec((HALO, SSD_XBC),
                         lambda b, p, i: (jnp.maximum(ck(b, p, i) * per - 1, 0), C_XBC // SSD_XBC)),
            pl.BlockSpec((HALO, SSD_XBC),
                         lambda b, p, i: (jnp.minimum(ck(b, p, i) * per + per, nhalo - 1), C_XBC // SSD_XBC)),
            pl.BlockSpec((SSD_Q, SSD_INNER), lambda b, p, i: (ck(b, p, i), C_Z // SSD_INNER)),
            pl.BlockSpec((SSD_Q, 128), lambda b, p, i: (ck(b, p, i), p)),
            full((CONV_W, SSD_XBC)), full((1, SSD_XBC)),
            bydir((1, 128)), bydir((1, 128)),
            full((128, SSD_INNER)), full((1, SSD_INNER)), full((1, SSD_INNER)),
        ],
        out_specs=pl.BlockSpec((SSD_Q, SSD_INNER), lambda b, p, i: (oc(b, p, i), 0)),
        out_shape=jax.ShapeDtypeStruct((t, SSD_INNER), BF16),
        scratch_shapes=[pltpu.VMEM((nctx + nlat, SSD_Q, SSD_INNER), F32),
                        pltpu.VMEM((SSD_GROUPS, SSD_STATE, 256), F32),
                        pltpu.VMEM((nctx + nlat, SSD_Q, SSD_XBC), BF16)],
        compiler_params=_cparams(("arbitrary", "arbitrary", "arbitrary")),
        name="ssd_mixer",
    )(pm, pm, pm, pm, sm, prm["ssd_cw"], prm["ssd_cb"], prm["ssd_brow"], prm["ssd_arow"],
      prm["ssd_e"], prm["ssd_dexp"], prm["ssd_norm"])


def _gla_kernel(q_ref, k_ref, v_ref, r_ref, sm_ref, g2h_ref, g2l_ref, gb_ref, nrm_ref,
                o_ref, yf_ref, st_ref, *, nctx, nlat):
    qc = GLA_CHUNK
    nsub = GLA_R // qc
    p = pl.program_id(1)
    i = pl.program_id(2)

    @pl.when(i == 0)
    def _():
        st_ref[...] = jnp.zeros_like(st_ref)

    r = GLA_R
    rr = lax.broadcasted_iota(jnp.int32, (r, r), 0)
    cc = lax.broadcasted_iota(jnp.int32, (r, r), 1)
    shift = qc.bit_length() - 1
    mask = (lax.shift_right_logical(rr, shift) == lax.shift_right_logical(cc, shift)) & ((rr - cc) * (1 - 2 * p) >= 0)
    tri = jnp.where(mask, 1.0, 0.0).astype(BF16)
    scale = GLA_DK ** -0.5

    sm = sm_ref[...]
    smh = sm.astype(BF16)
    sml = (sm - smh.astype(F32)).astype(BF16)
    g2h = g2h_ref[0]
    logit = _dot(smh, g2h) + _dot(sml, g2h) + _dot(smh, g2l_ref[0]) + gb_ref[0]
    g = -_softplus(-logit) * (1.0 / GLA_GATE_NORM)
    gc = _dot_mask_lhs(tri, g)
    glast = [jnp.sum(g[s * qc:(s + 1) * qc], axis=0, keepdims=True) for s in range(nsub)]
    bcast = lambda rows_: jnp.concatenate([jnp.broadcast_to(x, (qc, x.shape[1])) for x in rows_], axis=0)
    gref = bcast([gc[s * qc + qc // 2:s * qc + qc // 2 + 1] for s in range(nsub)])
    glast_f = bcast(glast)
    qf = q_ref[...].astype(F32) * scale
    kf = k_ref[...].astype(F32)
    vb = v_ref[...]
    qe = (qf * jnp.exp(gc - gref)).astype(BF16)
    ke = (kf * jnp.exp(gref - gc)).astype(BF16)
    qg = (qf * jnp.exp(gc)).astype(BF16)
    k2 = (kf * jnp.exp(glast_f - gc)).astype(BF16)
    o_intra = []
    for h in range(GLA_HEADS):
        hs = slice(h * GLA_DK, (h + 1) * GLA_DK)
        att = jnp.where(mask, _dot_nt(qe[:, hs], ke[:, hs]), 0.0).astype(BF16)
        o_intra.append(_dot(att, vb[:, hs]))

    def sweep(order, emit):
        st = [st_ref[h] for h in range(GLA_HEADS)]
        for s in order:
            rs = slice(s * qc, (s + 1) * qc)
            dec = jnp.exp(glast[s])
            outs = []
            for h in range(GLA_HEADS):
                hs = slice(h * GLA_DK, (h + 1) * GLA_DK)
                outs.append(o_intra[h][rs] + _dot_nt(qg[rs, hs], st[h].astype(BF16)))
                st[h] = st[h] * dec[:, hs] + _dot_tn(vb[rs, hs], k2[rs, hs])
            emit(rs, jnp.concatenate(outs, axis=1))
        for h in range(GLA_HEADS):
            st_ref[h] = st[h]

    @pl.when(p == 0)
    def _():
        def emit(rs, y):
            yf_ref[i, rs, :] = y
        sweep(range(nsub), emit)

    @pl.when(p == 1)
    def _():
        slot = _fwd_slot(i, nctx, nlat)
        nrm = nrm_ref[...]

        def emit(rs, y):
            yt = y + yf_ref[slot, rs, :]
            parts = []
            for h in range(GLA_HEADS):
                hs = slice(h * GLA_DK, (h + 1) * GLA_DK)
                yh = yt[:, hs]
                parts.append(yh * lax.rsqrt(jnp.mean(yh * yh, axis=-1, keepdims=True) + EPS) * nrm[:, hs])
            o_ref[rs, :] = (jnp.concatenate(parts, axis=1) * _silu(r_ref[rs, :].astype(F32))).astype(BF16)
        sweep(range(nsub - 1, -1, -1), emit)


def _gla_mixer(pm, sm, prm, dims):
    t = pm.shape[0]
    nb, nctx, nlat = dims["B"], dims["C"] // GLA_R, dims["L"] // GLA_R
    ck = functools.partial(_seq_chunk, nctx=nctx, nlat=nlat, nb=nb)
    oc = functools.partial(_seq_out_chunk, nctx=nctx, nlat=nlat, nb=nb)
    w = GLA_HEADS * GLA_DK
    col = lambda c: pl.BlockSpec((GLA_R, w), lambda b, p, i: (ck(b, p, i), c // w))
    bydir = lambda shape: pl.BlockSpec((1,) + shape, lambda b, p, i: (p,) + (0,) * len(shape))
    return pl.pallas_call(
        functools.partial(_gla_kernel, nctx=nctx, nlat=nlat),
        grid=(nb, 2, nctx + nlat),
        in_specs=[
            col(C_GQ), col(C_GK), col(C_GV), col(C_GR),
            pl.BlockSpec((GLA_R, 128), lambda b, p, i: (ck(b, p, i), p)),
            bydir((128, w)), bydir((128, w)), bydir((1, w)),
            pl.BlockSpec((1, w), lambda b, p, i: (0, 0)),
        ],
        out_specs=pl.BlockSpec((GLA_R, w), lambda b, p, i: (oc(b, p, i), 0)),
        out_shape=jax.ShapeDtypeStruct((t, w), BF16),
        scratch_shapes=[pltpu.VMEM((nctx + nlat, GLA_R, w), F32),
                        pltpu.VMEM((GLA_HEADS, GLA_DK, GLA_DK), F32)],
        compiler_params=_cparams(("arbitrary", "arbitrary", "arbitrary")),
        name="gla_mixer",
    )(pm, pm, pm, pm, sm, prm["gla_g2h"], prm["gla_g2l"], prm["gla_gb"], prm["gla_norm"])


def _lru_kernel(xc_ref, xp_ref, xn_ref, gl_ref, cw_ref, cb_ref, wg_ref, ba_ref, bx_ref, lam_ref,
                o_ref, yf_ref, h_ref, a_scr, v_scr, as_scr, hs_scr, *, nctx, nlat):
    r, seg = LRU_R, LRU_SEG
    p = pl.program_id(1)
    i = pl.program_id(2)
    k, n = _seq_pos(p, i, nctx, nlat)
    pv = (k > 0).astype(F32)
    nv = (k < n - 1).astype(F32)

    @pl.when(i == 0)
    def _():
        h_ref[...] = jnp.zeros_like(h_ref)

    u = _conv4(xp_ref, xc_ref, xn_ref, cw_ref[...], cb_ref[...], pv, nv, r)
    ub = u.astype(BF16)
    ra, ix = [], []
    for j in range(LRU_W // 128):
        gj = _dot(ub[:, j * 128:(j + 1) * 128], wg_ref[0, j])
        ra.append(gj[:, :128])
        ix.append(gj[:, 128:])
    rg = _sigmoid(jnp.concatenate(ra, axis=1) + ba_ref[0])
    ig = _sigmoid(jnp.concatenate(ix, axis=1) + bx_ref[0])
    log_a = (-LRU_C * _softplus(-lam_ref[0])) * rg
    a_all = jnp.exp(log_a)
    th = jnp.tanh(log_a)
    v_all = u * ig * jnp.sqrt(-2.0 * th / (1.0 - th))
    nslab = LRU_W // 128
    for j in range(nslab):
        a_scr[j] = a_all[:, j * 128:(j + 1) * 128]
        v_scr[j] = v_all[:, j * 128:(j + 1) * 128]

    def scan(order, seg_order):
        for j in range(nslab):
            ls = slice(j * 128, (j + 1) * 128)
            acc_a = jnp.ones((8, 128), F32)
            acc_h = jnp.zeros((8, 128), F32)
            for kk in order:
                ak = a_scr[j, pl.ds(kk, 8, stride=seg), :]
                vk = v_scr[j, pl.ds(kk, 8, stride=seg), :]
                acc_h = ak * acc_h + vk
                acc_a = ak * acc_a
                as_scr[j, kk * 8:(kk + 1) * 8, :] = acc_a
                hs_scr[j, kk * 8:(kk + 1) * 8, :] = acc_h
            carry = h_ref[0:1, ls]
            rows = [None] * 8
            for s in seg_order:
                rows[s] = carry
                carry = acc_a[s:s + 1, :] * carry + acc_h[s:s + 1, :]
            h_ref[0:1, ls] = carry
            cin = jnp.concatenate(rows, axis=0)
            for kk in order:
                ks = slice(kk * 8, (kk + 1) * 8)
                hs_scr[j, ks, :] = hs_scr[j, ks, :] + as_scr[j, ks, :] * cin

    @pl.when(p == 0)
    def _():
        scan(range(seg), range(8))

    @pl.when(p == 1)
    def _():
        scan(range(seg - 1, -1, -1), range(7, -1, -1))

    def natural(s):
        return jnp.concatenate([hs_scr[j, pl.ds(s, seg, stride=8), :] for j in range(nslab)], axis=1)

    @pl.when(p == 0)
    def _():
        for s in range(8):
            yf_ref[i, s * seg:(s + 1) * seg, :] = natural(s)

    @pl.when(p == 1)
    def _():
        slot = _fwd_slot(i, nctx, nlat)
        c0 = math.sqrt(2.0 / math.pi)
        for s in range(8):
            rs = slice(s * seg, (s + 1) * seg)
            gt = gl_ref[rs, :].astype(F32)
            gelu = 0.5 * gt * (1.0 + jnp.tanh(c0 * (gt + 0.044715 * (gt * gt * gt))))
            o_ref[rs, :] = ((natural(s) + yf_ref[slot, rs, :]) * gelu).astype(BF16)


def _lru_mixer(pm, prm, dims):
    t = pm.shape[0]
    nb, nctx, nlat = dims["B"], dims["C"] // LRU_R, dims["L"] // LRU_R
    nhalo = t // HALO
    per = LRU_R // HALO
    ck = functools.partial(_seq_chunk, nctx=nctx, nlat=nlat, nb=nb)
    oc = functools.partial(_seq_out_chunk, nctx=nctx, nlat=nlat, nb=nb)
    w = LRU_W
    full = lambda shape: pl.BlockSpec(shape, lambda b, p, i: (0,) * len(shape))
    bydir = lambda shape: pl.BlockSpec((1,) + shape, lambda b, p, i: (p,) + (0,) * len(shape))
    return pl.pallas_call(
        functools.partial(_lru_kernel, nctx=nctx, nlat=nlat),
        grid=(nb, 2, nctx + nlat),
        in_specs=[
            pl.BlockSpec((LRU_R, w), lambda b, p, i: (ck(b, p, i), C_LX // w)),
            pl.BlockSpec((HALO, w), lambda b, p, i: (jnp.maximum(ck(b, p, i) * per - 1, 0), C_LX // w)),
            pl.BlockSpec((HALO, w), lambda b, p, i: (jnp.minimum(ck(b, p, i) * per + per, nhalo - 1), C_LX // w)),
            pl.BlockSpec((LRU_R, w), lambda b, p, i: (ck(b, p, i), C_LG // w)),
            full((CONV_W, w)), full((1, w)),
            bydir((w // 128, 128, 256)), bydir((1, w)), bydir((1, w)), bydir((1, w)),
        ],
        out_specs=pl.BlockSpec((LRU_R, w), lambda b, p, i: (oc(b, p, i), 0)),
        out_shape=jax.ShapeDtypeStruct((t, w), BF16),
        scratch_shapes=[pltpu.VMEM((nctx + nlat, LRU_R, w), F32),
                        pltpu.VMEM((8, w), F32),
                        pltpu.VMEM((w // 128, LRU_R, 128), F32), pltpu.VMEM((w // 128, LRU_R, 128), F32),
                        pltpu.VMEM((w // 128, LRU_R, 128), F32), pltpu.VMEM((w // 128, LRU_R, 128), F32)],
        compiler_params=_cparams(("arbitrary", "arbitrary", "arbitrary")),
        name="lru_mixer",
    )(pm, pm, pm, pm, prm["lru_cw"], prm["lru_cb"], prm["lru_wg"], prm["lru_ba"], prm["lru_bx"],
      prm["lru_lam"])


def _qkprep_kernel(q_ref, k_ref, cos_ref, sin_ref, qn_ref, kn_ref, qo_ref, ko_ref, *, nlat_tiles):
    i = pl.program_id(0)
    is_ctx = i >= nlat_tiles
    cos = jnp.where(is_ctx, 1.0, cos_ref[...])
    sin = jnp.where(is_ctx, 0.0, sin_ref[...])
    lane = lax.broadcasted_iota(jnp.int32, cos.shape, 1)
    first = (lane % 64) < 32

    def prep(x, g, scale):
        xn = x * lax.rsqrt(jnp.mean(x * x, axis=-1, keepdims=True) + EPS) * g
        sw = jnp.where(first, pltpu.roll(xn, 96, 1), pltpu.roll(xn, 32, 1))
        return ((xn * cos + sw * sin) * scale).astype(BF16)

    qn, kn = qn_ref[...], kn_ref[...]
    for h in range(ATT_HEADS):
        hs = slice(h * ATT_HEAD_DIM, (h + 1) * ATT_HEAD_DIM)
        qo_ref[:, hs] = prep(q_ref[:, hs].astype(F32), qn, ATT_HEAD_DIM ** -0.5)
    for h in range(ATT_KV_HEADS):
        hs = slice(h * ATT_HEAD_DIM, (h + 1) * ATT_HEAD_DIM)
        ko_ref[:, hs] = prep(k_ref[:, hs].astype(F32), kn, 1.0)


def _qk_prep(pm, cos, sin, prm, dims):
    t = pm.shape[0]
    tm = dims["tm_qk"]
    nlat_tiles = dims["B"] * dims["L"] // tm
    per_seq = dims["L"] // tm
    return pl.pallas_call(
        functools.partial(_qkprep_kernel, nlat_tiles=nlat_tiles),
        grid=(t // tm,),
        in_specs=[
            pl.BlockSpec((tm, 512), lambda i: (i, C_AQ // 512)),
            pl.BlockSpec((tm, 256), lambda i: (i, C_AK // 256)),
            pl.BlockSpec((tm, 128), lambda i: (i % per_seq, 0)),
            pl.BlockSpec((tm, 128), lambda i: (i % per_seq, 0)),
            pl.BlockSpec((1, 128), lambda i: (0, 0)),
            pl.BlockSpec((1, 128), lambda i: (0, 0)),
        ],
        out_specs=[pl.BlockSpec((tm, 512), lambda i: (i, 0)), pl.BlockSpec((tm, 256), lambda i: (i, 0))],
        out_shape=[jax.ShapeDtypeStruct((t, 512), BF16), jax.ShapeDtypeStruct((t, 256), BF16)],
        compiler_params=_cparams(("arbitrary",)),
        name="qk_prep",
    )(pm, pm, cos, sin, prm["att_qnorm"], prm["att_knorm"])


def _attn_kernel(q_ref, kc_ref, vc_ref, *rest, tk, nlat_chunks):
    if nlat_chunks:
        kl_ref, vl_ref, o_ref = rest
    else:
        (o_ref,) = rest
    tq = q_ref.shape[0]
    dh = ATT_HEAD_DIM
    q = jnp.concatenate([q_ref[:, :dh], q_ref[:, dh:]], axis=0)

    chunks = [(kc_ref, vc_ref, slice(None))]
    chunks += [(kl_ref, vl_ref, slice(c * tk, (c + 1) * tk)) for c in range(nlat_chunks)]
    m = l = acc = None
    s_next = _dot_nt(q, kc_ref[...])
    for c, (_, v_ref, rows) in enumerate(chunks):
        s = s_next
        if c + 1 < len(chunks):
            k_ref, _, nrows = chunks[c + 1]
            s_next = _dot_nt(q, k_ref[nrows, :])
        mx = jnp.max(s, axis=-1, keepdims=True)
        if m is None:
            m = mx
            pmat = jnp.exp(s - m)
            l = jnp.sum(pmat, axis=-1, keepdims=True)
            acc = _dot(pmat.astype(BF16), v_ref[rows, :])
        else:
            m_new = jnp.maximum(m, mx)
            alpha = jnp.exp(m - m_new)
            pmat = jnp.exp(s - m_new)
            l = alpha * l + jnp.sum(pmat, axis=-1, keepdims=True)
            acc = alpha * acc + _dot(pmat.astype(BF16), v_ref[rows, :])
            m = m_new
    o = acc * (1.0 / l)
    o_ref[...] = jnp.concatenate([o[:tq], o[tq:]], axis=1).astype(BF16)


def _attention(qn, kn, pm, dims, latent):
    nb, l, c = dims["B"], dims["L"], dims["C"]
    dh = ATT_HEAD_DIM
    tq = dims["tq"] if latent else c
    nq = (l if latent else c) // tq
    ctx_blk = nb * l // c
    qrow = (lambda b, i: b * nq + i) if latent else (lambda b, i: ctx_blk + b)
    av = C_AV // dh
    in_specs = [
        pl.BlockSpec((tq, 2 * dh), lambda b, g, i: (qrow(b, i), g)),
        pl.BlockSpec((c, dh), lambda b, g, i: (ctx_blk + b, g)),
        pl.BlockSpec((c, dh), lambda b, g, i: (ctx_blk + b, av + g)),
    ]
    args = [qn, kn, pm]
    if latent:
        in_specs += [pl.BlockSpec((l, dh), lambda b, g, i: (b, g)),
                     pl.BlockSpec((l, dh), lambda b, g, i: (b, av + g))]
        args += [kn, pm]
    tk = _pick(l, (1024, 512, 256))
    return pl.pallas_call(
        functools.partial(_attn_kernel, tk=tk, nlat_chunks=(l // tk if latent else 0)),
        grid=(nb, ATT_KV_HEADS, nq),
        in_specs=in_specs,
        out_specs=pl.BlockSpec((tq, 2 * dh), lambda b, g, i: (b * nq + i, g)),
        out_shape=jax.ShapeDtypeStruct((nb * (l if latent else c), ATT_HEADS * dh), BF16),
        compiler_params=_cparams(("arbitrary", "arbitrary", "arbitrary")),
        name="attention_latent" if latent else "attention_context",
    )(*args)


def _route_weights(lg):
    lane = lax.broadcasted_iota(jnp.int32, lg.shape, 1)
    neg = -1e30
    is_g = lane < MOE_GROUPS
    gl = jnp.where(is_g, lg, neg)
    gmax = jnp.max(gl, axis=-1, keepdims=True)
    gsum = jnp.sum(jnp.where(is_g, jnp.exp(gl - gmax), 0.0), axis=-1, keepdims=True)
    gi = jnp.min(jnp.where(is_g & (gl == gmax), lane, 1 << 20), axis=-1, keepdims=True)
    pg_sel = 1.0 / gsum
    lo = MOE_GROUPS + gi * MOE_PER_GROUP
    in_grp = (lane >= lo) & (lane < lo + MOE_PER_GROUP)
    el = jnp.where(in_grp, lg, neg)
    emax = jnp.max(el, axis=-1, keepdims=True)
    ex = jnp.where(in_grp, jnp.exp(el - emax), 0.0)
    pe = ex / jnp.sum(ex, axis=-1, keepdims=True)
    pe_m = jnp.where(in_grp, pe, -1.0)
    v1 = jnp.max(pe_m, axis=-1, keepdims=True)
    i1 = jnp.min(jnp.where(pe_m == v1, lane, 1 << 20), axis=-1, keepdims=True)
    pe_m2 = jnp.where(lane == i1, -1.0, pe_m)
    v2 = jnp.max(pe_m2, axis=-1, keepdims=True)
    i2 = jnp.min(jnp.where(pe_m2 == v2, lane, 1 << 20), axis=-1, keepdims=True)
    tot = v1 + v2
    w = jnp.where(lane == i1, v1 / tot, 0.0) + jnp.where(lane == i2, v2 / tot, 0.0)
    return w * pg_sel, gi


def _merge_kernel(x_ref, ya_ref, yb_ref, yc_ref, ydl_ref, ydc_ref, mg_ref, g1_ref, sh_ref, sc_ref, n2_ref,
                  wb_ref, wo_ref, rwh_ref, rwl_ref, rb_ref, xo_ref, h_ref, plan_ref, cnt_ref, run_ref,
                  *, nlat_tiles):
    i = pl.program_id(0)
    tm, d = x_ref.shape

    @pl.when(i == 0)
    def _():
        run_ref[...] = jnp.zeros_like(run_ref)

    yd = jnp.where(i < nlat_tiles, ydl_ref[...], ydc_ref[...])
    acc = None
    for nbr, y in enumerate((ya_ref[...], yb_ref[...], yc_ref[...], yd)):
        gate = _sigmoid(mg_ref[:, nbr * d:(nbr + 1) * d].astype(F32))
        term = gate * _dot(y, wb_ref[nbr])
        acc = term if acc is None else acc + term
    xn = x_ref[...] + g1_ref[0] * _dot(acc.astype(BF16), wo_ref[...])
    xo_ref[...] = xn
    h = xn * lax.rsqrt(jnp.mean(xn * xn, axis=-1, keepdims=True) + EPS) * n2_ref[...]
    h = h * (1.0 + sc_ref[0]) + sh_ref[0]
    hh = h.astype(BF16)
    hl = (h - hh.astype(F32)).astype(BF16)
    rwh = rwh_ref[...]
    lg = _dot(hh, rwh) + _dot(hl, rwh) + _dot(hh, rwl_ref[...]) + rb_ref[...]
    rw, gi = _route_weights(lg)
    h_ref[:, :d] = h
    h_ref[:, d:] = rw

    lane = lax.broadcasted_iota(jnp.int32, (tm, 128), 1)
    onehot = jnp.where(lane == gi, 1.0, 0.0)
    r = lax.broadcasted_iota(jnp.int32, (tm, tm), 0)
    c = lax.broadcasted_iota(jnp.int32, (tm, tm), 1)
    before = jnp.where(c < r, 1.0, 0.0).astype(BF16)
    run = run_ref[0:1, :]
    rank = jnp.sum(onehot * (_dot(before, onehot.astype(BF16)) + run), axis=-1, keepdims=True)
    plan_ref[...] = jnp.where(lane == 0, rank, jnp.where(lane == 1, gi.astype(F32), 0.0))
    run = run + jnp.sum(onehot, axis=0, keepdims=True)
    run_ref[0:1, :] = run
    cnt_ref[...] = jnp.broadcast_to(run, cnt_ref.shape)


def _merge(x_all, ya, yb, yc, yd_lat, yd_ctx, pm, mod3, prm, dims, n_tiles):
    d = x_all.shape[1]
    tm = dims["tm_merge"]
    rows = n_tiles * tm
    nlat_tiles = dims["B"] * dims["L"] // tm
    modrow = dims["modrow"]
    row = lambda shape: pl.BlockSpec(shape, lambda i: (i, 0))
    full = lambda shape: pl.BlockSpec(shape, lambda i: (0,) * len(shape))
    modspec = lambda comp: pl.BlockSpec((1, 1, d), lambda i: (modrow(i, tm) * 6 + comp, 0, 0))
    return pl.pallas_call(
        functools.partial(_merge_kernel, nlat_tiles=nlat_tiles),
        grid=(n_tiles,),
        in_specs=[
            row((tm, d)), row((tm, BRANCH_W)), row((tm, BRANCH_W)), row((tm, BRANCH_W)),
            pl.BlockSpec((tm, BRANCH_W), lambda i: (jnp.minimum(i, nlat_tiles - 1), 0)),
            pl.BlockSpec((tm, BRANCH_W), lambda i: (jnp.maximum(i - nlat_tiles, 0), 0)),
            pl.BlockSpec((tm, 4 * d), lambda i: (i, C_MG // (4 * d))),
            modspec(2), modspec(3), modspec(4), full((1, d)),
            full((4, BRANCH_W, d)), full((d, d)), full((d, 128)), full((d, 128)), full((1, 128)),
        ],
        out_specs=[row((tm, d)), row((tm, d + 128)), row((tm, 128)), full((8, 128))],
        out_shape=[jax.ShapeDtypeStruct((rows, d), F32), jax.ShapeDtypeStruct((rows, d + 128), F32),
                   jax.ShapeDtypeStruct((rows, 128), F32), jax.ShapeDtypeStruct((8, 128), F32)],
        scratch_shapes=[pltpu.VMEM((8, 128), F32)],
        compiler_params=_cparams(("arbitrary",)),
        name="merge",
    )(x_all, ya, yb, yc, yd_lat, yd_ctx, pm, mod3, mod3, mod3, prm["norm2"], prm["w_branch"], prm["w_out"],
      prm["rw_hi"], prm["rw_lo"], prm["rb"])


DMA_UNROLL = 8


def _row_copies(n, make):
    def issue(blk, carry):
        for u in range(DMA_UNROLL):
            make(blk * DMA_UNROLL + u).start()
        return carry
    lax.fori_loop(0, n // DMA_UNROLL, issue, 0)

    def drain(blk, carry):
        for u in range(DMA_UNROLL):
            make(0).wait()
        return carry
    lax.fori_loop(0, n // DMA_UNROLL, drain, 0)


def _scatter_kernel(pos_ref, ztile_ref, src_ref, dst_ref, zero_ref, sem, zsem, *, tm):
    @pl.when(pl.program_id(0) == 0)
    def _():
        zero_ref[...] = jnp.zeros_like(zero_ref)
        for k in range(ztile_ref.shape[0]):
            cp = pltpu.make_async_copy(
                zero_ref, dst_ref.at[pl.ds(pl.multiple_of(ztile_ref[k] * tm, tm), tm)], zsem)
            cp.start()
            cp.wait()

    def make(r):
        return pltpu.make_async_copy(src_ref.at[pl.ds(r, 1)], dst_ref.at[pl.ds(pos_ref[r], 1)], sem)
    _row_copies(tm, make)


def _moe_scatter(pos, ztile, hext, n_rows, sorted_rows, tm):
    w = hext.shape[1]
    return pl.pallas_call(
        functools.partial(_scatter_kernel, tm=tm),
        grid=(n_rows // tm,),
        in_specs=[pl.BlockSpec((tm,), lambda i: (i,), memory_space=pltpu.SMEM),
                  pl.BlockSpec(memory_space=pltpu.SMEM),
                  pl.BlockSpec((tm, w), lambda i: (i, 0))],
        out_specs=pl.BlockSpec(memory_space=pl.ANY),
        out_shape=jax.ShapeDtypeStruct((sorted_rows, w), F32),
        scratch_shapes=[pltpu.VMEM((tm, w), F32), pltpu.SemaphoreType.DMA(()), pltpu.SemaphoreType.DMA(())],
        compiler_params=_cparams(("arbitrary",)),
        name="moe_scatter",
    )(pos, ztile, hext)


def _moe_ffn_kernel(tg_ref, h_ref, w1_ref, w3_ref, w2_ref, o_ref, hb_ref):
    i = pl.program_id(0)
    j = pl.program_id(1)
    d = o_ref.shape[1]

    @pl.when(j == 0)
    def _():
        hb_ref[...] = h_ref[:, :d].astype(BF16)
        o_ref[...] = jnp.zeros_like(o_ref)

    h = hb_ref[...]
    a = _silu(_dot(h, w1_ref[0].astype(BF16))) * _dot(h, w3_ref[0].astype(BF16))
    y = _dot(a.astype(BF16), w2_ref[0].astype(BF16))
    rw = h_ref[:, d:]
    lane = lax.broadcasted_iota(jnp.int32, rw.shape, 1)
    e = tg_ref[i] * MOE_PER_GROUP + j
    we = jnp.sum(jnp.where(lane == e + MOE_GROUPS, rw, 0.0), axis=-1, keepdims=True)
    o_ref[...] += we * y


def _moe_ffn(tile_group, xs, layer, params, tm):
    rows, w = xs.shape
    d = w - 128
    stacked = lambda a: a.reshape((a.shape[0] * a.shape[1],) + a.shape[2:])
    expert = lambda i, j, tg: (layer * MOE_EXPERTS + tg[i] * MOE_PER_GROUP + j, 0, 0)
    return pl.pallas_call(
        _moe_ffn_kernel,
        grid_spec=pltpu.PrefetchScalarGridSpec(
            num_scalar_prefetch=1,
            grid=(rows // tm, MOE_PER_GROUP),
            in_specs=[
                pl.BlockSpec((tm, w), lambda i, j, tg: (i, 0)),
                pl.BlockSpec((1, d, MOE_FF), expert),
                pl.BlockSpec((1, d, MOE_FF), expert),
                pl.BlockSpec((1, MOE_FF, d), expert),
            ],
            out_specs=pl.BlockSpec((tm, d), lambda i, j, tg: (i, 0)),
            scratch_shapes=[pltpu.VMEM((tm, d), BF16)],
        ),
        out_shape=jax.ShapeDtypeStruct((rows, d), F32),
        compiler_params=_cparams(("arbitrary", "arbitrary")),
        name="moe_ffn",
    )(tile_group, xs, stacked(params["exp_w1"]), stacked(params["exp_w3"]), stacked(params["exp_w2"]))


def _combine_kernel(pos_ref, x_ref, g2_ref, ys_ref, o_ref, buf_ref, sem):
    tm = x_ref.shape[0]

    def make(r):
        return pltpu.make_async_copy(ys_ref.at[pl.ds(pos_ref[r], 1)], buf_ref.at[pl.ds(r, 1)], sem)
    _row_copies(tm, make)
    o_ref[...] = x_ref[...] + g2_ref[0] * buf_ref[...]


def _moe_combine(pos, x_mid, ys, mod3, dims, n_rows, tm):
    d = x_mid.shape[1]
    modrow = dims["modrow"]
    return pl.pallas_call(
        _combine_kernel,
        grid=(n_rows // tm,),
        in_specs=[pl.BlockSpec((tm,), lambda i: (i,), memory_space=pltpu.SMEM),
                  pl.BlockSpec((tm, d), lambda i: (i, 0)),
                  pl.BlockSpec((1, 1, d), lambda i: (modrow(i, tm) * 6 + 5, 0, 0)),
                  pl.BlockSpec(memory_space=pl.ANY)],
        out_specs=pl.BlockSpec((tm, d), lambda i: (i, 0)),
        out_shape=jax.ShapeDtypeStruct((n_rows, d), F32),
        scratch_shapes=[pltpu.VMEM((tm, d), F32), pltpu.SemaphoreType.DMA(())],
        compiler_params=_cparams(("arbitrary",)),
        name="moe_combine",
    )(pos, x_mid, mod3, ys)


def _moe(x_mid, hext, plan, cnt, mod3, layer, params, dims, n_rows):
    tm = dims["tm_moe"]
    counts = cnt[0, :MOE_GROUPS].astype(jnp.int32)
    ends = jnp.cumsum((counts + tm - 1) // tm * tm)
    starts = ends - (counts + tm - 1) // tm * tm
    gid = plan[:, 1].astype(jnp.int32)
    pos = plan[:, 0].astype(jnp.int32)
    for g in range(MOE_GROUPS):
        pos = pos + jnp.where(gid == g, starts[g], 0)
    n_tiles = n_rows // tm + MOE_GROUPS
    tile_group = jnp.minimum(jnp.sum(jnp.arange(n_tiles)[:, None] * tm >= ends[None, :], axis=1),
                             MOE_GROUPS - 1).astype(jnp.int32)
    ztile = jnp.concatenate([jnp.maximum(ends // tm - 1, 0),
                             jnp.arange(n_tiles - MOE_GROUPS, n_tiles)]).astype(jnp.int32)
    xs = _moe_scatter(pos, ztile, hext, n_rows, n_tiles * tm, tm)
    ys = _moe_ffn(tile_group, xs, layer, params, tm)
    return _moe_combine(pos, x_mid, ys, mod3, dims, n_rows, tm)


def _hi_lo(w):
    hi = w.astype(BF16)
    return hi, (w - hi.astype(F32)).astype(BF16)


def _layer_params(l, p):
    d = p["w_in"].shape[1]
    w_in = p["w_in"][l]
    o = 0
    cols = {}
    for name, size in (("z", 512), ("xbc", 1024), ("dt", 16), ("gq", 512), ("gk", 512), ("gv", 512),
                       ("g1", 32), ("gr", 512), ("lx", 512), ("lg", 512), ("aq", 512), ("ak", 256),
                       ("av", 256), ("mg", 4 * d)):
        cols[name] = w_in[:, o:o + size]
        o += size
    out = {}
    out["w_main"] = jnp.concatenate([cols[n] for n in ("mg", "xbc", "z", "gq", "gk", "gv", "gr", "lx", "lg",
                                                        "aq", "ak", "av")], axis=1).astype(BF16)
    zpad = jnp.zeros((d, 128 - 8 - GLA_RANK), F32)
    w_small = jnp.concatenate([cols["dt"][:, :8], cols["g1"][:, :GLA_RANK], zpad,
                               cols["dt"][:, 8:], cols["g1"][:, GLA_RANK:], zpad], axis=1)
    out["ws_hi"], out["ws_lo"] = _hi_lo(w_small)
    out["norm1"] = p["norm1"][l][None]
    out["norm2"] = p["norm2"][l][None]

    out["ssd_cw"] = p["ssd_conv_w"][l]
    out["ssd_cb"] = p["ssd_conv_b"][l][None]
    pad8 = lambda v: jnp.pad(v, ((0, 0), (0, 128 - SSD_HEADS)))
    brow = pad8(p["ssd_dt_bias"][l])
    arow = pad8(-jnp.exp(p["ssd_a_log"][l]))
    out["ssd_brow"], out["ssd_arow"] = brow[:, None, :], arow[:, None, :]
    head_of_lane = jnp.arange(SSD_INNER) // SSD_HEAD_DIM
    out["ssd_e"] = (jnp.arange(128)[:, None] == head_of_lane[None, :]).astype(BF16)
    out["ssd_dexp"] = jnp.repeat(p["ssd_d"][l], SSD_HEAD_DIM)[None]
    out["ssd_norm"] = p["ssd_norm"][l][None]

    g2 = jnp.zeros((2, 128, GLA_HEADS * GLA_DK), F32).at[:, SM_G1:SM_G1 + GLA_RANK].set(p["gla_g2"][l])
    out["gla_g2h"], out["gla_g2l"] = _hi_lo(g2)
    out["gla_gb"] = p["gla_gb"][l][:, None, :]
    out["gla_norm"] = p["gla_norm"][l][None]

    def pairs(w):
        w = w.reshape(2, LRU_BLOCKS // 2, 2, LRU_BLOCK, LRU_BLOCK)
        z = jnp.zeros_like(w[:, :, 0])
        top = jnp.concatenate([w[:, :, 0], z], axis=-1)
        bot = jnp.concatenate([z, w[:, :, 1]], axis=-1)
        return jnp.concatenate([top, bot], axis=-2)
    out["lru_wg"] = jnp.concatenate([pairs(p["lru_wa"][l]), pairs(p["lru_wx"][l])], axis=-1).astype(BF16)
    out["lru_cw"] = p["lru_conv_w"][l]
    out["lru_cb"] = p["lru_conv_b"][l][None]
    out["lru_ba"] = p["lru_ba"][l][:, None, :]
    out["lru_bx"] = p["lru_bx"][l][:, None, :]
    out["lru_lam"] = p["lru_lambda"][l][:, None, :]

    out["att_qnorm"] = p["att_qnorm"][l][None]
    out["att_knorm"] = p["att_knorm"][l][None]
    out["w_branch"] = p["w_branch"][l].astype(BF16)
    out["w_out"] = p["w_out"][l].astype(BF16)
    rw = jnp.concatenate([p["router_wg"][l], p["router_we"][l],
                          jnp.zeros((d, 128 - MOE_GROUPS - MOE_EXPERTS), F32)], axis=1)
    out["rw_hi"], out["rw_lo"] = _hi_lo(rw)
    out["rb"] = jnp.concatenate([p["router_bg"][l], p["router_be"][l],
                                 jnp.zeros((128 - MOE_GROUPS - MOE_EXPERTS,), F32)])[None]
    return out


def _rope_tables(l):
    f = ATT_HEAD_DIM // 4
    inv = ROPE_THETA ** (-jnp.arange(f, dtype=F32) / f)
    tpos = jnp.arange(l, dtype=jnp.int32)
    row = (tpos // GRID_W).astype(F32)[:, None] * inv
    col = (tpos % GRID_W).astype(F32)[:, None] * inv
    cos = jnp.concatenate([jnp.cos(row), jnp.cos(row), jnp.cos(col), jnp.cos(col)], axis=1)
    sin = jnp.concatenate([-jnp.sin(row), jnp.sin(row), -jnp.sin(col), jnp.sin(col)], axis=1)
    return cos, sin


def _pick(n, cands):
    for c in cands:
        if n % c == 0:
            return c
    raise ValueError(f"no tile size for {n}")


def kernel(x, c, ctx, c_ctx, ada_w, ada_b, norm1, norm2, w_in, ssd_conv_w, ssd_conv_b, ssd_dt_bias, ssd_a_log, ssd_d, ssd_norm, gla_g2, gla_gb, gla_norm, lru_conv_w, lru_conv_b, lru_wa, lru_ba, lru_wx, lru_bx, lru_lambda, att_qnorm, att_knorm, w_branch, w_out, router_wg, router_bg, router_we, router_be, exp_w1, exp_w3, exp_w2):
    nb, l, d = x.shape
    c_len = ctx.shape[1]
    depth = ada_w.shape[0]
    tl, tc = nb * l, nb * c_len
    assert l % LRU_R == 0 and c_len % LRU_R == 0 and nb + 1 <= 8 and l % GRID_W == 0
    params = dict(norm1=norm1, norm2=norm2, w_in=w_in, ssd_conv_w=ssd_conv_w, ssd_conv_b=ssd_conv_b,
                  ssd_dt_bias=ssd_dt_bias, ssd_a_log=ssd_a_log, ssd_d=ssd_d, ssd_norm=ssd_norm,
                  gla_g2=gla_g2, gla_gb=gla_gb, gla_norm=gla_norm, lru_conv_w=lru_conv_w,
                  lru_conv_b=lru_conv_b, lru_wa=lru_wa, lru_ba=lru_ba, lru_wx=lru_wx, lru_bx=lru_bx,
                  lru_lambda=lru_lambda, att_qnorm=att_qnorm, att_knorm=att_knorm, w_branch=w_branch,
                  w_out=w_out, router_wg=router_wg, router_bg=router_bg, router_we=router_we,
                  router_be=router_be, exp_w1=exp_w1, exp_w3=exp_w3, exp_w2=exp_w2)

    tile = _pick(math.gcd(l, tc), (1024, 512, 256))

    def modrow(i, tm):
        return jnp.where(i < tl // tm, i // (l // tm), nb)

    dims = dict(B=nb, L=l, C=c_len, modrow=modrow, tm_proj=tile, tm_merge=min(tile, 512), tm_moe=tile,
                tm_qk=min(tile, 512), tq=_pick(l, (512, 256)))

    cvec = jnp.zeros((8, d), F32).at[:nb].set(c).at[nb].set(c_ctx)
    mod_all = _modulation(cvec, ada_w, ada_b)
    cos, sin = _rope_tables(l)
    x_all = jnp.concatenate([x.reshape(tl, d), ctx.reshape(tc, d)], axis=0)

    for layer in range(depth):
        last = layer == depth - 1
        prm = _layer_params(layer, params)
        mod3 = mod_all[layer].reshape(8 * 6, 1, d)
        pm, sm = _in_projection(x_all, prm["norm1"], mod3, prm["w_main"], prm["ws_hi"], prm["ws_lo"], dims)
        ya = _ssd_mixer(pm, sm, prm, dims)
        yb = _gla_mixer(pm, sm, prm, dims)
        yc = _lru_mixer(pm, prm, dims)
        qn, kn = _qk_prep(pm, cos, sin, prm, dims)
        yd = _attention(qn, kn, pm, dims, latent=True)
        yd_ctx = yd if last else _attention(qn, kn, pm, dims, latent=False)
        n_rows = tl if last else tl + tc
        x_mid, hext, plan, cnt = _merge(x_all, ya, yb, yc, yd, yd_ctx, pm, mod3, prm, dims,
                                        n_rows // dims["tm_merge"])
        x_all = _moe(x_mid, hext, plan, cnt, mod3, layer, params, dims, n_rows)
    return x_all.reshape(nb, l, d)
```
